```python
import math
import jax, jax.numpy as jnp
from jax import lax
import numpy as np


D_MODEL = 1024
BATCH = 8
SEQ = 2048
DEPTH = 4
DEC_BATCH = 32
DEC_SEQ = 4
PAST_LEN = 8192
PAGE_SIZE = 128

N_MIXERS = 3
N_A_LAYERS = (DEPTH + 2) // 3
N_B_LAYERS = (DEPTH + 1) // 3
N_C_LAYERS = DEPTH // 3
CHUNK = 128
D_GATE = D_MODEL
A_GROUPS = 4
POOL_WINDOWS = (2, 4, 8, 16)
B_GROUPS = len(POOL_WINDOWS)
POOL_BUF = max(POOL_WINDOWS) - 1
N_HEADS = 8
HEAD_DIM = D_MODEL // N_HEADS // 2
ROT_DIM = HEAD_DIM // 4
ROPE_THETA = 500000.0
Q_BLOCK = 128
N_EXPERTS = 16
N_EXPERT_GROUPS = 4
TOP_K = 2
D_EXPERT = 512
EPS = 1e-6

kernel_name = 'hybrid_gmlp_pool_diffattn_moe_step'


def rmsnorm(x, g):
    xf = x.astype(jnp.float32)
    y = xf * lax.rsqrt(jnp.mean(xf * xf, axis=-1, keepdims=True) + EPS)
    return (y * g.astype(jnp.float32)).astype(x.dtype)


def layernorm(x, g, b):
    xf = x.astype(jnp.float32)
    mu = jnp.mean(xf, axis=-1, keepdims=True)
    var = jnp.mean(jnp.square(xf - mu), axis=-1, keepdims=True)
    y = (xf - mu) * lax.rsqrt(var + EPS)
    return (y * g.astype(jnp.float32) + b.astype(jnp.float32)).astype(x.dtype)


def ada_mod(c, w, b):
    m = jax.nn.silu(c) @ w + b
    return jnp.split(m, 6, axis=-1)


def modulate(x, g, shift, scale):
    return rmsnorm(x, g) * (1 + scale[:, None, :]) + shift[:, None, :]


def chunk_gmlp(h, w_in, ln_g, ln_b, w_s, b_s, w_out):
    bsz, t, _ = h.shape
    z = jax.nn.gelu(h @ w_in, approximate=False)
    u, v = jnp.split(z, 2, axis=-1)
    v = layernorm(v, ln_g, ln_b)
    n_chunks = -(-t // CHUNK)
    vp = jnp.pad(v, ((0, 0), (0, n_chunks * CHUNK - t), (0, 0)))
    vp = vp.reshape(bsz, n_chunks, CHUNK, A_GROUPS, D_GATE // A_GROUPS)
    causal = jnp.tril(jnp.ones((CHUNK, CHUNK), dtype=bool))
    w_c = jnp.where(causal[None], w_s, 0)
    mixed = jnp.einsum('gts,bnsgc->bntgc', w_c, vp) + jnp.transpose(b_s)[None, None, :, :, None]
    mixed = mixed.reshape(bsz, n_chunks * CHUNK, D_GATE)[:, :t]
    y = (u * mixed) @ w_out
    start = ((t - 1) // CHUNK) * CHUNK
    return y, v[:, start:]


def pool_mixer(h, prev, pos0, w_grp, scale):
    bsz, t, _ = h.shape
    cg = D_MODEL // B_GROUPS
    ext = jnp.concatenate([prev, h], axis=1)
    cs = jnp.cumsum(ext.astype(jnp.float32), axis=1)
    cs = jnp.concatenate([jnp.zeros_like(cs[:, :1]), cs], axis=1)
    end = POOL_BUF + 1
    pos = pos0 + jnp.arange(t)
    outs = []
    for g, w in enumerate(POOL_WINDOWS):
        ch = slice(g * cg, (g + 1) * cg)
        wsum = cs[:, end:end + t, ch] - cs[:, end - w:end - w + t, ch]
        cnt = jnp.minimum(pos + 1, w).astype(jnp.float32)[None, :, None]
        outs.append(wsum / cnt)
    pooled = jnp.concatenate(outs, axis=-1).astype(h.dtype) - h
    y = jnp.einsum('btgc,gcd->btgd', pooled.reshape(bsz, t, B_GROUPS, cg), w_grp)
    y = y.reshape(bsz, t, D_MODEL) * scale
    return y, ext[:, -POOL_BUF:]


def rotary(x, pos):
    half = ROT_DIM // 2
    inv = ROPE_THETA ** (-(jnp.arange(half, dtype=jnp.float32) * 2.0) / ROT_DIM)
    ang = pos.astype(jnp.float32)[:, None] * inv[None, :]
    cos = jnp.cos(ang)[None, :, None, None, :]
    sin = jnp.sin(ang)[None, :, None, None, :]
    xr = x[..., :ROT_DIM].astype(jnp.float32)
    x1, x2 = xr[..., :half], xr[..., half:]
    rot = jnp.concatenate([x1 * cos - x2 * sin, x2 * cos + x1 * sin], axis=-1).astype(x.dtype)
    return jnp.concatenate([rot, x[..., ROT_DIM:]], axis=-1)


def diff_qkv(h, pos, w_qkv):
    bsz, t, _ = h.shape
    q, k, v = jnp.split(h @ w_qkv, 3, axis=-1)
    q = rotary(q.reshape(bsz, t, N_HEADS, 2, HEAD_DIM), pos)
    k = rotary(k.reshape(bsz, t, N_HEADS, 2, HEAD_DIM), pos)
    v = v.reshape(bsz, t, N_HEADS, 2 * HEAD_DIM)
    return q, k, v


def diff_core(q, k, v, q_pos, k_pos, lam):
    s = jnp.einsum('bqhcd,bshcd->bhcqs', q, k, preferred_element_type=jnp.float32) * (HEAD_DIM ** -0.5)
    mask = k_pos[None, :] <= q_pos[:, None]
    s = jnp.where(mask, s, -jnp.inf)
    p = jax.nn.softmax(s, axis=-1)
    a = p[:, :, 0] - lam * p[:, :, 1]
    return jnp.einsum('bhqs,bshe->bqhe', a, v.astype(jnp.float32))


def blocked_diff_attention(q, k, v, pos, lam):
    bsz, t = q.shape[0], q.shape[1]
    nqb = t // Q_BLOCK
    qb = jnp.swapaxes(q.reshape(bsz, nqb, Q_BLOCK, N_HEADS, 2, HEAD_DIM), 0, 1)
    pb = pos.reshape(nqb, Q_BLOCK)
    o = lax.map(lambda qp: diff_core(qp[0], k, v, qp[1], pos, lam), (qb, pb))
    return jnp.swapaxes(o, 0, 1).reshape(bsz, t, N_HEADS, 2 * HEAD_DIM)


def diff_out(o, subln_g, lambda_init, w_o, dtype):
    bsz, t = o.shape[0], o.shape[1]
    o = rmsnorm(o, subln_g) * (1.0 - lambda_init)
    return o.reshape(bsz, t, N_HEADS * 2 * HEAD_DIM).astype(dtype) @ w_o


def gather_pages(cache, page_table, j):
    g = cache[page_table, j]
    return g.reshape((g.shape[0], g.shape[1] * g.shape[2]) + g.shape[3:])


def moe(h, router_w, router_bias, w_gate, w_up, w_down):
    shp = h.shape
    hf = h.reshape(-1, D_MODEL)
    scores = jax.nn.sigmoid(jnp.einsum('nd,de->ne', hf, router_w, preferred_element_type=jnp.float32))
    sel = scores + router_bias.astype(jnp.float32)
    grp = sel.reshape(-1, N_EXPERT_GROUPS, N_EXPERTS // N_EXPERT_GROUPS)
    grp_score = lax.top_k(grp, TOP_K)[0].sum(-1)
    best = jnp.argmax(grp_score, axis=-1)
    in_grp = (best[:, None] == jnp.arange(N_EXPERT_GROUPS))[:, :, None]
    masked = jnp.where(in_grp, grp, -jnp.inf).reshape(-1, N_EXPERTS)
    _, idx = lax.top_k(masked, TOP_K)
    gates = jnp.take_along_axis(scores, idx, axis=-1)
    gates = gates / jnp.sum(gates, axis=-1, keepdims=True)
    combine = jnp.sum(jax.nn.one_hot(idx, N_EXPERTS, dtype=jnp.float32) * gates[..., None], axis=1)
    g = jnp.einsum('nd,edf->nef', hf, w_gate)
    u = jnp.einsum('nd,edf->nef', hf, w_up)
    a = jax.nn.silu(g) * u * combine.astype(h.dtype)[..., None]
    y = jnp.einsum('nef,efd->nd', a, w_down)
    return y.reshape(shp)


def setup_inputs(seed: int = 0) -> dict:
    key = jax.random.key(seed)
    keys = jax.random.split(key, 40)
    cnt = [0]

    def nrm(shape, s):
        k = keys[cnt[0]]
        cnt[0] += 1
        return jax.random.normal(k, shape, jnp.float32) * s

    n_pages = PAST_LEN // PAGE_SIZE
    n_used = DEC_BATCH * n_pages
    n_phys = n_used + n_used // 4
    perm = jax.random.permutation(keys[39], n_phys)
    page_table = perm[:n_used].reshape(DEC_BATCH, n_pages).astype(jnp.int32)
    cg = D_MODEL // B_GROUPS
    d_att = N_HEADS * 2 * HEAD_DIM
    return {
        'x_prompt': nrm((BATCH, SEQ, D_MODEL), 1.0),
        'x_sample': nrm((DEC_BATCH, DEC_SEQ, D_MODEL), 1.0),
        'cache_k': nrm((n_phys, N_C_LAYERS, PAGE_SIZE, N_HEADS, 2, HEAD_DIM), 1.0),
        'cache_v': nrm((n_phys, N_C_LAYERS, PAGE_SIZE, N_HEADS, 2 * HEAD_DIM), 1.0),
        'state_pool': nrm((DEC_BATCH, N_B_LAYERS, POOL_BUF, D_MODEL), 1.0),
        'page_table': page_table,
        'c_prompt': nrm((BATCH, D_MODEL), 1.0),
        'c_sample': nrm((DEC_BATCH, D_MODEL), 1.0),
        'norm1_g': 1.0 + nrm((DEPTH, D_MODEL), 0.02),
        'norm2_g': 1.0 + nrm((DEPTH, D_MODEL), 0.02),
        'ada_w': nrm((DEPTH, D_MODEL, 6 * D_MODEL), 0.5 * D_MODEL ** -0.5),
        'ada_b': nrm((DEPTH, 6 * D_MODEL), 0.02),
        'a_w_in': nrm((N_A_LAYERS, D_MODEL, 2 * D_GATE), D_MODEL ** -0.5),
        'a_ln_g': 1.0 + nrm((N_A_LAYERS, D_GATE), 0.02),
        'a_ln_b': nrm((N_A_LAYERS, D_GATE), 0.02),
        'a_w_s': nrm((N_A_LAYERS, A_GROUPS, CHUNK, CHUNK), CHUNK ** -0.5),
        'a_b_s': 1.0 + nrm((N_A_LAYERS, A_GROUPS, CHUNK), 0.02),
        'a_w_out': nrm((N_A_LAYERS, D_GATE, D_MODEL), D_GATE ** -0.5),
        'b_w_grp': nrm((N_B_LAYERS, B_GROUPS, cg, cg), cg ** -0.5),
        'b_scale': 1.0 + nrm((N_B_LAYERS, D_MODEL), 0.02),
        'c_w_qkv': nrm((N_C_LAYERS, D_MODEL, 3 * d_att), D_MODEL ** -0.5),
        'c_lq1': nrm((N_C_LAYERS, HEAD_DIM), 0.1),
        'c_lk1': nrm((N_C_LAYERS, HEAD_DIM), 0.1),
        'c_lq2': nrm((N_C_LAYERS, HEAD_DIM), 0.1),
        'c_lk2': nrm((N_C_LAYERS, HEAD_DIM), 0.1),
        'c_subln_g': 1.0 + nrm((N_C_LAYERS, 2 * HEAD_DIM), 0.02),
        'c_w_o': nrm((N_C_LAYERS, d_att, D_MODEL), d_att ** -0.5),
        'router_w': nrm((D_MODEL, N_EXPERTS), D_MODEL ** -0.5),
        'router_bias': nrm((N_EXPERTS,), 0.01),
        'e_w_gate': nrm((DEPTH, N_EXPERTS, D_MODEL, D_EXPERT), D_MODEL ** -0.5),
        'e_w_up': nrm((DEPTH, N_EXPERTS, D_MODEL, D_EXPERT), D_MODEL ** -0.5),
        'e_w_down': nrm((DEPTH, N_EXPERTS, D_EXPERT, D_MODEL), D_EXPERT ** -0.5),
        'final_g': 1.0 + nrm((D_MODEL,), 0.02),
    }


def reference(x_prompt, x_sample, cache_k, cache_v, state_pool, page_table, c_prompt, c_sample,
              norm1_g, norm2_g, ada_w, ada_b, a_w_in, a_ln_g, a_ln_b, a_w_s, a_b_s, a_w_out,
              b_w_grp, b_scale, c_w_qkv, c_lq1, c_lk1, c_lq2, c_lk2, c_subln_g, c_w_o,
              router_w, router_bias, e_w_gate, e_w_up, e_w_down, final_g):
    xp, xs = x_prompt, x_sample
    bsz_p, t_p = xp.shape[0], xp.shape[1]
    t_s = xs.shape[1]
    pos_p = jnp.arange(t_p)
    pos_s = PAST_LEN + jnp.arange(t_s)
    k_pos_s = jnp.arange(PAST_LEN + t_s)
    av_p, av_s, pr_p, pr_s, kp_l, vp_l, ks_l, vs_l = [], [], [], [], [], [], [], []
    for i in range(DEPTH):
        kind, j = i % N_MIXERS, i // N_MIXERS
        sh1p, sc1p, g1p, sh2p, sc2p, g2p = ada_mod(c_prompt, ada_w[i], ada_b[i])
        sh1s, sc1s, g1s, sh2s, sc2s, g2s = ada_mod(c_sample, ada_w[i], ada_b[i])
        hp = modulate(xp, norm1_g[i], sh1p, sc1p)
        hs = modulate(xs, norm1_g[i], sh1s, sc1s)
        if kind == 0:
            yp, vrow_p = chunk_gmlp(hp, a_w_in[j], a_ln_g[j], a_ln_b[j], a_w_s[j], a_b_s[j], a_w_out[j])
            ys, vrow_s = chunk_gmlp(hs, a_w_in[j], a_ln_g[j], a_ln_b[j], a_w_s[j], a_b_s[j], a_w_out[j])
            av_p.append(vrow_p)
            av_s.append(vrow_s)
        elif kind == 1:
            zeros_prev = jnp.zeros((bsz_p, POOL_BUF, D_MODEL), hp.dtype)
            yp, rows_p = pool_mixer(hp, zeros_prev, 0, b_w_grp[j], b_scale[j])
            ys, rows_s = pool_mixer(hs, state_pool[:, j], PAST_LEN, b_w_grp[j], b_scale[j])
            pr_p.append(rows_p)
            pr_s.append(rows_s)
        else:
            lambda_init = 0.8 - 0.6 * math.exp(-0.3 * i)
            f32 = jnp.float32
            lam = (jnp.exp(jnp.sum(c_lq1[j].astype(f32) * c_lk1[j].astype(f32)))
                   - jnp.exp(jnp.sum(c_lq2[j].astype(f32) * c_lk2[j].astype(f32))) + lambda_init)
            qp, kp, vp = diff_qkv(hp, pos_p, c_w_qkv[j])
            op = blocked_diff_attention(qp, kp, vp, pos_p, lam)
            yp = diff_out(op, c_subln_g[j], lambda_init, c_w_o[j], hp.dtype)
            qs, ks, vs = diff_qkv(hs, pos_s, c_w_qkv[j])
            k_all = jnp.concatenate([gather_pages(cache_k, page_table, j), ks], axis=1)
            v_all = jnp.concatenate([gather_pages(cache_v, page_table, j), vs], axis=1)
            os_ = diff_core(qs, k_all, v_all, pos_s, k_pos_s, lam)
            ys = diff_out(os_, c_subln_g[j], lambda_init, c_w_o[j], hs.dtype)
            kp_l.append(kp)
            vp_l.append(vp)
            ks_l.append(ks)
            vs_l.append(vs)
        xp = xp + g1p[:, None, :] * yp
        xs = xs + g1s[:, None, :] * ys
        h2p = modulate(xp, norm2_g[i], sh2p, sc2p)
        h2s = modulate(xs, norm2_g[i], sh2s, sc2s)
        xp = xp + g2p[:, None, :] * moe(h2p, router_w, router_bias, e_w_gate[i], e_w_up[i], e_w_down[i])
        xs = xs + g2s[:, None, :] * moe(h2s, router_w, router_bias, e_w_gate[i], e_w_up[i], e_w_down[i])
    y_prompt = rmsnorm(xp, final_g)
    y_sample = rmsnorm(xs, final_g)
    return (y_prompt, y_sample,
            jnp.stack(av_p, axis=1), jnp.stack(av_s, axis=1),
            jnp.stack(pr_p, axis=1), jnp.stack(pr_s, axis=1),
            jnp.stack(kp_l, axis=1), jnp.stack(vp_l, axis=1),
            jnp.stack(ks_l, axis=1), jnp.stack(vs_l, axis=1))
```

```python
import functools
import math

import jax
import jax.numpy as jnp
from jax import lax
from jax.experimental import pallas as pl
from jax.experimental.pallas import tpu as pltpu

F32 = jnp.float32
BF16 = jnp.bfloat16

D_MODEL = 1024
BATCH = 8
SEQ = 2048
DEPTH = 4
DEC_BATCH = 32
DEC_SEQ = 4
PAST_LEN = 8192
PAGE_SIZE = 128
N_PAGES = PAST_LEN // PAGE_SIZE
N_MIXERS = 3
CHUNK = 128
D_GATE = D_MODEL
A_GROUPS = 4
POOL_WINDOWS = (2, 4, 8, 16)
B_GROUPS = len(POOL_WINDOWS)
POOL_BUF = max(POOL_WINDOWS) - 1
POOL_HALO = POOL_BUF + 1
N_HEADS = 8
HEAD_DIM = D_MODEL // N_HEADS // 2
ROT_DIM = HEAD_DIM // 4
ROPE_THETA = 500000.0
N_EXPERTS = 16
N_EXPERT_GROUPS = 4
EXPERTS_PER_GROUP = N_EXPERTS // N_EXPERT_GROUPS
D_EXPERT = 512
EPS = 1e-6
N_MODS = 6

N_PROMPT = BATCH * SEQ
N_SAMPLE = DEC_BATCH * DEC_SEQ
LANES = 128
VMEM_LIMIT = 56 * 1024 * 1024

PROMPT_TILE = 256
MOE_TILE = 1024
ATTN_Q_TILE = 256
PAGES_PER_STEP = 8


def _params(*sem):
    return pltpu.CompilerParams(dimension_semantics=sem, vmem_limit_bytes=VMEM_LIMIT)


class _Stream:
    def __init__(self, rows, tile, tiles_per_batch, per_row_mods):
        self.rows = rows
        self.tile = tile
        self.tiles_per_batch = tiles_per_batch
        self.per_row_mods = per_row_mods
        self.grid = rows // tile

    def rows_spec(self, width=D_MODEL):
        return pl.BlockSpec((self.tile, width), lambda i: (i, 0))

    def mod_spec(self, k):
        if self.per_row_mods:
            return pl.BlockSpec((self.tile, D_MODEL), lambda i: (i, k))
        tpb = self.tiles_per_batch
        return pl.BlockSpec((None, None, 1, D_MODEL), lambda i: (i // tpb, k, 0, 0))


def _full_spec(shape):
    nd = len(shape)
    return pl.BlockSpec(shape, lambda i: (0,) * nd)


def _rms(x):
    return x * lax.rsqrt(jnp.mean(x * x, axis=-1, keepdims=True) + EPS)


def _modulate(x, g, sh, sc):
    return _rms(x) * g * (1.0 + sc) + sh


def _bdot(a, b):
    return jnp.dot(a, b, preferred_element_type=F32)


def _div_pow2(x, n):
    assert n & (n - 1) == 0
    return lax.shift_right_logical(x, n.bit_length() - 1)


def _mod_pow2(x, n):
    assert n & (n - 1) == 0
    return x & (n - 1)


def _ada_kernel(c_ref, w_ref, b_ref, o_ref):
    a = jax.nn.silu(c_ref[...]).astype(BF16)
    o_ref[...] = _bdot(a, w_ref[...].astype(BF16)) + b_ref[...]


def _ada(c_all, ada_w, ada_b):
    nb = c_all.shape[0]
    tn = 1536
    return pl.pallas_call(
        _ada_kernel,
        grid=(DEPTH, N_MODS * D_MODEL // tn),
        in_specs=[pl.BlockSpec((nb, D_MODEL), lambda l, n: (0, 0)),
                  pl.BlockSpec((None, D_MODEL, tn), lambda l, n: (l, 0, n)),
                  pl.BlockSpec((None, 1, tn), lambda l, n: (l, 0, n))],
        out_specs=pl.BlockSpec((None, nb, tn), lambda l, n: (l, 0, n)),
        out_shape=jax.ShapeDtypeStruct((DEPTH, nb, N_MODS * D_MODEL), F32),
        compiler_params=_params("arbitrary", "arbitrary"),
        name="ada_mod",
    )(c_all, ada_w, ada_b.reshape(DEPTH, 1, N_MODS * D_MODEL))


def _gmlp_kernel(x_ref, sh_ref, sc_ref, gt_ref, ng_ref, win_ref, lng_ref, lnb_ref, wmix_ref,
                 bmix_ref, wout_ref, xo_ref, v_ref, mix_ref, *, period):
    tm = x_ref.shape[0]
    x = x_ref[...]
    h = _modulate(x, ng_ref[...], sh_ref[...], sc_ref[...]).astype(BF16)
    z = _bdot(h, win_ref[...])
    z = 0.5 * z * (1.0 + lax.erf(z * (2.0 ** -0.5)))
    u = z[:, :D_GATE]
    v = z[:, D_GATE:]
    mu = jnp.mean(v, axis=-1, keepdims=True)
    vc = v - mu
    var = jnp.mean(vc * vc, axis=-1, keepdims=True)
    v = vc * lax.rsqrt(var + EPS) * lng_ref[...] + lnb_ref[...]
    v_ref[...] = v[tm - CHUNK:, :]
    vb = v.astype(BF16)
    t = lax.broadcasted_iota(jnp.int32, (CHUNK, CHUNK), 0)
    s = lax.broadcasted_iota(jnp.int32, (CHUNK, CHUNK), 1)
    keep = (s <= t) & (_div_pow2(t, period) == _div_pow2(s, period))
    cg = D_GATE // A_GROUPS
    for g in range(A_GROUPS):
        wm = jnp.where(keep, wmix_ref[g], 0.0).astype(BF16)
        for c in range(tm // CHUNK):
            rows = slice(c * CHUNK, (c + 1) * CHUNK)
            cols = slice(g * cg, (g + 1) * cg)
            mix_ref[rows, cols] = _bdot(wm, vb[rows, cols]) + bmix_ref[:, cols]
    y = _bdot((u * mix_ref[...]).astype(BF16), wout_ref[...])
    xo_ref[...] = x + gt_ref[...] * y


def _gmlp_layer(st, x, mods, ng, w_in, ln_g, ln_b, wmix, bmix, w_out, period, n_vblocks):
    kern = functools.partial(_gmlp_kernel, period=period)
    vb_per = st.grid // n_vblocks
    return pl.pallas_call(
        kern,
        grid=(st.grid,),
        in_specs=[st.rows_spec(), st.mod_spec(0), st.mod_spec(1), st.mod_spec(2),
                  _full_spec((1, D_MODEL)), _full_spec((D_MODEL, 2 * D_GATE)),
                  _full_spec((1, D_GATE)), _full_spec((1, D_GATE)),
                  _full_spec((A_GROUPS, CHUNK, CHUNK)), _full_spec((CHUNK, D_GATE)),
                  _full_spec((D_GATE, D_MODEL))],
        out_specs=[st.rows_spec(),
                   pl.BlockSpec((CHUNK, D_GATE), lambda i: (i // vb_per, 0))],
        out_shape=[jax.ShapeDtypeStruct((st.rows, D_MODEL), F32),
                   jax.ShapeDtypeStruct((n_vblocks * CHUNK, D_GATE), F32)],
        scratch_shapes=[pltpu.VMEM((st.tile, D_GATE), F32)],
        compiler_params=_params("arbitrary"),
        name="gmlp_mixer",
    )(x, mods, mods, mods, ng, w_in, ln_g, ln_b, wmix, bmix, w_out)


def _pool_kernel(x_ref, halo_ref, sh_ref, sc_ref, gt_ref, ng_ref, wgrp_ref, scale_ref,
                 xo_ref, rows_ref, *, halo_is_h, tiles_per_batch, pos0, n_rows_out):
    tm = x_ref.shape[0]
    x = x_ref[...]
    ng, sh, sc = ng_ref[...], sh_ref[...], sc_ref[...]
    h = _modulate(x, ng, sh, sc)
    if halo_is_h:
        halo = halo_ref[...]
    else:
        sh_h = sh if sh.shape[0] == 1 else sh[:POOL_HALO]
        sc_h = sc if sc.shape[0] == 1 else sc[:POOL_HALO]
        first = (pl.program_id(0) % tiles_per_batch) == 0
        halo = jnp.where(first, 0.0, _modulate(halo_ref[...], ng, sh_h, sc_h))
    ext = jnp.concatenate([halo, h], axis=0)
    rows_ref[...] = ext[POOL_HALO + tm - n_rows_out:, :]
    t0 = (pl.program_id(0) % tiles_per_batch) * tm
    pos = pos0 + t0 + lax.broadcasted_iota(jnp.int32, (tm, 1), 0)
    cg = D_MODEL // B_GROUPS
    ys = []
    for g, w in enumerate(POOL_WINDOWS):
        cols = slice(g * cg, (g + 1) * cg)
        acc = ext[:, cols]
        span = 1
        while span < w:
            acc = acc + pltpu.roll(acc, span, 0)
            span *= 2
        cnt = jnp.minimum(pos + 1, w).astype(F32)
        pooled = acc[POOL_HALO:, :] / cnt - h[:, cols]
        ys.append(_bdot(pooled.astype(BF16), wgrp_ref[g]))
    y = jnp.concatenate(ys, axis=-1) * scale_ref[...]
    xo_ref[...] = x + gt_ref[...] * y


def _pool_layer(st, x, halo, halo_spec, mods, ng, w_grp, scale, halo_is_h, pos0, n_rows_out, n_batches):
    kern = functools.partial(_pool_kernel, halo_is_h=halo_is_h, tiles_per_batch=st.tiles_per_batch,
                             pos0=pos0, n_rows_out=n_rows_out)
    cg = D_MODEL // B_GROUPS
    tpb = st.tiles_per_batch
    return pl.pallas_call(
        kern,
        grid=(st.grid,),
        in_specs=[st.rows_spec(), halo_spec, st.mod_spec(0), st.mod_spec(1), st.mod_spec(2),
                  _full_spec((1, D_MODEL)), _full_spec((B_GROUPS, cg, cg)), _full_spec((1, D_MODEL))],
        out_specs=[st.rows_spec(),
                   pl.BlockSpec((n_rows_out, D_MODEL), lambda i: (i // tpb, 0))],
        out_shape=[jax.ShapeDtypeStruct((st.rows, D_MODEL), F32),
                   jax.ShapeDtypeStruct((n_batches * n_rows_out, D_MODEL), F32)],
        compiler_params=_params("arbitrary"),
        name="pool_mixer",
    )(x, halo, mods, mods, mods, ng, w_grp, scale)


def _rope_tables(pos):
    half = ROT_DIM // 2
    inv = ROPE_THETA ** (-(jnp.arange(half, dtype=F32) * 2.0) / ROT_DIM)
    ang = pos.astype(F32)[:, None] * inv[None, :]
    cos, sin = jnp.cos(ang), jnp.sin(ang)
    n = pos.shape[0]
    pad = jnp.zeros((n, HEAD_DIM - ROT_DIM), F32)
    c = jnp.concatenate([cos, cos, pad + 1.0], axis=1)
    sa = jnp.concatenate([-sin, jnp.zeros_like(sin), pad], axis=1)
    sb = jnp.concatenate([jnp.zeros_like(sin), sin, pad], axis=1)
    return tuple(jnp.concatenate([t, t], axis=1) for t in (c, sa, sb))


def _qkv_kernel(x_ref, sh_ref, sc_ref, ng_ref, w_ref, c_ref, sa_ref, sb_ref, q_ref, k_ref, v_ref):
    h = _modulate(x_ref[...], ng_ref[...], sh_ref[...], sc_ref[...]).astype(BF16)
    qkv = _bdot(h, w_ref[...])
    half = ROT_DIM // 2
    c, sa, sb = c_ref[...], sa_ref[...], sb_ref[...]
    for hd in range(N_HEADS):
        for which, ref in ((0, q_ref), (1, k_ref)):
            lo = which * D_MODEL + hd * LANES
            xh = qkv[:, lo:lo + LANES]
            rot = xh * c + pltpu.roll(xh, LANES - half, 1) * sa + pltpu.roll(xh, half, 1) * sb
            ref[:, hd * LANES:(hd + 1) * LANES] = rot.astype(ref.dtype)
    v_ref[...] = qkv[:, 2 * D_MODEL:]


def _qkv_layer(st, x, mods, ng, w_qkv, tables, table_spec, q_dtype):
    return pl.pallas_call(
        _qkv_kernel,
        grid=(st.grid,),
        in_specs=[st.rows_spec(), st.mod_spec(0), st.mod_spec(1), _full_spec((1, D_MODEL)),
                  _full_spec((D_MODEL, 3 * D_MODEL)), table_spec, table_spec, table_spec],
        out_specs=[st.rows_spec(), st.rows_spec(), st.rows_spec()],
        out_shape=[jax.ShapeDtypeStruct((st.rows, D_MODEL), q_dtype),
                   jax.ShapeDtypeStruct((st.rows, D_MODEL), F32),
                   jax.ShapeDtypeStruct((st.rows, D_MODEL), F32)],
        compiler_params=_params("arbitrary"),
        name="qkv_rope",
    )(x, mods, mods, ng, w_qkv, *tables)


def _lambda(lq1_ref, lk1_ref, lq2_ref, lk2_ref, lambda_init):
    a = jnp.sum(lq1_ref[...] * lk1_ref[...], axis=-1, keepdims=True)
    b = jnp.sum(lq2_ref[...] * lk2_ref[...], axis=-1, keepdims=True)
    return jnp.exp(a) - jnp.exp(b) + lambda_init


def _attn_kernel(q_ref, k_ref, v_ref, lq1_ref, lk1_ref, lq2_ref, lk2_ref, sg_ref, o_ref, *, lambda_init):
    tq = q_ref.shape[0]
    tk = k_ref.shape[0]
    lam = _lambda(lq1_ref, lk1_ref, lq2_ref, lk2_ref, lambda_init)
    q = q_ref[...]
    kb = k_ref[...].astype(BF16)
    vb = v_ref[...].astype(BF16)
    first = lax.broadcasted_iota(jnp.int32, (1, LANES), 1) < HEAD_DIM
    qpos = pl.program_id(2) * tq + lax.broadcasted_iota(jnp.int32, (tq, 1), 0)
    kpos = lax.broadcasted_iota(jnp.int32, (1, tk), 1)
    visible = kpos <= qpos
    nt = (((1,), (1,)), ((), ()))
    probs = []
    for comp in range(2):
        qc = jnp.where(first if comp == 0 else jnp.logical_not(first), q, jnp.zeros_like(q))
        s = lax.dot_general(qc, kb, nt, preferred_element_type=F32) * (HEAD_DIM ** -0.5)
        s = jnp.where(visible, s, -jnp.inf)
        e = jnp.exp(s - jnp.max(s, axis=-1, keepdims=True))
        probs.append(e * (1.0 / jnp.sum(e, axis=-1, keepdims=True)))
    a = (probs[0] - lam * probs[1]).astype(BF16)
    o = _bdot(a, vb)
    o_ref[...] = (_rms(o) * sg_ref[...] * (1.0 - lambda_init)).astype(o_ref.dtype)


def _attn_prompt(q, k, v, lq1, lk1, lq2, lk2, subln_g, lambda_init):
    nq = SEQ // ATTN_Q_TILE
    kern = functools.partial(_attn_kernel, lambda_init=lambda_init)
    vec = pl.BlockSpec((1, HEAD_DIM), lambda b, h, i: (0, 0))
    return pl.pallas_call(
        kern,
        grid=(BATCH, N_HEADS, nq),
        in_specs=[pl.BlockSpec((ATTN_Q_TILE, LANES), lambda b, h, i: (b * nq + i, h)),
                  pl.BlockSpec((SEQ, LANES), lambda b, h, i: (b, h)),
                  pl.BlockSpec((SEQ, LANES), lambda b, h, i: (b, h)),
                  vec, vec, vec, vec,
                  pl.BlockSpec((1, 2 * HEAD_DIM), lambda b, h, i: (0, 0))],
        out_specs=pl.BlockSpec((ATTN_Q_TILE, LANES), lambda b, h, i: (b * nq + i, h)),
        out_shape=jax.ShapeDtypeStruct((N_PROMPT, D_MODEL), BF16),
        compiler_params=_params("arbitrary", "arbitrary", "arbitrary"),
        name="diff_attn_prompt",
    )(q, k, v, lq1, lk1, lq2, lk2, subln_g)


def _decode_kernel(pt_ref, q_ref, kn_ref, vn_ref, lq1_ref, lk1_ref, lq2_ref, lk2_ref, sg_ref, *rest,
                   lambda_init):
    k_refs = rest[:PAGES_PER_STEP]
    v_refs = rest[PAGES_PER_STEP:2 * PAGES_PER_STEP]
    o_ref, m_ref, l_ref, acc_ref = rest[2 * PAGES_PER_STEP:]
    step = pl.program_id(1)
    n_rows = 2 * DEC_SEQ * N_HEADS
    lane = lax.broadcasted_iota(jnp.int32, (N_HEADS, D_MODEL), 1)
    head = lax.broadcasted_iota(jnp.int32, (N_HEADS, D_MODEL), 0)
    own_head = _div_pow2(lane, LANES) == head
    comp0 = _mod_pow2(lane, LANES) < HEAD_DIM
    q = q_ref[...].astype(F32)
    pieces = []
    for comp in range(2):
        sel = own_head & (comp0 if comp == 0 else jnp.logical_not(comp0))
        for qi in range(DEC_SEQ):
            pieces.append(jnp.where(sel, jnp.broadcast_to(q[qi:qi + 1, :], (N_HEADS, D_MODEL)), 0.0))
    qrows = jnp.concatenate(pieces, axis=0).astype(BF16)
    scale = HEAD_DIM ** -0.5
    nt = (((1,), (1,)), ((), ()))

    @pl.when(step == 0)
    def _():
        m_ref[...] = jnp.full(m_ref.shape, -jnp.inf, F32)
        l_ref[...] = jnp.zeros(l_ref.shape, F32)
        acc_ref[...] = jnp.zeros(acc_ref.shape, F32)

    m, l, acc = m_ref[...], l_ref[...], acc_ref[...]
    for kr, vr in zip(k_refs, v_refs):
        s = lax.dot_general(qrows, kr[...].astype(BF16), nt, preferred_element_type=F32) * scale
        m_new = jnp.maximum(m, jnp.max(s, axis=-1, keepdims=True))
        alpha = jnp.exp(m - m_new)
        p = jnp.exp(s - m_new)
        l = l * alpha + jnp.sum(p, axis=-1, keepdims=True)
        acc = acc * alpha + _bdot(p.astype(BF16), vr[...].astype(BF16))
        m = m_new
    m_ref[...], l_ref[...], acc_ref[...] = m, l, acc

    @pl.when(step == pl.num_programs(1) - 1)
    def _():
        kn = kn_ref[...].astype(BF16).astype(F32)
        vn = vn_ref[...].astype(BF16).astype(F32)
        qf = qrows.astype(F32)
        row_q = _div_pow2(_mod_pow2(lax.broadcasted_iota(jnp.int32, (n_rows, 1), 0), DEC_SEQ * N_HEADS), N_HEADS)
        s_new = []
        for j in range(DEC_SEQ):
            sj = jnp.sum(qf * kn[j:j + 1, :], axis=-1, keepdims=True) * scale
            s_new.append(jnp.where(row_q >= j, sj, -jnp.inf))
        m2 = m
        for sj in s_new:
            m2 = jnp.maximum(m2, sj)
        alpha = jnp.exp(m - m2)
        l2 = l * alpha
        acc2 = acc * alpha
        for j, sj in enumerate(s_new):
            pj = jnp.exp(sj - m2)
            l2 = l2 + pj
            acc2 = acc2 + pj.astype(BF16).astype(F32) * vn[j:j + 1, :]
        outn = acc2 / l2
        lam = _lambda(lq1_ref, lk1_ref, lq2_ref, lk2_ref, lambda_init)
        for qi in range(DEC_SEQ):
            per_comp = []
            for comp in range(2):
                r0 = comp * DEC_SEQ * N_HEADS + qi * N_HEADS
                blk = jnp.where(own_head, outn[r0:r0 + N_HEADS, :], 0.0)
                per_comp.append(jnp.sum(blk, axis=0, keepdims=True))
            o = per_comp[0] - lam * per_comp[1]
            for hd in range(N_HEADS):
                cols = slice(hd * LANES, (hd + 1) * LANES)
                o_ref[qi:qi + 1, cols] = _rms(o[:, cols]) * sg_ref[...] * (1.0 - lambda_init)


def _attn_decode(page_table, q, k_new, v_new, cache_k, cache_v, layer, lq1, lk1, lq2, lk2, subln_g, lambda_init):
    n_steps = N_PAGES // PAGES_PER_STEP
    kern = functools.partial(_decode_kernel, lambda_init=lambda_init)
    tok = pl.BlockSpec((None, DEC_SEQ, D_MODEL), lambda b, p, pt: (b, 0, 0))
    vec = pl.BlockSpec((1, HEAD_DIM), lambda b, p, pt: (0, 0))

    def page_spec(i):
        return pl.BlockSpec((None, None, PAGE_SIZE, D_MODEL),
                            lambda b, p, pt: (pt[b * N_PAGES + p * PAGES_PER_STEP + i], layer, 0, 0))

    n_rows = 2 * DEC_SEQ * N_HEADS
    grid_spec = pltpu.PrefetchScalarGridSpec(
        num_scalar_prefetch=1,
        grid=(DEC_BATCH, n_steps),
        in_specs=[tok, tok, tok, vec, vec, vec, vec,
                  pl.BlockSpec((1, 2 * HEAD_DIM), lambda b, p, pt: (0, 0))]
                 + [page_spec(i) for i in range(PAGES_PER_STEP)] * 2,
        out_specs=tok,
        scratch_shapes=[pltpu.VMEM((n_rows, 1), F32), pltpu.VMEM((n_rows, 1), F32),
                        pltpu.VMEM((n_rows, D_MODEL), F32)],
    )
    return pl.pallas_call(
        kern,
        grid_spec=grid_spec,
        out_shape=jax.ShapeDtypeStruct((DEC_BATCH, DEC_SEQ, D_MODEL), F32),
        compiler_params=_params("arbitrary", "arbitrary"),
        name="diff_attn_decode",
    )(page_table.reshape(-1), q, k_new, v_new, lq1, lk1, lq2, lk2, subln_g,
      *([cache_k] * PAGES_PER_STEP), *([cache_v] * PAGES_PER_STEP))


def _outproj_kernel(o_ref, x_ref, gt_ref, w_ref, xo_ref):
    xo_ref[...] = x_ref[...] + gt_ref[...] * _bdot(o_ref[...].astype(BF16), w_ref[...])


def _outproj_layer(st, o, x, mods, w_o):
    return pl.pallas_call(
        _outproj_kernel,
        grid=(st.grid,),
        in_specs=[st.rows_spec(), st.rows_spec(), st.mod_spec(2), _full_spec((D_MODEL, D_MODEL))],
        out_specs=st.rows_spec(),
        out_shape=jax.ShapeDtypeStruct((st.rows, D_MODEL), F32),
        compiler_params=_params("arbitrary"),
        name="attn_out_proj",
    )(o, x, mods, w_o)


def _router_kernel(x_ref, sh_ref, sc_ref, ng_ref, rw_ref, rb_ref, h_ref, comb_ref):
    h = _modulate(x_ref[...], ng_ref[...], sh_ref[...], sc_ref[...])
    h_ref[...] = h.astype(h_ref.dtype)
    logits = jnp.dot(h, rw_ref[...], preferred_element_type=F32, precision=lax.Precision.HIGHEST)
    scores = jax.nn.sigmoid(logits)
    sel = scores + rb_ref[...]
    eid = lax.broadcasted_iota(jnp.int32, sel.shape, 1)
    grp = _div_pow2(eid, EXPERTS_PER_GROUP)
    neg = -jnp.inf

    def first_max(vals):
        m = jnp.max(vals, axis=-1, keepdims=True)
        idx = jnp.min(jnp.where(vals == m, eid, N_EXPERTS), axis=-1, keepdims=True)
        return m, idx

    best = None
    for g in range(N_EXPERT_GROUPS):
        vg = jnp.where(grp == g, sel, neg)
        m1, i1 = first_max(vg)
        m2, _ = first_max(jnp.where(eid == i1, neg, vg))
        gs = m1 + m2
        if best is None:
            best, best_score = jnp.zeros_like(i1), gs
        else:
            better = gs > best_score
            best = jnp.where(better, g, best)
            best_score = jnp.where(better, gs, best_score)
    masked = jnp.where(grp == best, sel, neg)
    _, ia = first_max(masked)
    _, ib = first_max(jnp.where(eid == ia, neg, masked))
    ga = jnp.sum(jnp.where(eid == ia, scores, 0.0), axis=-1, keepdims=True)
    gb = jnp.sum(jnp.where(eid == ib, scores, 0.0), axis=-1, keepdims=True)
    den = ga + gb
    comb_ref[...] = jnp.where(eid == ia, ga / den, 0.0) + jnp.where(eid == ib, gb / den, 0.0)


def _router_layer(st, x, mods, ng, router_w, router_b):
    return pl.pallas_call(
        _router_kernel,
        grid=(st.grid,),
        in_specs=[st.rows_spec(), st.mod_spec(3), st.mod_spec(4), _full_spec((1, D_MODEL)),
                  _full_spec((D_MODEL, N_EXPERTS)), _full_spec((1, N_EXPERTS))],
        out_specs=[st.rows_spec(), st.rows_spec(N_EXPERTS)],
        out_shape=[jax.ShapeDtypeStruct((st.rows, D_MODEL), BF16),
                   jax.ShapeDtypeStruct((st.rows, N_EXPERTS), F32)],
        compiler_params=_params("arbitrary"),
        name="moe_router",
    )(x, mods, mods, ng, router_w, router_b)


def _moe_kernel(h_ref, gate_ref, wg_ref, wu_ref, wd_ref, x_ref, g2_ref, o_ref):
    e = pl.program_id(1)
    h = h_ref[...]
    g = _bdot(h, wg_ref[...])
    u = _bdot(h, wu_ref[...])
    a = (jax.nn.silu(g) * u * gate_ref[...]).astype(BF16)
    y = _bdot(a, wd_ref[...])

    @pl.when(e == 0)
    def _():
        o_ref[...] = y

    @pl.when(e > 0)
    def _():
        o_ref[...] += y

    @pl.when(e == N_EXPERTS - 1)
    def _():
        o_ref[...] = x_ref[...] + g2_ref[...] * o_ref[...]


def _moe_layer(rows, tile, tiles_per_batch, per_row_mods, h, gate, wg, wu, wd, layer, x, mods):
    if per_row_mods:
        g2_spec = pl.BlockSpec((tile, D_MODEL), lambda i, e: (i, 5))
    else:
        g2_spec = pl.BlockSpec((None, None, 1, D_MODEL), lambda i, e: (i // tiles_per_batch, 5, 0, 0))
    row_spec = pl.BlockSpec((tile, D_MODEL), lambda i, e: (i, 0))
    return pl.pallas_call(
        _moe_kernel,
        grid=(rows // tile, N_EXPERTS),
        in_specs=[row_spec,
                  pl.BlockSpec((None, tile, 1), lambda i, e: (e, i, 0)),
                  pl.BlockSpec((None, None, D_MODEL, D_EXPERT), lambda i, e: (layer, e, 0, 0)),
                  pl.BlockSpec((None, None, D_MODEL, D_EXPERT), lambda i, e: (layer, e, 0, 0)),
                  pl.BlockSpec((None, None, D_EXPERT, D_MODEL), lambda i, e: (layer, e, 0, 0)),
                  row_spec, g2_spec],
        out_specs=row_spec,
        out_shape=jax.ShapeDtypeStruct((rows, D_MODEL), F32),
        compiler_params=_params("arbitrary", "arbitrary"),
        name="moe_experts",
    )(h, gate, wg, wu, wd, x, mods)


def _final_kernel(x_ref, g_ref, o_ref):
    o_ref[...] = _rms(x_ref[...]) * g_ref[...]


def _final_norm(st, x, g):
    return pl.pallas_call(
        _final_kernel,
        grid=(st.grid,),
        in_specs=[st.rows_spec(), _full_spec((1, D_MODEL))],
        out_specs=st.rows_spec(),
        out_shape=jax.ShapeDtypeStruct((st.rows, D_MODEL), F32),
        compiler_params=_params("arbitrary"),
        name="final_norm",
    )(x, g)


def kernel(x_prompt, x_sample, cache_k, cache_v, state_pool, page_table, c_prompt, c_sample,
           norm1_g, norm2_g, ada_w, ada_b, a_w_in, a_ln_g, a_ln_b, a_w_s, a_b_s, a_w_out,
           b_w_grp, b_scale, c_w_qkv, c_lq1, c_lk1, c_lq2, c_lk2, c_subln_g, c_w_o,
           router_w, router_bias, e_w_gate, e_w_up, e_w_down, final_g):
    sp = _Stream(N_PROMPT, PROMPT_TILE, SEQ // PROMPT_TILE, per_row_mods=False)
    ss = _Stream(N_SAMPLE, N_SAMPLE, 1, per_row_mods=True)

    xp = x_prompt.reshape(N_PROMPT, D_MODEL)
    xs = x_sample.reshape(N_SAMPLE, D_MODEL)
    mods_all = _ada(jnp.concatenate([c_prompt, c_sample], axis=0), ada_w, ada_b)

    wg_b, wu_b, wd_b = e_w_gate.astype(BF16), e_w_up.astype(BF16), e_w_down.astype(BF16)
    rb = router_bias.reshape(1, N_EXPERTS)
    n_phys = cache_k.shape[0]
    n_c_layers = cache_k.shape[1]
    ck = cache_k.reshape(n_phys, n_c_layers, PAGE_SIZE, D_MODEL)
    cv = cache_v.reshape(n_phys, n_c_layers, PAGE_SIZE, D_MODEL)

    av_p, av_s, pr_p, pr_s, kp_l, vp_l, ks_l, vs_l = [], [], [], [], [], [], [], []
    for i in range(DEPTH):
        kind, j = i % N_MIXERS, i // N_MIXERS
        mp = mods_all[i, :BATCH].reshape(BATCH, N_MODS, 1, D_MODEL)
        ms = jnp.repeat(mods_all[i, BATCH:], DEC_SEQ, axis=0)
        ng1 = norm1_g[i].reshape(1, D_MODEL)
        ng2 = norm2_g[i].reshape(1, D_MODEL)
        if kind == 0:
            w_in = a_w_in[j].astype(BF16)
            w_out = a_w_out[j].astype(BF16)
            ln_g, ln_b = a_ln_g[j].reshape(1, D_GATE), a_ln_b[j].reshape(1, D_GATE)
            bias = jnp.repeat(a_b_s[j].T, D_GATE // A_GROUPS, axis=1)
            xp, vrow_p = _gmlp_layer(sp, xp, mp, ng1, w_in, ln_g, ln_b, a_w_s[j], bias, w_out, CHUNK, BATCH)
            reps = CHUNK // DEC_SEQ
            wmix_s = jnp.tile(a_w_s[j][:, :DEC_SEQ, :DEC_SEQ], (1, reps, reps))
            bias_s = jnp.tile(bias[:DEC_SEQ], (reps, 1))
            xs, vrow_s = _gmlp_layer(ss, xs, ms, ng1, w_in, ln_g, ln_b, wmix_s, bias_s, w_out, DEC_SEQ, 1)
            av_p.append(vrow_p.reshape(BATCH, CHUNK, D_GATE))
            av_s.append(vrow_s.reshape(DEC_BATCH, DEC_SEQ, D_GATE))
        elif kind == 1:
            w_grp = b_w_grp[j].astype(BF16)
            scale = b_scale[j].reshape(1, D_MODEL)
            hb = PROMPT_TILE // POOL_HALO
            halo_spec = pl.BlockSpec((POOL_HALO, D_MODEL), lambda t: (jnp.maximum(t * hb - 1, 0), 0))
            xp, rows_p = _pool_layer(sp, xp, xp, halo_spec, mp, ng1, w_grp, scale, False, 0, POOL_HALO, BATCH)
            pr_p.append(rows_p.reshape(BATCH, POOL_HALO, D_MODEL)[:, 1:])
            pad_t = 2 * DEC_SEQ
            s8 = _Stream(DEC_BATCH * pad_t, pad_t, 1, per_row_mods=True)
            xs8 = jnp.pad(xs.reshape(DEC_BATCH, DEC_SEQ, D_MODEL), ((0, 0), (0, DEC_SEQ), (0, 0)))
            ms8 = jnp.repeat(mods_all[i, BATCH:], pad_t, axis=0)
            prev = jnp.pad(state_pool[:, j], ((0, 0), (1, 0), (0, 0))).reshape(DEC_BATCH * POOL_HALO, D_MODEL)
            prev_spec = pl.BlockSpec((POOL_HALO, D_MODEL), lambda t: (t, 0))
            n_out = POOL_HALO + pad_t
            xs8, rows_s = _pool_layer(s8, xs8.reshape(-1, D_MODEL), prev, prev_spec, ms8, ng1, w_grp, scale,
                                      True, PAST_LEN, n_out, DEC_BATCH)
            xs = xs8.reshape(DEC_BATCH, pad_t, D_MODEL)[:, :DEC_SEQ].reshape(N_SAMPLE, D_MODEL)
            lo = 1 + DEC_SEQ
            pr_s.append(rows_s.reshape(DEC_BATCH, n_out, D_MODEL)[:, lo:lo + POOL_BUF])
        else:
            lambda_init = 0.8 - 0.6 * math.exp(-0.3 * i)
            w_qkv = c_w_qkv[j].astype(BF16)
            w_o = c_w_o[j].astype(BF16)
            lq1, lk1 = c_lq1[j].reshape(1, HEAD_DIM), c_lk1[j].reshape(1, HEAD_DIM)
            lq2, lk2 = c_lq2[j].reshape(1, HEAD_DIM), c_lk2[j].reshape(1, HEAD_DIM)
            sg = c_subln_g[j].reshape(1, 2 * HEAD_DIM)
            tpb = sp.tiles_per_batch
            tab_p = _rope_tables(jnp.arange(SEQ))
            qp, kp, vp = _qkv_layer(sp, xp, mp, ng1, w_qkv, tab_p,
                                    pl.BlockSpec((PROMPT_TILE, LANES), lambda t: (t % tpb, 0)), BF16)
            op = _attn_prompt(qp, kp, vp, lq1, lk1, lq2, lk2, sg, lambda_init)
            xp = _outproj_layer(sp, op, xp, mp, w_o)
            tab_s = _rope_tables(jnp.tile(PAST_LEN + jnp.arange(DEC_SEQ), DEC_BATCH))
            qs, ks, vs = _qkv_layer(ss, xs, ms, ng1, w_qkv, tab_s,
                                    pl.BlockSpec((N_SAMPLE, LANES), lambda t: (0, 0)), F32)
            tok = (DEC_BATCH, DEC_SEQ, D_MODEL)
            os_ = _attn_decode(page_table, qs.reshape(tok), ks.reshape(tok), vs.reshape(tok), ck, cv, j,
                               lq1, lk1, lq2, lk2, sg, lambda_init)
            xs = _outproj_layer(ss, os_.reshape(N_SAMPLE, D_MODEL), xs, ms, w_o)
            kp_l.append(kp.reshape(BATCH, SEQ, N_HEADS, 2, HEAD_DIM))
            vp_l.append(vp.reshape(BATCH, SEQ, N_HEADS, 2 * HEAD_DIM))
            ks_l.append(ks.reshape(DEC_BATCH, DEC_SEQ, N_HEADS, 2, HEAD_DIM))
            vs_l.append(vs.reshape(DEC_BATCH, DEC_SEQ, N_HEADS, 2 * HEAD_DIM))

        h2p, comb_p = _router_layer(sp, xp, mp, ng2, router_w, rb)
        h2s, comb_s = _router_layer(ss, xs, ms, ng2, router_w, rb)
        gate_p = comb_p.T.reshape(N_EXPERTS, N_PROMPT, 1)
        gate_s = comb_s.T.reshape(N_EXPERTS, N_SAMPLE, 1)
        xp = _moe_layer(N_PROMPT, MOE_TILE, SEQ // MOE_TILE, False, h2p, gate_p, wg_b, wu_b, wd_b, i, xp, mp)
        xs = _moe_layer(N_SAMPLE, N_SAMPLE, 1, True, h2s, gate_s, wg_b, wu_b, wd_b, i, xs, ms)

    fg = final_g.reshape(1, D_MODEL)
    y_prompt = _final_norm(sp, xp, fg).reshape(BATCH, SEQ, D_MODEL)
    y_sample = _final_norm(ss, xs, fg).reshape(DEC_BATCH, DEC_SEQ, D_MODEL)
    return (y_prompt, y_sample,
            jnp.stack(av_p, axis=1), jnp.stack(av_s, axis=1),
            jnp.stack(pr_p, axis=1), jnp.stack(pr_s, axis=1),
            jnp.stack(kp_l, axis=1), jnp.stack(vp_l, axis=1),
            jnp.stack(ks_l, axis=1), jnp.stack(vs_l, axis=1))
```

```python
import functools
import math

import jax
import jax.numpy as jnp
from jax import lax
from jax.experimental import pallas as pl
from jax.experimental.pallas import tpu as pltpu

F32 = jnp.float32
BF16 = jnp.bfloat16

D_MODEL = 1024
BATCH = 8
SEQ = 2048
DEPTH = 4
DEC_BATCH = 32
DEC_SEQ = 4
PAST_LEN = 8192
PAGE_SIZE = 128
N_PAGES = PAST_LEN // PAGE_SIZE
N_MIXERS = 3
CHUNK = 128
D_GATE = D_MODEL
A_GROUPS = 4
POOL_WINDOWS = (2, 4, 8, 16)
B_GROUPS = len(POOL_WINDOWS)
POOL_BUF = max(POOL_WINDOWS) - 1
POOL_HALO = POOL_BUF + 1
N_HEADS = 8
HEAD_DIM = D_MODEL // N_HEADS // 2
ROT_DIM = HEAD_DIM // 4
ROPE_THETA = 500000.0
N_EXPERTS = 16
N_EXPERT_GROUPS = 4
EXPERTS_PER_GROUP = N_EXPERTS // N_EXPERT_GROUPS
D_EXPERT = 512
EPS = 1e-6
N_MODS = 6

N_PROMPT = BATCH * SEQ
N_SAMPLE = DEC_BATCH * DEC_SEQ
LANES = 128
VMEM_LIMIT = 56 * 1024 * 1024

PROMPT_TILE = 256
ATTN_Q_TILE = 256
PAGES_PER_STEP = 8


def _params(*sem):
    return pltpu.CompilerParams(dimension_semantics=sem, vmem_limit_bytes=VMEM_LIMIT)


class _Stream:
    def __init__(self, rows, tile, tiles_per_batch, per_row_mods):
        self.rows = rows
        self.tile = tile
        self.tiles_per_batch = tiles_per_batch
        self.per_row_mods = per_row_mods
        self.grid = rows // tile

    def rows_spec(self, width=D_MODEL):
        return pl.BlockSpec((self.tile, width), lambda i: (i, 0))

    def mod_spec(self, k):
        if self.per_row_mods:
            return pl.BlockSpec((self.tile, D_MODEL), lambda i: (i, k))
        tpb = self.tiles_per_batch
        return pl.BlockSpec((None, None, 1, D_MODEL), lambda i: (i // tpb, k, 0, 0))


def _full_spec(shape):
    nd = len(shape)
    return pl.BlockSpec(shape, lambda i: (0,) * nd)


def _rms(x):
    return x * lax.rsqrt(jnp.mean(x * x, axis=-1, keepdims=True) + EPS)


def _modulate(x, g, sh, sc):
    return _rms(x) * g * (1.0 + sc) + sh


def _bdot(a, b):
    return jnp.dot(a, b, preferred_element_type=F32)


def _div_pow2(x, n):
    assert n & (n - 1) == 0
    return lax.shift_right_logical(x, n.bit_length() - 1)


def _mod_pow2(x, n):
    assert n & (n - 1) == 0
    return x & (n - 1)


def _ada_kernel(c_ref, w_ref, b_ref, o_ref):
    a = jax.nn.silu(c_ref[...]).astype(BF16)
    o_ref[...] = _bdot(a, w_ref[...].astype(BF16)) + b_ref[...]


def _ada(c_all, ada_w, ada_b):
    nb = c_all.shape[0]
    tn = 1536
    return pl.pallas_call(
        _ada_kernel,
        grid=(DEPTH, N_MODS * D_MODEL // tn),
        in_specs=[pl.BlockSpec((nb, D_MODEL), lambda l, n: (0, 0)),
                  pl.BlockSpec((None, D_MODEL, tn), lambda l, n: (l, 0, n)),
                  pl.BlockSpec((None, 1, tn), lambda l, n: (l, 0, n))],
        out_specs=pl.BlockSpec((None, nb, tn), lambda l, n: (l, 0, n)),
        out_shape=jax.ShapeDtypeStruct((DEPTH, nb, N_MODS * D_MODEL), F32),
        compiler_params=_params("arbitrary", "arbitrary"),
        name="ada_mod",
    )(c_all, ada_w, ada_b.reshape(DEPTH, 1, N_MODS * D_MODEL))


def _gmlp_kernel(x_ref, sh_ref, sc_ref, gt_ref, ng_ref, win_ref, lng_ref, lnb_ref, wmix_ref,
                 bmix_ref, wout_ref, xo_ref, v_ref, mix_ref, *, period):
    tm = x_ref.shape[0]
    x = x_ref[...]
    h = _modulate(x, ng_ref[...], sh_ref[...], sc_ref[...]).astype(BF16)
    z = _bdot(h, win_ref[...])
    z = 0.5 * z * (1.0 + lax.erf(z * (2.0 ** -0.5)))
    u = z[:, :D_GATE]
    v = z[:, D_GATE:]
    mu = jnp.mean(v, axis=-1, keepdims=True)
    vc = v - mu
    var = jnp.mean(vc * vc, axis=-1, keepdims=True)
    v = vc * lax.rsqrt(var + EPS) * lng_ref[...] + lnb_ref[...]
    v_ref[...] = v[tm - CHUNK:, :]
    vb = v.astype(BF16)
    t = lax.broadcasted_iota(jnp.int32, (CHUNK, CHUNK), 0)
    s = lax.broadcasted_iota(jnp.int32, (CHUNK, CHUNK), 1)
    keep = (s <= t) & (_div_pow2(t, period) == _div_pow2(s, period))
    cg = D_GATE // A_GROUPS
    for g in range(A_GROUPS):
        wm = jnp.where(keep, wmix_ref[g], 0.0).astype(BF16)
        for c in range(tm // CHUNK):
            rows = slice(c * CHUNK, (c + 1) * CHUNK)
            cols = slice(g * cg, (g + 1) * cg)
            mix_ref[rows, cols] = _bdot(wm, vb[rows, cols]) + bmix_ref[:, cols]
    y = _bdot((u * mix_ref[...]).astype(BF16), wout_ref[...])
    xo_ref[...] = x + gt_ref[...] * y


def _gmlp_layer(st, x, mods, ng, w_in, ln_g, ln_b, wmix, bmix, w_out, period, n_vblocks):
    kern = functools.partial(_gmlp_kernel, period=period)
    vb_per = st.grid // n_vblocks
    return pl.pallas_call(
        kern,
        grid=(st.grid,),
        in_specs=[st.rows_spec(), st.mod_spec(0), st.mod_spec(1), st.mod_spec(2),
                  _full_spec((1, D_MODEL)), _full_spec((D_MODEL, 2 * D_GATE)),
                  _full_spec((1, D_GATE)), _full_spec((1, D_GATE)),
                  _full_spec((A_GROUPS, CHUNK, CHUNK)), _full_spec((CHUNK, D_GATE)),
                  _full_spec((D_GATE, D_MODEL))],
        out_specs=[st.rows_spec(),
                   pl.BlockSpec((CHUNK, D_GATE), lambda i: (i // vb_per, 0))],
        out_shape=[jax.ShapeDtypeStruct((st.rows, D_MODEL), F32),
                   jax.ShapeDtypeStruct((n_vblocks * CHUNK, D_GATE), F32)],
        scratch_shapes=[pltpu.VMEM((st.tile, D_GATE), F32)],
        compiler_params=_params("arbitrary"),
        name="gmlp_mixer",
    )(x, mods, mods, mods, ng, w_in, ln_g, ln_b, wmix, bmix, w_out)


def _pool_kernel(x_ref, halo_ref, sh_ref, sc_ref, gt_ref, ng_ref, wgrp_ref, scale_ref,
                 xo_ref, rows_ref, *, halo_is_h, tiles_per_batch, pos0, n_rows_out):
    tm = x_ref.shape[0]
    x = x_ref[...]
    ng, sh, sc = ng_ref[...], sh_ref[...], sc_ref[...]
    h = _modulate(x, ng, sh, sc)
    if halo_is_h:
        halo = halo_ref[...]
    else:
        sh_h = sh if sh.shape[0] == 1 else sh[:POOL_HALO]
        sc_h = sc if sc.shape[0] == 1 else sc[:POOL_HALO]
        first = (pl.program_id(0) % tiles_per_batch) == 0
        halo = jnp.where(first, 0.0, _modulate(halo_ref[...], ng, sh_h, sc_h))
    ext = jnp.concatenate([halo, h], axis=0)
    rows_ref[...] = ext[POOL_HALO + tm - n_rows_out:, :]
    t0 = (pl.program_id(0) % tiles_per_batch) * tm
    pos = pos0 + t0 + lax.broadcasted_iota(jnp.int32, (tm, 1), 0)
    cg = D_MODEL // B_GROUPS
    ys = []
    for g, w in enumerate(POOL_WINDOWS):
        cols = slice(g * cg, (g + 1) * cg)
        acc = ext[:, cols]
        span = 1
        while span < w:
            acc = acc + pltpu.roll(acc, span, 0)
            span *= 2
        cnt = jnp.minimum(pos + 1, w).astype(F32)
        pooled = acc[POOL_HALO:, :] / cnt - h[:, cols]
        ys.append(_bdot(pooled.astype(BF16), wgrp_ref[g]))
    y = jnp.concatenate(ys, axis=-1) * scale_ref[...]
    xo_ref[...] = x + gt_ref[...] * y


def _pool_layer(st, x, halo, halo_spec, mods, ng, w_grp, scale, halo_is_h, pos0, n_rows_out, n_batches):
    kern = functools.partial(_pool_kernel, halo_is_h=halo_is_h, tiles_per_batch=st.tiles_per_batch,
                             pos0=pos0, n_rows_out=n_rows_out)
    cg = D_MODEL // B_GROUPS
    tpb = st.tiles_per_batch
    return pl.pallas_call(
        kern,
        grid=(st.grid,),
        in_specs=[st.rows_spec(), halo_spec, st.mod_spec(0), st.mod_spec(1), st.mod_spec(2),
                  _full_spec((1, D_MODEL)), _full_spec((B_GROUPS, cg, cg)), _full_spec((1, D_MODEL))],
        out_specs=[st.rows_spec(),
                   pl.BlockSpec((n_rows_out, D_MODEL), lambda i: (i // tpb, 0))],
        out_shape=[jax.ShapeDtypeStruct((st.rows, D_MODEL), F32),
                   jax.ShapeDtypeStruct((n_batches * n_rows_out, D_MODEL), F32)],
        compiler_params=_params("arbitrary"),
        name="pool_mixer",
    )(x, halo, mods, mods, mods, ng, w_grp, scale)


def _rope_angles(pos):
    half = ROT_DIM // 2
    inv = ROPE_THETA ** (-(jnp.arange(half, dtype=F32) * 2.0) / ROT_DIM)
    ang = pos.astype(F32)[:, None] * inv[None, :]
    return jnp.cos(ang), jnp.sin(ang)


def _rope_tables(pos):
    cos, sin = _rope_angles(pos)
    n = pos.shape[0]
    pad = jnp.zeros((n, HEAD_DIM - ROT_DIM), F32)
    c = jnp.concatenate([cos, cos, pad + 1.0], axis=1)
    sa = jnp.concatenate([-sin, jnp.zeros_like(sin), pad], axis=1)
    sb = jnp.concatenate([jnp.zeros_like(sin), sin, pad], axis=1)
    return tuple(jnp.concatenate([t, t], axis=1) for t in (c, sa, sb))


def _rope_rows(x, c, sa, sb):
    half = ROT_DIM // 2
    out = []
    for hd in range(N_HEADS):
        xh = x[:, hd * LANES:(hd + 1) * LANES]
        out.append(xh * c + pltpu.roll(xh, LANES - half, 1) * sa + pltpu.roll(xh, half, 1) * sb)
    return out


def _qkv_kernel(x_ref, sh_ref, sc_ref, ng_ref, w_ref, c_ref, sa_ref, sb_ref, q_ref, k_ref, v_ref):
    h = _modulate(x_ref[...], ng_ref[...], sh_ref[...], sc_ref[...]).astype(BF16)
    qkv = _bdot(h, w_ref[...])
    c, sa, sb = c_ref[...], sa_ref[...], sb_ref[...]
    for which, ref in ((0, q_ref), (1, k_ref)):
        rot = _rope_rows(qkv[:, which * D_MODEL:(which + 1) * D_MODEL], c, sa, sb)
        for hd in range(N_HEADS):
            ref[:, hd * LANES:(hd + 1) * LANES] = rot[hd]
    v_ref[...] = qkv[:, 2 * D_MODEL:]


def _qkv_layer(st, x, mods, ng, w_qkv, tables, table_spec):
    return pl.pallas_call(
        _qkv_kernel,
        grid=(st.grid,),
        in_specs=[st.rows_spec(), st.mod_spec(0), st.mod_spec(1), _full_spec((1, D_MODEL)),
                  _full_spec((D_MODEL, 3 * D_MODEL)), table_spec, table_spec, table_spec],
        out_specs=[st.rows_spec(), st.rows_spec(), st.rows_spec()],
        out_shape=[jax.ShapeDtypeStruct((st.rows, D_MODEL), F32)] * 3,
        compiler_params=_params("arbitrary"),
        name="qkv_rope",
    )(x, mods, mods, ng, w_qkv, *tables)


def _qkv_t_kernel(x_ref, sh_ref, sc_ref, ng_ref, wq_ref, wkt_ref, wv_ref, c_ref, sa_ref, sb_ref,
                  ct_ref, st_ref, q_ref, kt_ref, ktb_ref, v_ref, vb_ref):
    h = _modulate(x_ref[...], ng_ref[...], sh_ref[...], sc_ref[...]).astype(BF16)
    rot = _rope_rows(_bdot(h, wq_ref[...]), c_ref[...], sa_ref[...], sb_ref[...])
    for hd in range(N_HEADS):
        q_ref[:, hd * LANES:(hd + 1) * LANES] = (rot[hd] * (HEAD_DIM ** -0.5)).astype(q_ref.dtype)
    kt = lax.dot_general(wkt_ref[...], h, (((1,), (1,)), ((), ())), preferred_element_type=F32)
    ct, st = ct_ref[...], st_ref[...]
    half = ROT_DIM // 2
    for blk in range(2 * N_HEADS):
        base = blk * HEAD_DIM
        x1 = kt[base:base + half]
        x2 = kt[base + half:base + ROT_DIM]
        full = jnp.concatenate([x1 * ct - x2 * st, x2 * ct + x1 * st, kt[base + ROT_DIM:base + HEAD_DIM]], axis=0)
        kt_ref[base:base + HEAD_DIM, :] = full
        ktb_ref[base:base + HEAD_DIM, :] = full.astype(ktb_ref.dtype)
    v = _bdot(h, wv_ref[...])
    v_ref[...] = v
    vb_ref[...] = v.astype(vb_ref.dtype)


def _qkv_t_layer(st, x, mods, ng, wq, wkt, wv, tables, angles_t):
    tpb = st.tiles_per_batch
    tm = st.tile
    tab = pl.BlockSpec((tm, LANES), lambda t: (t % tpb, 0))
    tab_t = pl.BlockSpec((ROT_DIM // 2, tm), lambda t: (0, t % tpb))
    w = _full_spec((D_MODEL, D_MODEL))
    return pl.pallas_call(
        _qkv_t_kernel,
        grid=(st.grid,),
        in_specs=[st.rows_spec(), st.mod_spec(0), st.mod_spec(1), _full_spec((1, D_MODEL)), w, w, w,
                  tab, tab, tab, tab_t, tab_t],
        out_specs=[st.rows_spec(),
                   pl.BlockSpec((None, D_MODEL, tm), lambda t: (t // tpb, 0, t % tpb)),
                   pl.BlockSpec((None, None, D_MODEL, tm), lambda t: (t // tpb, t % tpb, 0, 0)),
                   st.rows_spec(), st.rows_spec()],
        out_shape=[jax.ShapeDtypeStruct((st.rows, D_MODEL), BF16),
                   jax.ShapeDtypeStruct((BATCH, D_MODEL, SEQ), F32),
                   jax.ShapeDtypeStruct((BATCH, tpb, D_MODEL, tm), BF16),
                   jax.ShapeDtypeStruct((st.rows, D_MODEL), F32),
                   jax.ShapeDtypeStruct((st.rows, D_MODEL), BF16)],
        compiler_params=_params("arbitrary"),
        name="qkv_rope_prompt",
    )(x, mods, mods, ng, wq, wkt, wv, *tables, *angles_t)


def _lambda(lq1_ref, lk1_ref, lq2_ref, lk2_ref, lambda_init):
    a = jnp.sum(lq1_ref[...] * lk1_ref[...], axis=-1, keepdims=True)
    b = jnp.sum(lq2_ref[...] * lk2_ref[...], axis=-1, keepdims=True)
    return jnp.exp(a) - jnp.exp(b) + lambda_init


def _attn_kernel(q_ref, kt_ref, v_ref, lq1_ref, lk1_ref, lq2_ref, lk2_ref, sg_ref, o_ref,
                 m_ref, l_ref, acc_ref, *, lambda_init):
    tq = q_ref.shape[0]
    qi = pl.program_id(2)
    q = q_ref[...]
    first = lax.broadcasted_iota(jnp.int32, (1, LANES), 1) < HEAD_DIM
    zero = jnp.zeros_like(q)
    qs = (jnp.where(first, q, zero), jnp.where(first, zero, q))
    m_ref[...] = jnp.full(m_ref.shape, -jnp.inf, F32)
    l_ref[...] = jnp.zeros(l_ref.shape, F32)
    acc_ref[...] = jnp.zeros(acc_ref.shape, F32)
    row = lax.broadcasted_iota(jnp.int32, (tq, tq), 0)
    col = lax.broadcasted_iota(jnp.int32, (tq, tq), 1)

    def key_tile(kb, diagonal):
        kblk = kt_ref[kb]
        vblk = v_ref[pl.ds(pl.multiple_of(kb * tq, tq), tq), :]
        for comp in range(2):
            s = _bdot(qs[comp], kblk)
            if diagonal:
                s = jnp.where(col <= row, s, -jnp.inf)
            m_old = m_ref[comp]
            m_new = jnp.maximum(m_old, jnp.max(s, axis=-1, keepdims=True))
            alpha = jnp.exp(m_old - m_new)
            p = jnp.exp(s - m_new)
            l_ref[comp] = l_ref[comp] * alpha + jnp.sum(p, axis=-1, keepdims=True)
            acc_ref[comp] = acc_ref[comp] * alpha + _bdot(p.astype(BF16), vblk)
            m_ref[comp] = m_new

    def body(kb, carry):
        key_tile(kb, False)
        return carry

    lax.fori_loop(0, qi, body, 0)
    key_tile(qi, True)
    lam = _lambda(lq1_ref, lk1_ref, lq2_ref, lk2_ref, lambda_init)
    o = acc_ref[0] / l_ref[0] - lam * (acc_ref[1] / l_ref[1])
    o_ref[...] = (_rms(o) * sg_ref[...] * (1.0 - lambda_init)).astype(o_ref.dtype)


def _attn_prompt(q, ktb, vb, lq1, lk1, lq2, lk2, subln_g, lambda_init):
    nq = SEQ // ATTN_Q_TILE
    kern = functools.partial(_attn_kernel, lambda_init=lambda_init)
    vec = pl.BlockSpec((1, HEAD_DIM), lambda b, h, i: (0, 0))
    return pl.pallas_call(
        kern,
        grid=(BATCH, N_HEADS, nq),
        in_specs=[pl.BlockSpec((ATTN_Q_TILE, LANES), lambda b, h, i: (b * nq + i, h)),
                  pl.BlockSpec((None, nq, LANES, ATTN_Q_TILE), lambda b, h, i: (b, 0, h, 0)),
                  pl.BlockSpec((SEQ, LANES), lambda b, h, i: (b, h)),
                  vec, vec, vec, vec,
                  pl.BlockSpec((1, 2 * HEAD_DIM), lambda b, h, i: (0, 0))],
        out_specs=pl.BlockSpec((ATTN_Q_TILE, LANES), lambda b, h, i: (b * nq + i, h)),
        out_shape=jax.ShapeDtypeStruct((N_PROMPT, D_MODEL), BF16),
        scratch_shapes=[pltpu.VMEM((2, ATTN_Q_TILE, 1), F32), pltpu.VMEM((2, ATTN_Q_TILE, 1), F32),
                        pltpu.VMEM((2, ATTN_Q_TILE, LANES), F32)],
        compiler_params=_params("arbitrary", "arbitrary", "arbitrary"),
        name="diff_attn_prompt",
    )(q, ktb, vb, lq1, lk1, lq2, lk2, subln_g)


DEC_ROWS = N_HEADS * 2 * DEC_SEQ


def _decode_kernel(pt_ref, q_ref, kn_ref, vn_ref, lq1_ref, lk1_ref, lq2_ref, lk2_ref, sg_ref, *rest,
                   lambda_init):
    k_refs = rest[:PAGES_PER_STEP]
    v_refs = rest[PAGES_PER_STEP:2 * PAGES_PER_STEP]
    o_ref, m_ref, l_ref, acc_ref = rest[2 * PAGES_PER_STEP:]
    step = pl.program_id(1)
    rows_per_head = 2 * DEC_SEQ
    lane = lax.broadcasted_iota(jnp.int32, (rows_per_head, D_MODEL), 1)
    comp = _div_pow2(lax.broadcasted_iota(jnp.int32, (rows_per_head, D_MODEL), 0), DEC_SEQ)
    q8 = q_ref[...] * (HEAD_DIM ** -0.5)
    blk = _div_pow2(lane, HEAD_DIM)
    qbd = jnp.concatenate([jnp.where(blk == 2 * hd + comp, q8, 0.0) for hd in range(N_HEADS)],
                          axis=0).astype(BF16)

    @pl.when(step == 0)
    def _():
        m_ref[...] = jnp.full(m_ref.shape, -jnp.inf, F32)
        l_ref[...] = jnp.zeros(l_ref.shape, F32)
        acc_ref[...] = jnp.zeros(acc_ref.shape, F32)

    m, l, acc = m_ref[...], l_ref[...], acc_ref[...]
    for kr, vr in zip(k_refs, v_refs):
        s = _bdot(qbd, kr[...].astype(BF16))
        m_new = jnp.maximum(m, jnp.max(s, axis=-1, keepdims=True))
        alpha = jnp.exp(m - m_new)
        p = jnp.exp(s - m_new)
        l = l * alpha + jnp.sum(p, axis=-1, keepdims=True)
        pv = []
        for hd in range(N_HEADS):
            rows = slice(hd * rows_per_head, (hd + 1) * rows_per_head)
            v_h = vr[pl.ds(hd, PAGE_SIZE, stride=N_HEADS), :]
            pv.append(_bdot(p[rows].astype(BF16), v_h.astype(BF16)))
        acc = acc * alpha + jnp.concatenate(pv, axis=0)
        m = m_new
    m_ref[...], l_ref[...], acc_ref[...] = m, l, acc

    @pl.when(step == pl.num_programs(1) - 1)
    def _():
        kn = kn_ref[...].astype(BF16).astype(F32)
        vn = vn_ref[...].astype(BF16).astype(F32)
        qf = qbd.astype(F32)
        row_q = _mod_pow2(lax.broadcasted_iota(jnp.int32, (DEC_ROWS, 1), 0), DEC_SEQ)
        s_new = []
        for j in range(DEC_SEQ):
            sj = jnp.sum(qf * kn[j:j + 1, :], axis=-1, keepdims=True)
            s_new.append(jnp.where(row_q >= j, sj, -jnp.inf))
        m2 = m
        for sj in s_new:
            m2 = jnp.maximum(m2, sj)
        alpha = jnp.exp(m - m2)
        l2 = l * alpha
        acc2 = acc * alpha
        for j, sj in enumerate(s_new):
            pj = jnp.exp(sj - m2)
            l2 = l2 + pj
            v_rows = jnp.concatenate(
                [jnp.broadcast_to(vn[j:j + 1, hd * LANES:(hd + 1) * LANES], (rows_per_head, LANES))
                 for hd in range(N_HEADS)], axis=0)
            acc2 = acc2 + pj.astype(BF16).astype(F32) * v_rows
        outn = acc2 / l2
        lam = _lambda(lq1_ref, lk1_ref, lq2_ref, lk2_ref, lambda_init)
        for hd in range(N_HEADS):
            r0 = hd * rows_per_head
            o = outn[r0:r0 + DEC_SEQ] - lam * outn[r0 + DEC_SEQ:r0 + rows_per_head]
            o_ref[:, hd * LANES:(hd + 1) * LANES] = _rms(o) * sg_ref[...] * (1.0 - lambda_init)


def _attn_decode(page_table, q8, k_new, v_new, cache_kt, cache_v, layer, lq1, lk1, lq2, lk2, subln_g, lambda_init):
    n_steps = N_PAGES // PAGES_PER_STEP
    kern = functools.partial(_decode_kernel, lambda_init=lambda_init)
    tok = pl.BlockSpec((None, DEC_SEQ, D_MODEL), lambda b, p, pt: (b, 0, 0))
    vec = pl.BlockSpec((1, HEAD_DIM), lambda b, p, pt: (0, 0))

    def page_spec(i):
        return pl.BlockSpec((None, None, N_HEADS * PAGE_SIZE, LANES),
                            lambda b, p, pt: (pt[b * N_PAGES + p * PAGES_PER_STEP + i], layer, 0, 0))

    grid_spec = pltpu.PrefetchScalarGridSpec(
        num_scalar_prefetch=1,
        grid=(DEC_BATCH, n_steps),
        in_specs=[pl.BlockSpec((None, 2 * DEC_SEQ, D_MODEL), lambda b, p, pt: (b, 0, 0)), tok, tok,
                  vec, vec, vec, vec, pl.BlockSpec((1, 2 * HEAD_DIM), lambda b, p, pt: (0, 0))]
                 + [page_spec(i) for i in range(PAGES_PER_STEP)] * 2,
        out_specs=tok,
        scratch_shapes=[pltpu.VMEM((DEC_ROWS, 1), F32), pltpu.VMEM((DEC_ROWS, 1), F32),
                        pltpu.VMEM((DEC_ROWS, LANES), F32)],
    )
    return pl.pallas_call(
        kern,
        grid_spec=grid_spec,
        out_shape=jax.ShapeDtypeStruct((DEC_BATCH, DEC_SEQ, D_MODEL), F32),
        compiler_params=_params("arbitrary", "arbitrary"),
        name="diff_attn_decode",
    )(page_table.reshape(-1), q8, k_new, v_new, lq1, lk1, lq2, lk2, subln_g,
      *([cache_kt] * PAGES_PER_STEP), *([cache_v] * PAGES_PER_STEP))


def _outproj_kernel(o_ref, x_ref, gt_ref, w_ref, xo_ref):
    xo_ref[...] = x_ref[...] + gt_ref[...] * _bdot(o_ref[...].astype(BF16), w_ref[...])


def _outproj_layer(st, o, x, mods, w_o):
    return pl.pallas_call(
        _outproj_kernel,
        grid=(st.grid,),
        in_specs=[st.rows_spec(), st.rows_spec(), st.mod_spec(2), _full_spec((D_MODEL, D_MODEL))],
        out_specs=st.rows_spec(),
        out_shape=jax.ShapeDtypeStruct((st.rows, D_MODEL), F32),
        compiler_params=_params("arbitrary"),
        name="attn_out_proj",
    )(o, x, mods, w_o)


N_ALL = N_PROMPT + N_SAMPLE
ROW_W = D_MODEL + LANES
PAIRS_PER_GROUP = EXPERTS_PER_GROUP * (EXPERTS_PER_GROUP - 1) // 2
N_BUCKETS = N_EXPERT_GROUPS * PAIRS_PER_GROUP
SORT_TILE = 256
N_SORT_TILES = (N_ALL + N_BUCKETS * (SORT_TILE - 1)) // SORT_TILE
N_SORT_ROWS = N_SORT_TILES * SORT_TILE
DMA_UNROLL = 8


def _router_kernel(x_ref, sh_ref, sc_ref, ng_ref, rw_ref, rb_ref, h_ref, route_ref):
    h = _modulate(x_ref[...], ng_ref[...], sh_ref[...], sc_ref[...])
    logits = jnp.dot(h, rw_ref[...], preferred_element_type=F32, precision=lax.Precision.HIGHEST)
    scores = jax.nn.sigmoid(logits)
    sel = scores + rb_ref[...]
    eid = lax.broadcasted_iota(jnp.int32, sel.shape, 1)
    grp = _div_pow2(eid, EXPERTS_PER_GROUP)
    neg = -jnp.inf

    def first_max(vals):
        m = jnp.max(vals, axis=-1, keepdims=True)
        idx = jnp.min(jnp.where(vals == m, eid, N_EXPERTS), axis=-1, keepdims=True)
        return m, idx

    best = None
    for g in range(N_EXPERT_GROUPS):
        vg = jnp.where(grp == g, sel, neg)
        m1, i1 = first_max(vg)
        m2, _ = first_max(jnp.where(eid == i1, neg, vg))
        gs = m1 + m2
        if best is None:
            best, best_score = jnp.zeros_like(i1), gs
        else:
            better = gs > best_score
            best = jnp.where(better, g, best)
            best_score = jnp.where(better, gs, best_score)
    masked = jnp.where(grp == best, sel, neg)
    _, ia = first_max(masked)
    _, ib = first_max(jnp.where(eid == ia, neg, masked))
    ga = jnp.sum(jnp.where(eid == ia, scores, 0.0), axis=-1, keepdims=True)
    gb = jnp.sum(jnp.where(eid == ib, scores, 0.0), axis=-1, keepdims=True)
    den = ga + gb
    a_first = ia < ib
    lo = jnp.where(a_first, ia, ib).astype(F32)
    hi = jnp.where(a_first, ib, ia).astype(F32)
    g_lo = jnp.where(a_first, ga, gb) / den
    g_hi = jnp.where(a_first, gb, ga) / den
    h_ref[:, :D_MODEL] = h
    lane = lax.broadcasted_iota(jnp.int32, (h.shape[0], LANES), 1)
    h_ref[:, D_MODEL:] = jnp.where(lane == 0, g_lo, jnp.where(lane == 1, g_hi, 0.0))
    col = lax.broadcasted_iota(jnp.int32, route_ref.shape, 1)
    route_ref[...] = jnp.where(col == 0, lo, jnp.where(col == 1, hi, jnp.where(col == 2, g_lo, g_hi)))


def _router_layer(st, x, mods, ng, router_w, router_b):
    return pl.pallas_call(
        _router_kernel,
        grid=(st.grid,),
        in_specs=[st.rows_spec(), st.mod_spec(3), st.mod_spec(4), _full_spec((1, D_MODEL)),
                  _full_spec((D_MODEL, N_EXPERTS)), _full_spec((1, N_EXPERTS))],
        out_specs=[st.rows_spec(ROW_W), st.rows_spec(4)],
        out_shape=[jax.ShapeDtypeStruct((st.rows, ROW_W), F32),
                   jax.ShapeDtypeStruct((st.rows, 4), F32)],
        compiler_params=_params("arbitrary"),
        name="moe_router",
    )(x, mods, mods, ng, router_w, router_b)


def _moe_plan(route):
    lo = route[:, 0].astype(jnp.int32)
    hi = route[:, 1].astype(jnp.int32)
    i, j = lo % EXPERTS_PER_GROUP, hi % EXPERTS_PER_GROUP
    pair = i * (2 * EXPERTS_PER_GROUP - 1 - i) // 2 + (j - i - 1)
    bucket = (lo // EXPERTS_PER_GROUP) * PAIRS_PER_GROUP + pair
    onehot = (bucket[:, None] == jnp.arange(N_BUCKETS)[None, :]).astype(jnp.int32)
    csum = jnp.cumsum(onehot, axis=0)
    counts = csum[-1]
    rank = jnp.sum(onehot * csum, axis=1) - 1
    tiles = (counts + SORT_TILE - 1) // SORT_TILE
    ends = jnp.cumsum(tiles)
    starts = ends - tiles
    n_used = ends[-1]
    pos = jnp.sum(onehot * starts[None, :], axis=1) * SORT_TILE + rank
    tile_id = jnp.arange(N_SORT_TILES)
    tile_bucket = jnp.sum((jnp.minimum(tile_id, n_used - 1)[:, None] >= ends[None, :]).astype(jnp.int32), axis=1)
    tile_bucket = jnp.minimum(tile_bucket, N_BUCKETS - 1)
    pair_lo = jnp.array([a for a in range(EXPERTS_PER_GROUP) for _ in range(a + 1, EXPERTS_PER_GROUP)], jnp.int32)
    pair_hi = jnp.array([b for a in range(EXPERTS_PER_GROUP) for b in range(a + 1, EXPERTS_PER_GROUP)], jnp.int32)
    base = (tile_bucket // PAIRS_PER_GROUP) * EXPERTS_PER_GROUP
    tile_lo = base + pair_lo[tile_bucket % PAIRS_PER_GROUP]
    tile_hi = base + pair_hi[tile_bucket % PAIRS_PER_GROUP]
    return pos.astype(jnp.int32), tile_lo, tile_hi, n_used.reshape(1).astype(jnp.int32)


def _scatter_rows_kernel(pos_ref, src_p_ref, src_s_ref, dst_in_ref, dst_ref, sem):
    del dst_in_ref
    row0 = 0
    for src_ref in (src_p_ref, src_s_ref):
        n = src_ref.shape[0]

        def body(i, carry, src_ref=src_ref, row0=row0):
            for u in range(DMA_UNROLL):
                t = i * DMA_UNROLL + u
                pltpu.make_async_copy(src_ref.at[pl.ds(t, 1), :],
                                      dst_ref.at[pl.ds(pos_ref[row0 + t], 1), :], sem).start()
            return carry

        lax.fori_loop(0, n // DMA_UNROLL, body, 0)
        row0 += n
    for src_ref in (src_p_ref, src_s_ref):
        pltpu.make_async_copy(src_ref, dst_ref.at[pl.ds(0, src_ref.shape[0]), :], sem).wait()


def _scatter_rows(pos, src_p, src_s, n_dst):
    hbm = pl.BlockSpec(memory_space=pl.ANY)
    grid_spec = pltpu.PrefetchScalarGridSpec(
        num_scalar_prefetch=1,
        grid=(1,),
        in_specs=[hbm, hbm, hbm],
        out_specs=hbm,
        scratch_shapes=[pltpu.SemaphoreType.DMA(())],
    )
    return pl.pallas_call(
        _scatter_rows_kernel,
        grid_spec=grid_spec,
        out_shape=jax.ShapeDtypeStruct((n_dst, ROW_W), F32),
        input_output_aliases={3: 0},
        compiler_params=_params("arbitrary"),
        name="moe_bucket_rows",
    )(pos, src_p, src_s, jnp.zeros((n_dst, ROW_W), F32))


def _moe_kernel(lo_ref, hi_ref, nu_ref, xs_ref, wg_lo, wu_lo, wd_lo, wg_hi, wu_hi, wd_hi, ys_ref):
    del lo_ref, hi_ref
    t = pl.program_id(0)

    @pl.when(t < nu_ref[0])
    def _():
        row = xs_ref[...]
        x = row[:, :D_MODEL].astype(BF16)

        def expert(wg, wu, wd, gate):
            g = _bdot(x, wg[...])
            u = _bdot(x, wu[...])
            return _bdot((jax.nn.silu(g) * u * gate).astype(BF16), wd[...])

        ys_ref[...] = (expert(wg_lo, wu_lo, wd_lo, row[:, D_MODEL:D_MODEL + 1])
                       + expert(wg_hi, wu_hi, wd_hi, row[:, D_MODEL + 1:D_MODEL + 2]))

    @pl.when(t >= nu_ref[0])
    def _():
        ys_ref[...] = jnp.zeros(ys_ref.shape, ys_ref.dtype)


def _moe_experts(xs, tile_lo, tile_hi, n_used, wg, wu, wd, layer):
    def w_spec(shape, which):
        return pl.BlockSpec((None, None) + shape,
                            lambda t, lo, hi, nu: (layer, (lo, hi)[which][t], 0, 0))

    up, down = (D_MODEL, D_EXPERT), (D_EXPERT, D_MODEL)
    grid_spec = pltpu.PrefetchScalarGridSpec(
        num_scalar_prefetch=3,
        grid=(N_SORT_TILES,),
        in_specs=[pl.BlockSpec((SORT_TILE, ROW_W), lambda t, lo, hi, nu: (t, 0)),
                  w_spec(up, 0), w_spec(up, 0), w_spec(down, 0),
                  w_spec(up, 1), w_spec(up, 1), w_spec(down, 1)],
        out_specs=pl.BlockSpec((SORT_TILE, D_MODEL), lambda t, lo, hi, nu: (t, 0)),
    )
    return pl.pallas_call(
        _moe_kernel,
        grid_spec=grid_spec,
        out_shape=jax.ShapeDtypeStruct((N_SORT_ROWS, D_MODEL), F32),
        compiler_params=_params("arbitrary"),
        name="moe_experts",
    )(tile_lo, tile_hi, n_used, xs, wg, wu, wd, wg, wu, wd)


def _unsort_kernel(pos_ref, ys_ref, x_ref, g2_ref, o_ref, buf, sem, *, row0):
    tm = x_ref.shape[0]
    t = pl.program_id(0)

    def fetch(tile, slot):
        base = row0 + tile * tm

        def body(i, carry):
            for u in range(DMA_UNROLL):
                r = i * DMA_UNROLL + u
                pltpu.make_async_copy(ys_ref.at[pl.ds(pos_ref[base + r], 1), :],
                                      buf.at[slot, pl.ds(r, 1), :], sem.at[slot]).start()
            return carry

        lax.fori_loop(0, tm // DMA_UNROLL, body, 0)

    @pl.when(t == 0)
    def _():
        fetch(0, 0)

    @pl.when(t + 1 < pl.num_programs(0))
    def _():
        fetch(t + 1, (t + 1) % 2)

    slot = t % 2
    pltpu.make_async_copy(ys_ref.at[pl.ds(0, tm), :], buf.at[slot], sem.at[slot]).wait()
    o_ref[...] = x_ref[...] + g2_ref[...] * buf[slot]


def _unsort_layer(st, pos, ys, x, mods, row0):
    kern = functools.partial(_unsort_kernel, row0=row0)
    if st.per_row_mods:
        g2_spec = pl.BlockSpec((st.tile, D_MODEL), lambda i, p: (i, 5))
    else:
        tpb = st.tiles_per_batch
        g2_spec = pl.BlockSpec((None, None, 1, D_MODEL), lambda i, p: (i // tpb, 5, 0, 0))
    rows = pl.BlockSpec((st.tile, D_MODEL), lambda i, p: (i, 0))
    grid_spec = pltpu.PrefetchScalarGridSpec(
        num_scalar_prefetch=1,
        grid=(st.grid,),
        in_specs=[pl.BlockSpec(memory_space=pl.ANY), rows, g2_spec],
        out_specs=rows,
        scratch_shapes=[pltpu.VMEM((2, st.tile, D_MODEL), F32), pltpu.SemaphoreType.DMA((2,))],
    )
    return pl.pallas_call(
        kern,
        grid_spec=grid_spec,
        out_shape=jax.ShapeDtypeStruct((st.rows, D_MODEL), F32),
        compiler_params=_params("arbitrary"),
        name="moe_unsort_residual",
    )(pos, ys, x, mods)


def _final_kernel(x_ref, g_ref, o_ref):
    o_ref[...] = _rms(x_ref[...]) * g_ref[...]


def _final_norm(st, x, g):
    return pl.pallas_call(
        _final_kernel,
        grid=(st.grid,),
        in_specs=[st.rows_spec(), _full_spec((1, D_MODEL))],
        out_specs=st.rows_spec(),
        out_shape=jax.ShapeDtypeStruct((st.rows, D_MODEL), F32),
        compiler_params=_params("arbitrary"),
        name="final_norm",
    )(x, g)


def kernel(x_prompt, x_sample, cache_k, cache_v, state_pool, page_table, c_prompt, c_sample,
           norm1_g, norm2_g, ada_w, ada_b, a_w_in, a_ln_g, a_ln_b, a_w_s, a_b_s, a_w_out,
           b_w_grp, b_scale, c_w_qkv, c_lq1, c_lk1, c_lq2, c_lk2, c_subln_g, c_w_o,
           router_w, router_bias, e_w_gate, e_w_up, e_w_down, final_g):
    sp = _Stream(N_PROMPT, PROMPT_TILE, SEQ // PROMPT_TILE, per_row_mods=False)
    ss = _Stream(N_SAMPLE, N_SAMPLE, 1, per_row_mods=True)

    xp = x_prompt.reshape(N_PROMPT, D_MODEL)
    xs = x_sample.reshape(N_SAMPLE, D_MODEL)
    mods_all = _ada(jnp.concatenate([c_prompt, c_sample], axis=0), ada_w, ada_b)

    wg_b, wu_b, wd_b = e_w_gate.astype(BF16), e_w_up.astype(BF16), e_w_down.astype(BF16)
    rb = router_bias.reshape(1, N_EXPERTS)
    n_phys = cache_k.shape[0]
    n_c_layers = cache_k.shape[1]
    ckt = jnp.transpose(cache_k, (0, 1, 3, 4, 5, 2)).reshape(n_phys, n_c_layers, D_MODEL, PAGE_SIZE)
    cv = cache_v.reshape(n_phys, n_c_layers, PAGE_SIZE * N_HEADS, 2 * HEAD_DIM)

    av_p, av_s, pr_p, pr_s, kp_l, vp_l, ks_l, vs_l = [], [], [], [], [], [], [], []
    for i in range(DEPTH):
        kind, j = i % N_MIXERS, i // N_MIXERS
        mp = mods_all[i, :BATCH].reshape(BATCH, N_MODS, 1, D_MODEL)
        ms = jnp.repeat(mods_all[i, BATCH:], DEC_SEQ, axis=0)
        ng1 = norm1_g[i].reshape(1, D_MODEL)
        ng2 = norm2_g[i].reshape(1, D_MODEL)
        if kind == 0:
            w_in = a_w_in[j].astype(BF16)
            w_out = a_w_out[j].astype(BF16)
            ln_g, ln_b = a_ln_g[j].reshape(1, D_GATE), a_ln_b[j].reshape(1, D_GATE)
            bias = jnp.repeat(a_b_s[j].T, D_GATE // A_GROUPS, axis=1)
            xp, vrow_p = _gmlp_layer(sp, xp, mp, ng1, w_in, ln_g, ln_b, a_w_s[j], bias, w_out, CHUNK, BATCH)
            reps = CHUNK // DEC_SEQ
            wmix_s = jnp.tile(a_w_s[j][:, :DEC_SEQ, :DEC_SEQ], (1, reps, reps))
            bias_s = jnp.tile(bias[:DEC_SEQ], (reps, 1))
            xs, vrow_s = _gmlp_layer(ss, xs, ms, ng1, w_in, ln_g, ln_b, wmix_s, bias_s, w_out, DEC_SEQ, 1)
            av_p.append(vrow_p.reshape(BATCH, CHUNK, D_GATE))
            av_s.append(vrow_s.reshape(DEC_BATCH, DEC_SEQ, D_GATE))
        elif kind == 1:
            w_grp = b_w_grp[j].astype(BF16)
            scale = b_scale[j].reshape(1, D_MODEL)
            hb = PROMPT_TILE // POOL_HALO
            halo_spec = pl.BlockSpec((POOL_HALO, D_MODEL), lambda t: (jnp.maximum(t * hb - 1, 0), 0))
            xp, rows_p = _pool_layer(sp, xp, xp, halo_spec, mp, ng1, w_grp, scale, False, 0, POOL_HALO, BATCH)
            pr_p.append(rows_p.reshape(BATCH, POOL_HALO, D_MODEL)[:, 1:])
            pad_t = 2 * DEC_SEQ
            s8 = _Stream(DEC_BATCH * pad_t, pad_t, 1, per_row_mods=True)
            xs8 = jnp.pad(xs.reshape(DEC_BATCH, DEC_SEQ, D_MODEL), ((0, 0), (0, DEC_SEQ), (0, 0)))
            ms8 = jnp.repeat(mods_all[i, BATCH:], pad_t, axis=0)
            prev = jnp.pad(state_pool[:, j], ((0, 0), (1, 0), (0, 0))).reshape(DEC_BATCH * POOL_HALO, D_MODEL)
            prev_spec = pl.BlockSpec((POOL_HALO, D_MODEL), lambda t: (t, 0))
            n_out = POOL_HALO + pad_t
            xs8, rows_s = _pool_layer(s8, xs8.reshape(-1, D_MODEL), prev, prev_spec, ms8, ng1, w_grp, scale,
                                      True, PAST_LEN, n_out, DEC_BATCH)
            xs = xs8.reshape(DEC_BATCH, pad_t, D_MODEL)[:, :DEC_SEQ].reshape(N_SAMPLE, D_MODEL)
            lo = 1 + DEC_SEQ
            pr_s.append(rows_s.reshape(DEC_BATCH, n_out, D_MODEL)[:, lo:lo + POOL_BUF])
        else:
            lambda_init = 0.8 - 0.6 * math.exp(-0.3 * i)
            w_qkv = c_w_qkv[j].astype(BF16)
            w_o = c_w_o[j].astype(BF16)
            lq1, lk1 = c_lq1[j].reshape(1, HEAD_DIM), c_lk1[j].reshape(1, HEAD_DIM)
            lq2, lk2 = c_lq2[j].reshape(1, HEAD_DIM), c_lk2[j].reshape(1, HEAD_DIM)
            sg = c_subln_g[j].reshape(1, 2 * HEAD_DIM)
            pos_p = jnp.arange(SEQ)
            cos_p, sin_p = _rope_angles(pos_p)
            qp, kt, ktb, vp, vpb = _qkv_t_layer(
                sp, xp, mp, ng1, w_qkv[:, :D_MODEL], w_qkv[:, D_MODEL:2 * D_MODEL].T, w_qkv[:, 2 * D_MODEL:],
                _rope_tables(pos_p), (cos_p.T, sin_p.T))
            op = _attn_prompt(qp, ktb, vpb, lq1, lk1, lq2, lk2, sg, lambda_init)
            xp = _outproj_layer(sp, op, xp, mp, w_o)
            tab_s = _rope_tables(jnp.tile(PAST_LEN + jnp.arange(DEC_SEQ), DEC_BATCH))
            qs, ks, vs = _qkv_layer(ss, xs, ms, ng1, w_qkv, tab_s,
                                    pl.BlockSpec((N_SAMPLE, LANES), lambda t: (0, 0)))
            tok = (DEC_BATCH, DEC_SEQ, D_MODEL)
            q8 = jnp.concatenate([qs.reshape(tok)] * 2, axis=1)
            os_ = _attn_decode(page_table, q8, ks.reshape(tok), vs.reshape(tok), ckt, cv, j,
                               lq1, lk1, lq2, lk2, sg, lambda_init)
            xs = _outproj_layer(ss, os_.reshape(N_SAMPLE, D_MODEL), xs, ms, w_o)
            kp = jnp.transpose(kt.reshape(BATCH, N_HEADS, 2, HEAD_DIM, SEQ), (0, 4, 1, 2, 3))
            kp_l.append(kp)
            vp_l.append(vp.reshape(BATCH, SEQ, N_HEADS, 2 * HEAD_DIM))
            ks_l.append(ks.reshape(DEC_BATCH, DEC_SEQ, N_HEADS, 2, HEAD_DIM))
            vs_l.append(vs.reshape(DEC_BATCH, DEC_SEQ, N_HEADS, 2 * HEAD_DIM))

        rows_p, route_p = _router_layer(sp, xp, mp, ng2, router_w, rb)
        rows_s, route_s = _router_layer(ss, xs, ms, ng2, router_w, rb)
        pos, tile_lo, tile_hi, n_used = _moe_plan(jnp.concatenate([route_p, route_s], axis=0))
        bucketed = _scatter_rows(pos, rows_p, rows_s, N_SORT_ROWS)
        ys = _moe_experts(bucketed, tile_lo, tile_hi, n_used, wg_b, wu_b, wd_b, i)
        xp = _unsort_layer(sp, pos, ys, xp, mp, 0)
        xs = _unsort_layer(ss, pos, ys, xs, ms, N_PROMPT)

    fg = final_g.reshape(1, D_MODEL)
    y_prompt = _final_norm(sp, xp, fg).reshape(BATCH, SEQ, D_MODEL)
    y_sample = _final_norm(ss, xs, fg).reshape(DEC_BATCH, DEC_SEQ, D_MODEL)
    return (y_prompt, y_sample,
            jnp.stack(av_p, axis=1), jnp.stack(av_s, axis=1),
            jnp.stack(pr_p, axis=1), jnp.stack(pr_s, axis=1),
            jnp.stack(kp_l, axis=1), jnp.stack(vp_l, axis=1),
            jnp.stack(ks_l, axis=1), jnp.stack(vs_l, axis=1))
```

```python
import functools
import math

import jax
import jax.numpy as jnp
from jax import lax
from jax.experimental import pallas as pl
from jax.experimental.pallas import tpu as pltpu

F32 = jnp.float32
BF16 = jnp.bfloat16

D_MODEL = 1024
BATCH = 8
SEQ = 2048
DEPTH = 4
DEC_BATCH = 32
DEC_SEQ = 4
PAST_LEN = 8192
PAGE_SIZE = 128
N_PAGES = PAST_LEN // PAGE_SIZE
N_MIXERS = 3
CHUNK = 128
D_GATE = D_MODEL
A_GROUPS = 4
POOL_WINDOWS = (2, 4, 8, 16)
B_GROUPS = len(POOL_WINDOWS)
POOL_BUF = max(POOL_WINDOWS) - 1
POOL_HALO = POOL_BUF + 1
N_HEADS = 8
HEAD_DIM = D_MODEL // N_HEADS // 2
ROT_DIM = HEAD_DIM // 4
ROPE_THETA = 500000.0
N_EXPERTS = 16
N_EXPERT_GROUPS = 4
EXPERTS_PER_GROUP = N_EXPERTS // N_EXPERT_GROUPS
D_EXPERT = 512
EPS = 1e-6
N_MODS = 6

N_PROMPT = BATCH * SEQ
N_SAMPLE = DEC_BATCH * DEC_SEQ
LANES = 128
VMEM_LIMIT = 56 * 1024 * 1024

PROMPT_TILE = 256
ATTN_Q_TILE = 256
PAGES_PER_STEP = 8


def _params(*sem):
    return pltpu.CompilerParams(dimension_semantics=sem, vmem_limit_bytes=VMEM_LIMIT)


class _Stream:
    def __init__(self, rows, tile, tiles_per_batch, per_row_mods):
        self.rows = rows
        self.tile = tile
        self.tiles_per_batch = tiles_per_batch
        self.per_row_mods = per_row_mods
        self.grid = rows // tile

    def rows_spec(self, width=D_MODEL):
        return pl.BlockSpec((self.tile, width), lambda i: (i, 0))

    def mod_spec(self, k):
        if self.per_row_mods:
            return pl.BlockSpec((self.tile, D_MODEL), lambda i: (i, k))
        tpb = self.tiles_per_batch
        return pl.BlockSpec((None, None, 1, D_MODEL), lambda i: (i // tpb, k, 0, 0))


def _full_spec(shape):
    nd = len(shape)
    return pl.BlockSpec(shape, lambda i: (0,) * nd)


def _rms(x):
    return x * lax.rsqrt(jnp.mean(x * x, axis=-1, keepdims=True) + EPS)


def _modulate(x, g, sh, sc):
    return _rms(x) * g * (1.0 + sc) + sh


def _bdot(a, b):
    return jnp.dot(a, b, preferred_element_type=F32)


def _div_pow2(x, n):
    assert n & (n - 1) == 0
    return lax.shift_right_logical(x, n.bit_length() - 1)


def _mod_pow2(x, n):
    assert n & (n - 1) == 0
    return x & (n - 1)


def _ada_kernel(c_ref, w_ref, b_ref, o_ref):
    a = jax.nn.silu(c_ref[...]).astype(BF16)
    o_ref[...] = _bdot(a, w_ref[...].astype(BF16)) + b_ref[...]


def _ada(c_all, ada_w, ada_b):
    nb = c_all.shape[0]
    tn = 1536
    return pl.pallas_call(
        _ada_kernel,
        grid=(DEPTH, N_MODS * D_MODEL // tn),
        in_specs=[pl.BlockSpec((nb, D_MODEL), lambda l, n: (0, 0)),
                  pl.BlockSpec((None, D_MODEL, tn), lambda l, n: (l, 0, n)),
                  pl.BlockSpec((None, 1, tn), lambda l, n: (l, 0, n))],
        out_specs=pl.BlockSpec((None, nb, tn), lambda l, n: (l, 0, n)),
        out_shape=jax.ShapeDtypeStruct((DEPTH, nb, N_MODS * D_MODEL), F32),
        compiler_params=_params("arbitrary", "arbitrary"),
        name="ada_mod",
    )(c_all, ada_w, ada_b.reshape(DEPTH, 1, N_MODS * D_MODEL))


def _mm(a, b, precise):
    if precise:
        return jnp.dot(a.astype(F32), b.astype(F32), preferred_element_type=F32, precision=lax.Precision.HIGHEST)
    return _bdot(a.astype(BF16), b.astype(BF16))


def _gmlp_kernel(x_ref, sh_ref, sc_ref, gt_ref, ng_ref, win_ref, lng_ref, lnb_ref, wmix_ref,
                 bmix_ref, wout_ref, xo_ref, v_ref, mix_ref, *, period, precise):
    tm = x_ref.shape[0]
    x = x_ref[...]
    h = _modulate(x, ng_ref[...], sh_ref[...], sc_ref[...])
    z = _mm(h, win_ref[...], precise)
    z = 0.5 * z * (1.0 + lax.erf(z * (2.0 ** -0.5)))
    u = z[:, :D_GATE]
    v = z[:, D_GATE:]
    mu = jnp.mean(v, axis=-1, keepdims=True)
    vc = v - mu
    var = jnp.mean(vc * vc, axis=-1, keepdims=True)
    v = vc * lax.rsqrt(var + EPS) * lng_ref[...] + lnb_ref[...]
    v_ref[...] = v[tm - CHUNK:, :]
    vb = v if precise else v.astype(BF16)
    t = lax.broadcasted_iota(jnp.int32, (CHUNK, CHUNK), 0)
    s = lax.broadcasted_iota(jnp.int32, (CHUNK, CHUNK), 1)
    keep = (s <= t) & (_div_pow2(t, period) == _div_pow2(s, period))
    cg = D_GATE // A_GROUPS
    for g in range(A_GROUPS):
        wm = jnp.where(keep, wmix_ref[g], 0.0)
        for c in range(tm // CHUNK):
            rows = slice(c * CHUNK, (c + 1) * CHUNK)
            cols = slice(g * cg, (g + 1) * cg)
            mix_ref[rows, cols] = _mm(wm, vb[rows, cols], precise) + bmix_ref[:, cols]
    y = _mm(u * mix_ref[...], wout_ref[...], precise)
    xo_ref[...] = x + gt_ref[...] * y


def _gmlp_layer(st, x, mods, ng, w_in, ln_g, ln_b, wmix, bmix, w_out, period, n_vblocks, precise):
    kern = functools.partial(_gmlp_kernel, period=period, precise=precise)
    vb_per = st.grid // n_vblocks
    return pl.pallas_call(
        kern,
        grid=(st.grid,),
        in_specs=[st.rows_spec(), st.mod_spec(0), st.mod_spec(1), st.mod_spec(2),
                  _full_spec((1, D_MODEL)), _full_spec((D_MODEL, 2 * D_GATE)),
                  _full_spec((1, D_GATE)), _full_spec((1, D_GATE)),
                  _full_spec((A_GROUPS, CHUNK, CHUNK)), _full_spec((CHUNK, D_GATE)),
                  _full_spec((D_GATE, D_MODEL))],
        out_specs=[st.rows_spec(),
                   pl.BlockSpec((CHUNK, D_GATE), lambda i: (i // vb_per, 0))],
        out_shape=[jax.ShapeDtypeStruct((st.rows, D_MODEL), F32),
                   jax.ShapeDtypeStruct((n_vblocks * CHUNK, D_GATE), F32)],
        scratch_shapes=[pltpu.VMEM((st.tile, D_GATE), F32)],
        compiler_params=_params("arbitrary"),
        name="gmlp_mixer",
    )(x, mods, mods, mods, ng, w_in, ln_g, ln_b, wmix, bmix, w_out)


def _pool_kernel(x_ref, halo_ref, sh_ref, sc_ref, gt_ref, ng_ref, wgrp_ref, scale_ref,
                 xo_ref, rows_ref, *, halo_is_h, tiles_per_batch, pos0, n_rows_out):
    tm = x_ref.shape[0]
    x = x_ref[...]
    ng, sh, sc = ng_ref[...], sh_ref[...], sc_ref[...]
    h = _modulate(x, ng, sh, sc)
    if halo_is_h:
        halo = halo_ref[...]
    else:
        sh_h = sh if sh.shape[0] == 1 else sh[:POOL_HALO]
        sc_h = sc if sc.shape[0] == 1 else sc[:POOL_HALO]
        first = (pl.program_id(0) % tiles_per_batch) == 0
        halo = jnp.where(first, 0.0, _modulate(halo_ref[...], ng, sh_h, sc_h))
    ext = jnp.concatenate([halo, h], axis=0)
    rows_ref[...] = ext[POOL_HALO + tm - n_rows_out:, :]
    t0 = (pl.program_id(0) % tiles_per_batch) * tm
    pos = pos0 + t0 + lax.broadcasted_iota(jnp.int32, (tm, 1), 0)
    cg = D_MODEL // B_GROUPS
    ys = []
    for g, w in enumerate(POOL_WINDOWS):
        cols = slice(g * cg, (g + 1) * cg)
        acc = ext[:, cols]
        span = 1
        while span < w:
            acc = acc + pltpu.roll(acc, span, 0)
            span *= 2
        cnt = jnp.minimum(pos + 1, w).astype(F32)
        pooled = acc[POOL_HALO:, :] / cnt - h[:, cols]
        ys.append(_bdot(pooled.astype(BF16), wgrp_ref[g]))
    y = jnp.concatenate(ys, axis=-1) * scale_ref[...]
    xo_ref[...] = x + gt_ref[...] * y


def _pool_layer(st, x, halo, halo_spec, mods, ng, w_grp, scale, halo_is_h, pos0, n_rows_out, n_batches):
    kern = functools.partial(_pool_kernel, halo_is_h=halo_is_h, tiles_per_batch=st.tiles_per_batch,
                             pos0=pos0, n_rows_out=n_rows_out)
    cg = D_MODEL // B_GROUPS
    tpb = st.tiles_per_batch
    return pl.pallas_call(
        kern,
        grid=(st.grid,),
        in_specs=[st.rows_spec(), halo_spec, st.mod_spec(0), st.mod_spec(1), st.mod_spec(2),
                  _full_spec((1, D_MODEL)), _full_spec((B_GROUPS, cg, cg)), _full_spec((1, D_MODEL))],
        out_specs=[st.rows_spec(),
                   pl.BlockSpec((n_rows_out, D_MODEL), lambda i: (i // tpb, 0))],
        out_shape=[jax.ShapeDtypeStruct((st.rows, D_MODEL), F32),
                   jax.ShapeDtypeStruct((n_batches * n_rows_out, D_MODEL), F32)],
        compiler_params=_params("arbitrary"),
        name="pool_mixer",
    )(x, halo, mods, mods, mods, ng, w_grp, scale)


def _rope_angles(pos):
    half = ROT_DIM // 2
    inv = ROPE_THETA ** (-(jnp.arange(half, dtype=F32) * 2.0) / ROT_DIM)
    ang = pos.astype(F32)[:, None] * inv[None, :]
    return jnp.cos(ang), jnp.sin(ang)


def _rope_tables(pos):
    cos, sin = _rope_angles(pos)
    n = pos.shape[0]
    pad = jnp.zeros((n, HEAD_DIM - ROT_DIM), F32)
    c = jnp.concatenate([cos, cos, pad + 1.0], axis=1)
    sa = jnp.concatenate([-sin, jnp.zeros_like(sin), pad], axis=1)
    sb = jnp.concatenate([jnp.zeros_like(sin), sin, pad], axis=1)
    return tuple(jnp.concatenate([t, t], axis=1) for t in (c, sa, sb))


def _rope_rows(x, c, sa, sb):
    half = ROT_DIM // 2
    out = []
    for hd in range(N_HEADS):
        xh = x[:, hd * LANES:(hd + 1) * LANES]
        out.append(xh * c + pltpu.roll(xh, LANES - half, 1) * sa + pltpu.roll(xh, half, 1) * sb)
    return out


def _qkv_kernel(x_ref, sh_ref, sc_ref, ng_ref, w_ref, c_ref, sa_ref, sb_ref, q_ref, k_ref, v_ref):
    h = _modulate(x_ref[...], ng_ref[...], sh_ref[...], sc_ref[...]).astype(BF16)
    qkv = _bdot(h, w_ref[...])
    c, sa, sb = c_ref[...], sa_ref[...], sb_ref[...]
    for which, ref in ((0, q_ref), (1, k_ref)):
        rot = _rope_rows(qkv[:, which * D_MODEL:(which + 1) * D_MODEL], c, sa, sb)
        for hd in range(N_HEADS):
            ref[:, hd * LANES:(hd + 1) * LANES] = rot[hd]
    v_ref[...] = qkv[:, 2 * D_MODEL:]


def _qkv_layer(st, x, mods, ng, w_qkv, tables, table_spec):
    return pl.pallas_call(
        _qkv_kernel,
        grid=(st.grid,),
        in_specs=[st.rows_spec(), st.mod_spec(0), st.mod_spec(1), _full_spec((1, D_MODEL)),
                  _full_spec((D_MODEL, 3 * D_MODEL)), table_spec, table_spec, table_spec],
        out_specs=[st.rows_spec(), st.rows_spec(), st.rows_spec()],
        out_shape=[jax.ShapeDtypeStruct((st.rows, D_MODEL), F32)] * 3,
        compiler_params=_params("arbitrary"),
        name="qkv_rope",
    )(x, mods, mods, ng, w_qkv, *tables)


def _qkv_t_kernel(x_ref, sh_ref, sc_ref, ng_ref, wq_ref, wkt_ref, wv_ref, c_ref, sa_ref, sb_ref,
                  ct_ref, st_ref, q_ref, kt_ref, ktb_ref, v_ref, vb_ref):
    h = _modulate(x_ref[...], ng_ref[...], sh_ref[...], sc_ref[...]).astype(BF16)
    rot = _rope_rows(_bdot(h, wq_ref[...]), c_ref[...], sa_ref[...], sb_ref[...])
    for hd in range(N_HEADS):
        q_ref[:, hd * LANES:(hd + 1) * LANES] = (rot[hd] * (HEAD_DIM ** -0.5)).astype(q_ref.dtype)
    kt = lax.dot_general(wkt_ref[...], h, (((1,), (1,)), ((), ())), preferred_element_type=F32)
    ct, st = ct_ref[...], st_ref[...]
    half = ROT_DIM // 2
    for blk in range(2 * N_HEADS):
        base = blk * HEAD_DIM
        x1 = kt[base:base + half]
        x2 = kt[base + half:base + ROT_DIM]
        full = jnp.concatenate([x1 * ct - x2 * st, x2 * ct + x1 * st, kt[base + ROT_DIM:base + HEAD_DIM]], axis=0)
        kt_ref[base:base + HEAD_DIM, :] = full
        ktb_ref[base:base + HEAD_DIM, :] = full.astype(ktb_ref.dtype)
    v = _bdot(h, wv_ref[...])
    v_ref[...] = v
    vb_ref[...] = v.astype(vb_ref.dtype)


def _qkv_t_layer(st, x, mods, ng, wq, wkt, wv, tables, angles_t):
    tpb = st.tiles_per_batch
    tm = st.tile
    tab = pl.BlockSpec((tm, LANES), lambda t: (t % tpb, 0))
    tab_t = pl.BlockSpec((ROT_DIM // 2, tm), lambda t: (0, t % tpb))
    w = _full_spec((D_MODEL, D_MODEL))
    return pl.pallas_call(
        _qkv_t_kernel,
        grid=(st.grid,),
        in_specs=[st.rows_spec(), st.mod_spec(0), st.mod_spec(1), _full_spec((1, D_MODEL)), w, w, w,
                  tab, tab, tab, tab_t, tab_t],
        out_specs=[st.rows_spec(),
                   pl.BlockSpec((None, D_MODEL, tm), lambda t: (t // tpb, 0, t % tpb)),
                   pl.BlockSpec((None, D_MODEL, tm), lambda t: (t // tpb, 0, t % tpb)),
                   st.rows_spec(), st.rows_spec()],
        out_shape=[jax.ShapeDtypeStruct((st.rows, D_MODEL), BF16),
                   jax.ShapeDtypeStruct((BATCH, D_MODEL, SEQ), F32),
                   jax.ShapeDtypeStruct((BATCH, D_MODEL, SEQ), BF16),
                   jax.ShapeDtypeStruct((st.rows, D_MODEL), F32),
                   jax.ShapeDtypeStruct((st.rows, D_MODEL), BF16)],
        compiler_params=_params("arbitrary"),
        name="qkv_rope_prompt",
    )(x, mods, mods, ng, wq, wkt, wv, *tables, *angles_t)


def _lambda(lq1_ref, lk1_ref, lq2_ref, lk2_ref, lambda_init):
    a = jnp.sum(lq1_ref[...] * lk1_ref[...], axis=-1, keepdims=True)
    b = jnp.sum(lq2_ref[...] * lk2_ref[...], axis=-1, keepdims=True)
    return jnp.exp(a) - jnp.exp(b) + lambda_init


def _attn_kernel(q_ref, kt_ref, v_ref, lq1_ref, lk1_ref, lq2_ref, lk2_ref, sg_ref, o_ref, *, lambda_init):
    tq = q_ref.shape[0]
    qi = pl.program_id(2)
    q = q_ref[...]
    first = lax.broadcasted_iota(jnp.int32, (1, LANES), 1) < HEAD_DIM
    zero = jnp.zeros_like(q)
    qs = (jnp.where(first, q, zero), jnp.where(first, zero, q))
    lam = _lambda(lq1_ref, lk1_ref, lq2_ref, lk2_ref, lambda_init)

    def query_tile(n):
        tv = (n + 1) * tq
        row = n * tq + lax.broadcasted_iota(jnp.int32, (tq, tv), 0)
        visible = lax.broadcasted_iota(jnp.int32, (tq, tv), 1) <= row
        probs = []
        for comp in range(2):
            s = jnp.where(visible, _bdot(qs[comp], kt_ref[:, :tv]), -jnp.inf)
            e = jnp.exp(s - jnp.max(s, axis=-1, keepdims=True))
            probs.append(e * (1.0 / jnp.sum(e, axis=-1, keepdims=True)))
        a = (probs[0] - lam * probs[1]).astype(BF16)
        o = _bdot(a, v_ref[:tv, :])
        o_ref[...] = (_rms(o) * sg_ref[...] * (1.0 - lambda_init)).astype(o_ref.dtype)

    for n in range(SEQ // tq):
        pl.when(qi == n)(functools.partial(query_tile, n))


def _attn_prompt(q, ktb, vb, lq1, lk1, lq2, lk2, subln_g, lambda_init):
    nq = SEQ // ATTN_Q_TILE
    kern = functools.partial(_attn_kernel, lambda_init=lambda_init)
    vec = pl.BlockSpec((1, HEAD_DIM), lambda b, h, i: (0, 0))
    return pl.pallas_call(
        kern,
        grid=(BATCH, N_HEADS, nq),
        in_specs=[pl.BlockSpec((ATTN_Q_TILE, LANES), lambda b, h, i: (b * nq + i, h)),
                  pl.BlockSpec((None, LANES, SEQ), lambda b, h, i: (b, h, 0)),
                  pl.BlockSpec((SEQ, LANES), lambda b, h, i: (b, h)),
                  vec, vec, vec, vec,
                  pl.BlockSpec((1, 2 * HEAD_DIM), lambda b, h, i: (0, 0))],
        out_specs=pl.BlockSpec((ATTN_Q_TILE, LANES), lambda b, h, i: (b * nq + i, h)),
        out_shape=jax.ShapeDtypeStruct((N_PROMPT, D_MODEL), BF16),
        compiler_params=_params("arbitrary", "arbitrary", "arbitrary"),
        name="diff_attn_prompt",
    )(q, ktb, vb, lq1, lk1, lq2, lk2, subln_g)


DEC_ROWS = N_HEADS * 2 * DEC_SEQ


def _decode_kernel(pt_ref, q_ref, kn_ref, vn_ref, lq1_ref, lk1_ref, lq2_ref, lk2_ref, sg_ref, *rest,
                   lambda_init):
    k_refs = rest[:PAGES_PER_STEP]
    v_refs = rest[PAGES_PER_STEP:2 * PAGES_PER_STEP]
    o_ref, m_ref, l_ref, acc_ref = rest[2 * PAGES_PER_STEP:]
    step = pl.program_id(1)
    rows_per_head = 2 * DEC_SEQ
    lane = lax.broadcasted_iota(jnp.int32, (rows_per_head, D_MODEL), 1)
    comp = _div_pow2(lax.broadcasted_iota(jnp.int32, (rows_per_head, D_MODEL), 0), DEC_SEQ)
    q8 = q_ref[...] * (HEAD_DIM ** -0.5)
    blk = _div_pow2(lane, HEAD_DIM)
    qbd = jnp.concatenate([jnp.where(blk == 2 * hd + comp, q8, 0.0) for hd in range(N_HEADS)],
                          axis=0).astype(BF16)

    @pl.when(step == 0)
    def _():
        m_ref[...] = jnp.full(m_ref.shape, -jnp.inf, F32)
        l_ref[...] = jnp.zeros(l_ref.shape, F32)
        acc_ref[...] = jnp.zeros(acc_ref.shape, F32)

    m, l, acc = m_ref[...], l_ref[...], acc_ref[...]
    s = jnp.concatenate([_bdot(qbd, kr[...].astype(BF16)) for kr in k_refs], axis=1)
    m_new = jnp.maximum(m, jnp.max(s, axis=-1, keepdims=True))
    alpha = jnp.exp(m - m_new)
    p = jnp.exp(s - m_new)
    l = l * alpha + jnp.sum(p, axis=-1, keepdims=True)
    pv = []
    for hd in range(N_HEADS):
        rows = slice(hd * rows_per_head, (hd + 1) * rows_per_head)
        v_h = jnp.concatenate([vr[pl.ds(hd, PAGE_SIZE, stride=N_HEADS), :].astype(BF16) for vr in v_refs],
                              axis=0)
        pv.append(_bdot(p[rows].astype(BF16), v_h))
    acc = acc * alpha + jnp.concatenate(pv, axis=0)
    m = m_new
    m_ref[...], l_ref[...], acc_ref[...] = m, l, acc

    @pl.when(step == pl.num_programs(1) - 1)
    def _():
        kn = kn_ref[...].astype(BF16).astype(F32)
        vn = vn_ref[...].astype(BF16).astype(F32)
        qf = qbd.astype(F32)
        row_q = _mod_pow2(lax.broadcasted_iota(jnp.int32, (DEC_ROWS, 1), 0), DEC_SEQ)
        s_new = []
        for j in range(DEC_SEQ):
            sj = jnp.sum(qf * kn[j:j + 1, :], axis=-1, keepdims=True)
            s_new.append(jnp.where(row_q >= j, sj, -jnp.inf))
        m2 = m
        for sj in s_new:
            m2 = jnp.maximum(m2, sj)
        alpha = jnp.exp(m - m2)
        l2 = l * alpha
        acc2 = acc * alpha
        for j, sj in enumerate(s_new):
            pj = jnp.exp(sj - m2)
            l2 = l2 + pj
            v_rows = jnp.concatenate(
                [jnp.broadcast_to(vn[j:j + 1, hd * LANES:(hd + 1) * LANES], (rows_per_head, LANES))
                 for hd in range(N_HEADS)], axis=0)
            acc2 = acc2 + pj.astype(BF16).astype(F32) * v_rows
        outn = acc2 / l2
        lam = _lambda(lq1_ref, lk1_ref, lq2_ref, lk2_ref, lambda_init)
        for hd in range(N_HEADS):
            r0 = hd * rows_per_head
            o = outn[r0:r0 + DEC_SEQ] - lam * outn[r0 + DEC_SEQ:r0 + rows_per_head]
            o_ref[:, hd * LANES:(hd + 1) * LANES] = _rms(o) * sg_ref[...] * (1.0 - lambda_init)


def _attn_decode(page_table, q8, k_new, v_new, cache_kt, cache_v, layer, lq1, lk1, lq2, lk2, subln_g, lambda_init):
    n_steps = N_PAGES // PAGES_PER_STEP
    kern = functools.partial(_decode_kernel, lambda_init=lambda_init)
    tok = pl.BlockSpec((None, DEC_SEQ, D_MODEL), lambda b, p, pt: (b, 0, 0))
    vec = pl.BlockSpec((1, HEAD_DIM), lambda b, p, pt: (0, 0))

    def page_spec(i):
        return pl.BlockSpec((None, None, N_HEADS * PAGE_SIZE, LANES),
                            lambda b, p, pt: (pt[b * N_PAGES + p * PAGES_PER_STEP + i], layer, 0, 0))

    grid_spec = pltpu.PrefetchScalarGridSpec(
        num_scalar_prefetch=1,
        grid=(DEC_BATCH, n_steps),
        in_specs=[pl.BlockSpec((None, 2 * DEC_SEQ, D_MODEL), lambda b, p, pt: (b, 0, 0)), tok, tok,
                  vec, vec, vec, vec, pl.BlockSpec((1, 2 * HEAD_DIM), lambda b, p, pt: (0, 0))]
                 + [page_spec(i) for i in range(PAGES_PER_STEP)] * 2,
        out_specs=tok,
        scratch_shapes=[pltpu.VMEM((DEC_ROWS, 1), F32), pltpu.VMEM((DEC_ROWS, 1), F32),
                        pltpu.VMEM((DEC_ROWS, LANES), F32)],
    )
    return pl.pallas_call(
        kern,
        grid_spec=grid_spec,
        out_shape=jax.ShapeDtypeStruct((DEC_BATCH, DEC_SEQ, D_MODEL), F32),
        compiler_params=_params("arbitrary", "arbitrary"),
        name="diff_attn_decode",
    )(page_table.reshape(-1), q8, k_new, v_new, lq1, lk1, lq2, lk2, subln_g,
      *([cache_kt] * PAGES_PER_STEP), *([cache_v] * PAGES_PER_STEP))


def _outproj_kernel(o_ref, x_ref, gt_ref, w_ref, xo_ref):
    xo_ref[...] = x_ref[...] + gt_ref[...] * _bdot(o_ref[...].astype(BF16), w_ref[...])


def _outproj_layer(st, o, x, mods, w_o):
    return pl.pallas_call(
        _outproj_kernel,
        grid=(st.grid,),
        in_specs=[st.rows_spec(), st.rows_spec(), st.mod_spec(2), _full_spec((D_MODEL, D_MODEL))],
        out_specs=st.rows_spec(),
        out_shape=jax.ShapeDtypeStruct((st.rows, D_MODEL), F32),
        compiler_params=_params("arbitrary"),
        name="attn_out_proj",
    )(o, x, mods, w_o)


N_ALL = N_PROMPT + N_SAMPLE
ROW_W = D_MODEL + LANES
PAIRS_PER_GROUP = EXPERTS_PER_GROUP * (EXPERTS_PER_GROUP - 1) // 2
N_BUCKETS = N_EXPERT_GROUPS * PAIRS_PER_GROUP
SORT_TILE = 256
N_SORT_TILES = (N_ALL + N_BUCKETS * (SORT_TILE - 1)) // SORT_TILE
N_SORT_ROWS = N_SORT_TILES * SORT_TILE
DMA_UNROLL = 8


def _router_kernel(x_ref, sh_ref, sc_ref, ng_ref, rw_ref, rb_ref, route_ref):
    h = _modulate(x_ref[...], ng_ref[...], sh_ref[...], sc_ref[...])
    logits = jnp.dot(h, rw_ref[...], preferred_element_type=F32, precision=lax.Precision.HIGHEST)
    scores = jax.nn.sigmoid(logits)
    sel = scores + rb_ref[...]
    eid = lax.broadcasted_iota(jnp.int32, sel.shape, 1)
    grp = _div_pow2(eid, EXPERTS_PER_GROUP)
    neg = -jnp.inf

    def first_max(vals):
        m = jnp.max(vals, axis=-1, keepdims=True)
        idx = jnp.min(jnp.where(vals == m, eid, N_EXPERTS), axis=-1, keepdims=True)
        return m, idx

    best = None
    for g in range(N_EXPERT_GROUPS):
        vg = jnp.where(grp == g, sel, neg)
        m1, i1 = first_max(vg)
        m2, _ = first_max(jnp.where(eid == i1, neg, vg))
        gs = m1 + m2
        if best is None:
            best, best_score = jnp.zeros_like(i1), gs
        else:
            better = gs > best_score
            best = jnp.where(better, g, best)
            best_score = jnp.where(better, gs, best_score)
    masked = jnp.where(grp == best, sel, neg)
    _, ia = first_max(masked)
    _, ib = first_max(jnp.where(eid == ia, neg, masked))
    ga = jnp.sum(jnp.where(eid == ia, scores, 0.0), axis=-1, keepdims=True)
    gb = jnp.sum(jnp.where(eid == ib, scores, 0.0), axis=-1, keepdims=True)
    den = ga + gb
    a_first = ia < ib
    lo = jnp.where(a_first, ia, ib).astype(F32)
    hi = jnp.where(a_first, ib, ia).astype(F32)
    g_lo = jnp.where(a_first, ga, gb) / den
    g_hi = jnp.where(a_first, gb, ga) / den
    col = lax.broadcasted_iota(jnp.int32, route_ref.shape, 1)
    route_ref[...] = jnp.where(col == 0, lo, jnp.where(col == 1, hi, jnp.where(col == 2, g_lo, g_hi)))


def _router_layer(st, x, mods, ng, router_w, router_b):
    return pl.pallas_call(
        _router_kernel,
        grid=(st.grid,),
        in_specs=[st.rows_spec(), st.mod_spec(3), st.mod_spec(4), _full_spec((1, D_MODEL)),
                  _full_spec((D_MODEL, N_EXPERTS)), _full_spec((1, N_EXPERTS))],
        out_specs=st.rows_spec(4),
        out_shape=jax.ShapeDtypeStruct((st.rows, 4), F32),
        compiler_params=_params("arbitrary"),
        name="moe_router",
    )(x, mods, mods, ng, router_w, router_b)


def _moe_plan(route):
    lo = route[:, 0].astype(jnp.int32)
    hi = route[:, 1].astype(jnp.int32)
    i, j = lo % EXPERTS_PER_GROUP, hi % EXPERTS_PER_GROUP
    pair = i * (2 * EXPERTS_PER_GROUP - 1 - i) // 2 + (j - i - 1)
    bucket = (lo // EXPERTS_PER_GROUP) * PAIRS_PER_GROUP + pair
    onehot = (bucket[:, None] == jnp.arange(N_BUCKETS)[None, :]).astype(jnp.int32)
    csum = jnp.cumsum(onehot, axis=0)
    counts = csum[-1]
    rank = jnp.sum(onehot * csum, axis=1) - 1
    tiles = (counts + SORT_TILE - 1) // SORT_TILE
    ends = jnp.cumsum(tiles)
    starts = ends - tiles
    n_used = ends[-1]
    pos = jnp.sum(onehot * starts[None, :], axis=1) * SORT_TILE + rank
    tile_id = jnp.arange(N_SORT_TILES)
    tile_bucket = jnp.sum((jnp.minimum(tile_id, n_used - 1)[:, None] >= ends[None, :]).astype(jnp.int32), axis=1)
    tile_bucket = jnp.minimum(tile_bucket, N_BUCKETS - 1)
    pair_lo = jnp.array([a for a in range(EXPERTS_PER_GROUP) for _ in range(a + 1, EXPERTS_PER_GROUP)], jnp.int32)
    pair_hi = jnp.array([b for a in range(EXPERTS_PER_GROUP) for b in range(a + 1, EXPERTS_PER_GROUP)], jnp.int32)
    base = (tile_bucket // PAIRS_PER_GROUP) * EXPERTS_PER_GROUP
    tile_lo = base + pair_lo[tile_bucket % PAIRS_PER_GROUP]
    tile_hi = base + pair_hi[tile_bucket % PAIRS_PER_GROUP]
    return pos.astype(jnp.int32), tile_lo, tile_hi, n_used.reshape(1).astype(jnp.int32)


def _bucket_kernel(pos_ref, x_ref, sh_ref, sc_ref, ng_ref, route_ref, dst_in_ref, dst_ref, buf, sem, *, row0):
    del dst_in_ref
    tm = x_ref.shape[0]
    t = pl.program_id(0)
    nt = pl.num_programs(0)
    slot = t % 2

    def wait(s):
        pltpu.make_async_copy(buf.at[s], dst_ref.at[pl.ds(0, tm), :], sem.at[s]).wait()

    @pl.when(t >= 2)
    def _():
        wait(slot)

    route = route_ref[...]
    lane = lax.broadcasted_iota(jnp.int32, (tm, LANES), 1)
    buf[slot, :, :D_MODEL] = _modulate(x_ref[...], ng_ref[...], sh_ref[...], sc_ref[...])
    buf[slot, :, D_MODEL:] = jnp.where(lane == 0, route[:, 2:3], jnp.where(lane == 1, route[:, 3:4], 0.0))
    base = row0 + t * tm

    def body(i, carry):
        for u in range(DMA_UNROLL):
            r = i * DMA_UNROLL + u
            pltpu.make_async_copy(buf.at[slot, pl.ds(r, 1), :],
                                  dst_ref.at[pl.ds(pos_ref[base + r], 1), :], sem.at[slot]).start()
        return carry

    lax.fori_loop(0, tm // DMA_UNROLL, body, 0)

    @pl.when(t == nt - 1)
    def _():
        wait(slot)

    @pl.when((t == nt - 1) & (nt > 1))
    def _():
        wait(1 - slot)


def _bucket_rows(st, pos, x, mods, ng, route, dst, row0):
    kern = functools.partial(_bucket_kernel, row0=row0)
    if st.per_row_mods:
        def mod_spec(k):
            return pl.BlockSpec((st.tile, D_MODEL), lambda i, p: (i, k))
    else:
        tpb = st.tiles_per_batch

        def mod_spec(k):
            return pl.BlockSpec((None, None, 1, D_MODEL), lambda i, p: (i // tpb, k, 0, 0))
    hbm = pl.BlockSpec(memory_space=pl.ANY)
    grid_spec = pltpu.PrefetchScalarGridSpec(
        num_scalar_prefetch=1,
        grid=(st.grid,),
        in_specs=[pl.BlockSpec((st.tile, D_MODEL), lambda i, p: (i, 0)), mod_spec(3), mod_spec(4),
                  pl.BlockSpec((1, D_MODEL), lambda i, p: (0, 0)),
                  pl.BlockSpec((st.tile, 4), lambda i, p: (i, 0)), hbm],
        out_specs=hbm,
        scratch_shapes=[pltpu.VMEM((2, st.tile, ROW_W), F32), pltpu.SemaphoreType.DMA((2,))],
    )
    return pl.pallas_call(
        kern,
        grid_spec=grid_spec,
        out_shape=jax.ShapeDtypeStruct(dst.shape, dst.dtype),
        input_output_aliases={6: 0},
        compiler_params=_params("arbitrary"),
        name="moe_bucket_rows",
    )(pos, x, mods, mods, ng, route, dst)


def _moe_kernel(lo_ref, hi_ref, nu_ref, xs_ref, wg_lo, wu_lo, wd_lo, wg_hi, wu_hi, wd_hi, ys_ref):
    del lo_ref, hi_ref
    t = pl.program_id(0)

    @pl.when(t < nu_ref[0])
    def _():
        row = xs_ref[...]
        x = row[:, :D_MODEL].astype(BF16)

        def expert(wg, wu, wd, gate):
            g = _bdot(x, wg[...])
            u = _bdot(x, wu[...])
            return _bdot((jax.nn.silu(g) * u * gate).astype(BF16), wd[...])

        ys_ref[...] = (expert(wg_lo, wu_lo, wd_lo, row[:, D_MODEL:D_MODEL + 1])
                       + expert(wg_hi, wu_hi, wd_hi, row[:, D_MODEL + 1:D_MODEL + 2]))

    @pl.when(t >= nu_ref[0])
    def _():
        ys_ref[...] = jnp.zeros(ys_ref.shape, ys_ref.dtype)


def _moe_experts(xs, tile_lo, tile_hi, n_used, wg, wu, wd, layer):
    def w_spec(shape, which):
        return pl.BlockSpec((None, None) + shape,
                            lambda t, lo, hi, nu: (layer, (lo, hi)[which][t], 0, 0))

    up, down = (D_MODEL, D_EXPERT), (D_EXPERT, D_MODEL)
    grid_spec = pltpu.PrefetchScalarGridSpec(
        num_scalar_prefetch=3,
        grid=(N_SORT_TILES,),
        in_specs=[pl.BlockSpec((SORT_TILE, ROW_W), lambda t, lo, hi, nu: (t, 0)),
                  w_spec(up, 0), w_spec(up, 0), w_spec(down, 0),
                  w_spec(up, 1), w_spec(up, 1), w_spec(down, 1)],
        out_specs=pl.BlockSpec((SORT_TILE, D_MODEL), lambda t, lo, hi, nu: (t, 0)),
    )
    return pl.pallas_call(
        _moe_kernel,
        grid_spec=grid_spec,
        out_shape=jax.ShapeDtypeStruct((N_SORT_ROWS, D_MODEL), F32),
        compiler_params=_params("arbitrary"),
        name="moe_experts",
    )(tile_lo, tile_hi, n_used, xs, wg, wu, wd, wg, wu, wd)


def _unsort_kernel(pos_ref, ys_ref, x_ref, g2_ref, o_ref, buf, sem, *, row0):
    tm = x_ref.shape[0]
    t = pl.program_id(0)

    def fetch(tile, slot):
        base = row0 + tile * tm

        def body(i, carry):
            for u in range(DMA_UNROLL):
                r = i * DMA_UNROLL + u
                pltpu.make_async_copy(ys_ref.at[pl.ds(pos_ref[base + r], 1), :],
                                      buf.at[slot, pl.ds(r, 1), :], sem.at[slot]).start()
            return carry

        lax.fori_loop(0, tm // DMA_UNROLL, body, 0)

    @pl.when(t == 0)
    def _():
        fetch(0, 0)

    @pl.when(t + 1 < pl.num_programs(0))
    def _():
        fetch(t + 1, (t + 1) % 2)

    slot = t % 2
    pltpu.make_async_copy(ys_ref.at[pl.ds(0, tm), :], buf.at[slot], sem.at[slot]).wait()
    o_ref[...] = x_ref[...] + g2_ref[...] * buf[slot]


def _unsort_layer(st, pos, ys, x, mods, row0):
    kern = functools.partial(_unsort_kernel, row0=row0)
    if st.per_row_mods:
        g2_spec = pl.BlockSpec((st.tile, D_MODEL), lambda i, p: (i, 5))
    else:
        tpb = st.tiles_per_batch
        g2_spec = pl.BlockSpec((None, None, 1, D_MODEL), lambda i, p: (i // tpb, 5, 0, 0))
    rows = pl.BlockSpec((st.tile, D_MODEL), lambda i, p: (i, 0))
    grid_spec = pltpu.PrefetchScalarGridSpec(
        num_scalar_prefetch=1,
        grid=(st.grid,),
        in_specs=[pl.BlockSpec(memory_space=pl.ANY), rows, g2_spec],
        out_specs=rows,
        scratch_shapes=[pltpu.VMEM((2, st.tile, D_MODEL), F32), pltpu.SemaphoreType.DMA((2,))],
    )
    return pl.pallas_call(
        kern,
        grid_spec=grid_spec,
        out_shape=jax.ShapeDtypeStruct((st.rows, D_MODEL), F32),
        compiler_params=_params("arbitrary"),
        name="moe_unsort_residual",
    )(pos, ys, x, mods)


def _final_kernel(x_ref, g_ref, o_ref):
    o_ref[...] = _rms(x_ref[...]) * g_ref[...]


def _final_norm(st, x, g):
    return pl.pallas_call(
        _final_kernel,
        grid=(st.grid,),
        in_specs=[st.rows_spec(), _full_spec((1, D_MODEL))],
        out_specs=st.rows_spec(),
        out_shape=jax.ShapeDtypeStruct((st.rows, D_MODEL), F32),
        compiler_params=_params("arbitrary"),
        name="final_norm",
    )(x, g)


def kernel(x_prompt, x_sample, cache_k, cache_v, state_pool, page_table, c_prompt, c_sample,
           norm1_g, norm2_g, ada_w, ada_b, a_w_in, a_ln_g, a_ln_b, a_w_s, a_b_s, a_w_out,
           b_w_grp, b_scale, c_w_qkv, c_lq1, c_lk1, c_lq2, c_lk2, c_subln_g, c_w_o,
           router_w, router_bias, e_w_gate, e_w_up, e_w_down, final_g):
    sp = _Stream(N_PROMPT, PROMPT_TILE, SEQ // PROMPT_TILE, per_row_mods=False)
    ss = _Stream(N_SAMPLE, N_SAMPLE, 1, per_row_mods=True)

    xp = x_prompt.reshape(N_PROMPT, D_MODEL)
    xs = x_sample.reshape(N_SAMPLE, D_MODEL)
    mods_all = _ada(jnp.concatenate([c_prompt, c_sample], axis=0), ada_w, ada_b)

    wg_b, wu_b, wd_b = e_w_gate.astype(BF16), e_w_up.astype(BF16), e_w_down.astype(BF16)
    rb = router_bias.reshape(1, N_EXPERTS)
    n_phys = cache_k.shape[0]
    n_c_layers = cache_k.shape[1]
    ckt = jnp.transpose(cache_k, (0, 1, 3, 4, 5, 2)).reshape(n_phys, n_c_layers, D_MODEL, PAGE_SIZE)
    cv = cache_v.reshape(n_phys, n_c_layers, PAGE_SIZE * N_HEADS, 2 * HEAD_DIM)

    av_p, av_s, pr_p, pr_s, kp_l, vp_l, ks_l, vs_l = [], [], [], [], [], [], [], []
    for i in range(DEPTH):
        kind, j = i % N_MIXERS, i // N_MIXERS
        mp = mods_all[i, :BATCH].reshape(BATCH, N_MODS, 1, D_MODEL)
        ms = jnp.repeat(mods_all[i, BATCH:], DEC_SEQ, axis=0)
        ng1 = norm1_g[i].reshape(1, D_MODEL)
        ng2 = norm2_g[i].reshape(1, D_MODEL)
        if kind == 0:
            w_in = a_w_in[j].astype(BF16)
            w_out = a_w_out[j].astype(BF16)
            ln_g, ln_b = a_ln_g[j].reshape(1, D_GATE), a_ln_b[j].reshape(1, D_GATE)
            bias = jnp.repeat(a_b_s[j].T, D_GATE // A_GROUPS, axis=1)
            xp, vrow_p = _gmlp_layer(sp, xp, mp, ng1, w_in, ln_g, ln_b, a_w_s[j], bias, w_out, CHUNK, BATCH, False)
            reps = CHUNK // DEC_SEQ
            wmix_s = jnp.tile(a_w_s[j][:, :DEC_SEQ, :DEC_SEQ], (1, reps, reps))
            bias_s = jnp.tile(bias[:DEC_SEQ], (reps, 1))
            xs, vrow_s = _gmlp_layer(ss, xs, ms, ng1, a_w_in[j], ln_g, ln_b, wmix_s, bias_s, a_w_out[j],
                                     DEC_SEQ, 1, True)
            av_p.append(vrow_p.reshape(BATCH, CHUNK, D_GATE))
            av_s.append(vrow_s.reshape(DEC_BATCH, DEC_SEQ, D_GATE))
        elif kind == 1:
            w_grp = b_w_grp[j].astype(BF16)
            scale = b_scale[j].reshape(1, D_MODEL)
            hb = PROMPT_TILE // POOL_HALO
            halo_spec = pl.BlockSpec((POOL_HALO, D_MODEL), lambda t: (jnp.maximum(t * hb - 1, 0), 0))
            xp, rows_p = _pool_layer(sp, xp, xp, halo_spec, mp, ng1, w_grp, scale, False, 0, POOL_HALO, BATCH)
            pr_p.append(rows_p.reshape(BATCH, POOL_HALO, D_MODEL)[:, 1:])
            pad_t = 2 * DEC_SEQ
            s8 = _Stream(DEC_BATCH * pad_t, pad_t, 1, per_row_mods=True)
            xs8 = jnp.pad(xs.reshape(DEC_BATCH, DEC_SEQ, D_MODEL), ((0, 0), (0, DEC_SEQ), (0, 0)))
            ms8 = jnp.repeat(mods_all[i, BATCH:], pad_t, axis=0)
            prev = jnp.pad(state_pool[:, j], ((0, 0), (1, 0), (0, 0))).reshape(DEC_BATCH * POOL_HALO, D_MODEL)
            prev_spec = pl.BlockSpec((POOL_HALO, D_MODEL), lambda t: (t, 0))
            n_out = POOL_HALO + pad_t
            xs8, rows_s = _pool_layer(s8, xs8.reshape(-1, D_MODEL), prev, prev_spec, ms8, ng1, w_grp, scale,
                                      True, PAST_LEN, n_out, DEC_BATCH)
            xs = xs8.reshape(DEC_BATCH, pad_t, D_MODEL)[:, :DEC_SEQ].reshape(N_SAMPLE, D_MODEL)
            lo = 1 + DEC_SEQ
            pr_s.append(rows_s.reshape(DEC_BATCH, n_out, D_MODEL)[:, lo:lo + POOL_BUF])
        else:
            lambda_init = 0.8 - 0.6 * math.exp(-0.3 * i)
            w_qkv = c_w_qkv[j].astype(BF16)
            w_o = c_w_o[j].astype(BF16)
            lq1, lk1 = c_lq1[j].reshape(1, HEAD_DIM), c_lk1[j].reshape(1, HEAD_DIM)
            lq2, lk2 = c_lq2[j].reshape(1, HEAD_DIM), c_lk2[j].reshape(1, HEAD_DIM)
            sg = c_subln_g[j].reshape(1, 2 * HEAD_DIM)
            pos_p = jnp.arange(SEQ)
            cos_p, sin_p = _rope_angles(pos_p)
            qp, kt, ktb, vp, vpb = _qkv_t_layer(
                sp, xp, mp, ng1, w_qkv[:, :D_MODEL], w_qkv[:, D_MODEL:2 * D_MODEL].T, w_qkv[:, 2 * D_MODEL:],
                _rope_tables(pos_p), (cos_p.T, sin_p.T))
            op = _attn_prompt(qp, ktb, vpb, lq1, lk1, lq2, lk2, sg, lambda_init)
            xp = _outproj_layer(sp, op, xp, mp, w_o)
            tab_s = _rope_tables(jnp.tile(PAST_LEN + jnp.arange(DEC_SEQ), DEC_BATCH))
            qs, ks, vs = _qkv_layer(ss, xs, ms, ng1, w_qkv, tab_s,
                                    pl.BlockSpec((N_SAMPLE, LANES), lambda t: (0, 0)))
            tok = (DEC_BATCH, DEC_SEQ, D_MODEL)
            q8 = jnp.concatenate([qs.reshape(tok)] * 2, axis=1)
            os_ = _attn_decode(page_table, q8, ks.reshape(tok), vs.reshape(tok), ckt, cv, j,
                               lq1, lk1, lq2, lk2, sg, lambda_init)
            xs = _outproj_layer(ss, os_.reshape(N_SAMPLE, D_MODEL), xs, ms, w_o)
            kp = jnp.transpose(kt.reshape(BATCH, N_HEADS, 2, HEAD_DIM, SEQ), (0, 4, 1, 2, 3))
            kp_l.append(kp)
            vp_l.append(vp.reshape(BATCH, SEQ, N_HEADS, 2 * HEAD_DIM))
            ks_l.append(ks.reshape(DEC_BATCH, DEC_SEQ, N_HEADS, 2, HEAD_DIM))
            vs_l.append(vs.reshape(DEC_BATCH, DEC_SEQ, N_HEADS, 2 * HEAD_DIM))

        route_p = _router_layer(sp, xp, mp, ng2, router_w, rb)
        route_s = _router_layer(ss, xs, ms, ng2, router_w, rb)
        pos, tile_lo, tile_hi, n_used = _moe_plan(jnp.concatenate([route_p, route_s], axis=0))
        bucketed = jnp.zeros((N_SORT_ROWS, ROW_W), F32)
        bucketed = _bucket_rows(sp, pos, xp, mp, ng2, route_p, bucketed, 0)
        bucketed = _bucket_rows(ss, pos, xs, ms, ng2, route_s, bucketed, N_PROMPT)
        ys = _moe_experts(bucketed, tile_lo, tile_hi, n_used, wg_b, wu_b, wd_b, i)
        xp = _unsort_layer(sp, pos, ys, xp, mp, 0)
        xs = _unsort_layer(ss, pos, ys, xs, ms, N_PROMPT)

    fg = final_g.reshape(1, D_MODEL)
    y_prompt = _final_norm(sp, xp, fg).reshape(BATCH, SEQ, D_MODEL)
    y_sample = _final_norm(ss, xs, fg).reshape(DEC_BATCH, DEC_SEQ, D_MODEL)
    return (y_prompt, y_sample,
            jnp.stack(av_p, axis=1), jnp.stack(av_s, axis=1),
            jnp.stack(pr_p, axis=1), jnp.stack(pr_s, axis=1),
            jnp.stack(kp_l, axis=1), jnp.stack(vp_l, axis=1),
            jnp.stack(ks_l, axis=1), jnp.stack(vs_l, axis=1))
```

```python
import functools
import math

import jax
import jax.numpy as jnp
from jax import lax
from jax.experimental import pallas as pl
from jax.experimental.pallas import tpu as pltpu

F32 = jnp.float32
BF16 = jnp.bfloat16

D_MODEL = 1024
BATCH = 8
SEQ = 2048
DEPTH = 4
DEC_BATCH = 32
DEC_SEQ = 4
PAST_LEN = 8192
PAGE_SIZE = 128
N_PAGES = PAST_LEN // PAGE_SIZE
N_MIXERS = 3
CHUNK = 128
D_GATE = D_MODEL
A_GROUPS = 4
POOL_WINDOWS = (2, 4, 8, 16)
B_GROUPS = len(POOL_WINDOWS)
POOL_BUF = max(POOL_WINDOWS) - 1
POOL_HALO = POOL_BUF + 1
N_HEADS = 8
HEAD_DIM = D_MODEL // N_HEADS // 2
ROT_DIM = HEAD_DIM // 4
ROPE_THETA = 500000.0
N_EXPERTS = 16
N_EXPERT_GROUPS = 4
EXPERTS_PER_GROUP = N_EXPERTS // N_EXPERT_GROUPS
D_EXPERT = 512
EPS = 1e-6
N_MODS = 6

N_PROMPT = BATCH * SEQ
N_SAMPLE = DEC_BATCH * DEC_SEQ
LANES = 128
VMEM_LIMIT = 56 * 1024 * 1024

PROMPT_TILE = 256
ATTN_Q_TILE = 256
PAGES_PER_STEP = 8


def _params(*sem):
    return pltpu.CompilerParams(dimension_semantics=sem, vmem_limit_bytes=VMEM_LIMIT)


class _Stream:
    def __init__(self, rows, tile, tiles_per_batch, per_row_mods):
        self.rows = rows
        self.tile = tile
        self.tiles_per_batch = tiles_per_batch
        self.per_row_mods = per_row_mods
        self.grid = rows // tile

    def rows_spec(self, width=D_MODEL):
        return pl.BlockSpec((self.tile, width), lambda i: (i, 0))

    def mod_spec(self, k):
        if self.per_row_mods:
            return pl.BlockSpec((self.tile, D_MODEL), lambda i: (i, k))
        tpb = self.tiles_per_batch
        return pl.BlockSpec((None, None, 1, D_MODEL), lambda i: (i // tpb, k, 0, 0))


def _full_spec(shape):
    nd = len(shape)
    return pl.BlockSpec(shape, lambda i: (0,) * nd)


def _rms(x):
    return x * lax.rsqrt(jnp.mean(x * x, axis=-1, keepdims=True) + EPS)


def _modulate(x, g, sh, sc):
    return _rms(x) * g * (1.0 + sc) + sh


def _bdot(a, b):
    return jnp.dot(a, b, preferred_element_type=F32)


def _div_pow2(x, n):
    assert n & (n - 1) == 0
    return lax.shift_right_logical(x, n.bit_length() - 1)


def _mod_pow2(x, n):
    assert n & (n - 1) == 0
    return x & (n - 1)


def _ada_kernel(c_ref, w_ref, b_ref, o_ref):
    o_ref[...] = jnp.dot(jax.nn.silu(c_ref[...]), w_ref[...], preferred_element_type=F32,
                         precision=lax.Precision.HIGHEST) + b_ref[...]


def _ada(c_all, ada_w, ada_b):
    nb = c_all.shape[0]
    tn = 1536
    return pl.pallas_call(
        _ada_kernel,
        grid=(DEPTH, N_MODS * D_MODEL // tn),
        in_specs=[pl.BlockSpec((nb, D_MODEL), lambda l, n: (0, 0)),
                  pl.BlockSpec((None, D_MODEL, tn), lambda l, n: (l, 0, n)),
                  pl.BlockSpec((None, 1, tn), lambda l, n: (l, 0, n))],
        out_specs=pl.BlockSpec((None, nb, tn), lambda l, n: (l, 0, n)),
        out_shape=jax.ShapeDtypeStruct((DEPTH, nb, N_MODS * D_MODEL), F32),
        compiler_params=_params("arbitrary", "arbitrary"),
        name="ada_mod",
    )(c_all, ada_w, ada_b.reshape(DEPTH, 1, N_MODS * D_MODEL))


def _mm(a, b, precise):
    if precise:
        return jnp.dot(a.astype(F32), b.astype(F32), preferred_element_type=F32, precision=lax.Precision.HIGHEST)
    return _bdot(a.astype(BF16), b.astype(BF16))


def _gmlp_kernel(x_ref, sh_ref, sc_ref, gt_ref, ng_ref, win_ref, lng_ref, lnb_ref, wmix_ref,
                 bmix_ref, wout_ref, xo_ref, v_ref, mix_ref, *, period, precise):
    tm = x_ref.shape[0]
    x = x_ref[...]
    h = _modulate(x, ng_ref[...], sh_ref[...], sc_ref[...])
    z = _mm(h, win_ref[...], precise)
    z = 0.5 * z * (1.0 + lax.erf(z * (2.0 ** -0.5)))
    u = z[:, :D_GATE]
    v = z[:, D_GATE:]
    mu = jnp.mean(v, axis=-1, keepdims=True)
    vc = v - mu
    var = jnp.mean(vc * vc, axis=-1, keepdims=True)
    v = vc * lax.rsqrt(var + EPS) * lng_ref[...] + lnb_ref[...]
    v_ref[...] = v[tm - CHUNK:, :]
    vb = v if precise else v.astype(BF16)
    t = lax.broadcasted_iota(jnp.int32, (CHUNK, CHUNK), 0)
    s = lax.broadcasted_iota(jnp.int32, (CHUNK, CHUNK), 1)
    keep = (s <= t) & (_div_pow2(t, period) == _div_pow2(s, period))
    cg = D_GATE // A_GROUPS
    for g in range(A_GROUPS):
        wm = jnp.where(keep, wmix_ref[g], 0.0)
        for c in range(tm // CHUNK):
            rows = slice(c * CHUNK, (c + 1) * CHUNK)
            cols = slice(g * cg, (g + 1) * cg)
            mix_ref[rows, cols] = _mm(wm, vb[rows, cols], precise) + bmix_ref[:, cols]
    y = _mm(u * mix_ref[...], wout_ref[...], precise)
    xo_ref[...] = x + gt_ref[...] * y


def _gmlp_layer(st, x, mods, ng, w_in, ln_g, ln_b, wmix, bmix, w_out, period, n_vblocks, precise):
    kern = functools.partial(_gmlp_kernel, period=period, precise=precise)
    vb_per = st.grid // n_vblocks
    return pl.pallas_call(
        kern,
        grid=(st.grid,),
        in_specs=[st.rows_spec(), st.mod_spec(0), st.mod_spec(1), st.mod_spec(2),
                  _full_spec((1, D_MODEL)), _full_spec((D_MODEL, 2 * D_GATE)),
                  _full_spec((1, D_GATE)), _full_spec((1, D_GATE)),
                  _full_spec((A_GROUPS, CHUNK, CHUNK)), _full_spec((CHUNK, D_GATE)),
                  _full_spec((D_GATE, D_MODEL))],
        out_specs=[st.rows_spec(),
                   pl.BlockSpec((CHUNK, D_GATE), lambda i: (i // vb_per, 0))],
        out_shape=[jax.ShapeDtypeStruct((st.rows, D_MODEL), F32),
                   jax.ShapeDtypeStruct((n_vblocks * CHUNK, D_GATE), F32)],
        scratch_shapes=[pltpu.VMEM((st.tile, D_GATE), F32)],
        compiler_params=_params("arbitrary"),
        name="gmlp_mixer",
    )(x, mods, mods, mods, ng, w_in, ln_g, ln_b, wmix, bmix, w_out)


def _pool_kernel(x_ref, halo_ref, sh_ref, sc_ref, gt_ref, ng_ref, wgrp_ref, scale_ref,
                 xo_ref, rows_ref, *, halo_is_h, tiles_per_batch, pos0, n_rows_out):
    tm = x_ref.shape[0]
    x = x_ref[...]
    ng, sh, sc = ng_ref[...], sh_ref[...], sc_ref[...]
    h = _modulate(x, ng, sh, sc)
    if halo_is_h:
        halo = halo_ref[...]
    else:
        sh_h = sh if sh.shape[0] == 1 else sh[:POOL_HALO]
        sc_h = sc if sc.shape[0] == 1 else sc[:POOL_HALO]
        first = (pl.program_id(0) % tiles_per_batch) == 0
        halo = jnp.where(first, 0.0, _modulate(halo_ref[...], ng, sh_h, sc_h))
    ext = jnp.concatenate([halo, h], axis=0)
    rows_ref[...] = ext[POOL_HALO + tm - n_rows_out:, :]
    t0 = (pl.program_id(0) % tiles_per_batch) * tm
    pos = pos0 + t0 + lax.broadcasted_iota(jnp.int32, (tm, 1), 0)
    cg = D_MODEL // B_GROUPS
    ys = []
    for g, w in enumerate(POOL_WINDOWS):
        cols = slice(g * cg, (g + 1) * cg)
        acc = ext[:, cols]
        span = 1
        while span < w:
            acc = acc + pltpu.roll(acc, span, 0)
            span *= 2
        cnt = jnp.minimum(pos + 1, w).astype(F32)
        pooled = acc[POOL_HALO:, :] / cnt - h[:, cols]
        ys.append(_bdot(pooled.astype(BF16), wgrp_ref[g]))
    y = jnp.concatenate(ys, axis=-1) * scale_ref[...]
    xo_ref[...] = x + gt_ref[...] * y


def _pool_layer(st, x, halo, halo_spec, mods, ng, w_grp, scale, halo_is_h, pos0, n_rows_out, n_batches):
    kern = functools.partial(_pool_kernel, halo_is_h=halo_is_h, tiles_per_batch=st.tiles_per_batch,
                             pos0=pos0, n_rows_out=n_rows_out)
    cg = D_MODEL // B_GROUPS
    tpb = st.tiles_per_batch
    return pl.pallas_call(
        kern,
        grid=(st.grid,),
        in_specs=[st.rows_spec(), halo_spec, st.mod_spec(0), st.mod_spec(1), st.mod_spec(2),
                  _full_spec((1, D_MODEL)), _full_spec((B_GROUPS, cg, cg)), _full_spec((1, D_MODEL))],
        out_specs=[st.rows_spec(),
                   pl.BlockSpec((n_rows_out, D_MODEL), lambda i: (i // tpb, 0))],
        out_shape=[jax.ShapeDtypeStruct((st.rows, D_MODEL), F32),
                   jax.ShapeDtypeStruct((n_batches * n_rows_out, D_MODEL), F32)],
        compiler_params=_params("arbitrary"),
        name="pool_mixer",
    )(x, halo, mods, mods, mods, ng, w_grp, scale)


def _rope_angles(pos):
    half = ROT_DIM // 2
    inv = ROPE_THETA ** (-(jnp.arange(half, dtype=F32) * 2.0) / ROT_DIM)
    ang = pos.astype(F32)[:, None] * inv[None, :]
    return jnp.cos(ang), jnp.sin(ang)


def _rope_tables(pos):
    cos, sin = _rope_angles(pos)
    n = pos.shape[0]
    pad = jnp.zeros((n, HEAD_DIM - ROT_DIM), F32)
    c = jnp.concatenate([cos, cos, pad + 1.0], axis=1)
    sa = jnp.concatenate([-sin, jnp.zeros_like(sin), pad], axis=1)
    sb = jnp.concatenate([jnp.zeros_like(sin), sin, pad], axis=1)
    return tuple(jnp.concatenate([t, t], axis=1) for t in (c, sa, sb))


def _rope_rows(x, c, sa, sb):
    half = ROT_DIM // 2
    out = []
    for hd in range(N_HEADS):
        xh = x[:, hd * LANES:(hd + 1) * LANES]
        out.append(xh * c + pltpu.roll(xh, LANES - half, 1) * sa + pltpu.roll(xh, half, 1) * sb)
    return out


def _qkv_kernel(x_ref, sh_ref, sc_ref, ng_ref, w_ref, c_ref, sa_ref, sb_ref, q_ref, k_ref, v_ref):
    h = _modulate(x_ref[...], ng_ref[...], sh_ref[...], sc_ref[...]).astype(BF16)
    qkv = _bdot(h, w_ref[...])
    c, sa, sb = c_ref[...], sa_ref[...], sb_ref[...]
    for which, ref in ((0, q_ref), (1, k_ref)):
        rot = _rope_rows(qkv[:, which * D_MODEL:(which + 1) * D_MODEL], c, sa, sb)
        for hd in range(N_HEADS):
            ref[:, hd * LANES:(hd + 1) * LANES] = rot[hd]
    v_ref[...] = qkv[:, 2 * D_MODEL:]


def _qkv_layer(st, x, mods, ng, w_qkv, tables, table_spec):
    return pl.pallas_call(
        _qkv_kernel,
        grid=(st.grid,),
        in_specs=[st.rows_spec(), st.mod_spec(0), st.mod_spec(1), _full_spec((1, D_MODEL)),
                  _full_spec((D_MODEL, 3 * D_MODEL)), table_spec, table_spec, table_spec],
        out_specs=[st.rows_spec(), st.rows_spec(), st.rows_spec()],
        out_shape=[jax.ShapeDtypeStruct((st.rows, D_MODEL), F32)] * 3,
        compiler_params=_params("arbitrary"),
        name="qkv_rope",
    )(x, mods, mods, ng, w_qkv, *tables)


def _qkv_t_kernel(x_ref, sh_ref, sc_ref, ng_ref, wq_ref, wkt_ref, wv_ref, c_ref, sa_ref, sb_ref,
                  ct_ref, st_ref, q_ref, kt_ref, ktb_ref, v_ref, vb_ref):
    h = _modulate(x_ref[...], ng_ref[...], sh_ref[...], sc_ref[...]).astype(BF16)
    rot = _rope_rows(_bdot(h, wq_ref[...]), c_ref[...], sa_ref[...], sb_ref[...])
    for hd in range(N_HEADS):
        q_ref[:, hd * LANES:(hd + 1) * LANES] = (rot[hd] * (HEAD_DIM ** -0.5)).astype(q_ref.dtype)
    kt = lax.dot_general(wkt_ref[...], h, (((1,), (1,)), ((), ())), preferred_element_type=F32)
    ct, st = ct_ref[...], st_ref[...]
    half = ROT_DIM // 2
    for blk in range(2 * N_HEADS):
        base = blk * HEAD_DIM
        x1 = kt[base:base + half]
        x2 = kt[base + half:base + ROT_DIM]
        full = jnp.concatenate([x1 * ct - x2 * st, x2 * ct + x1 * st, kt[base + ROT_DIM:base + HEAD_DIM]], axis=0)
        kt_ref[base:base + HEAD_DIM, :] = full
        ktb_ref[base:base + HEAD_DIM, :] = full.astype(ktb_ref.dtype)
    v = _bdot(h, wv_ref[...])
    v_ref[...] = v
    vb_ref[...] = v.astype(vb_ref.dtype)


def _qkv_t_layer(st, x, mods, ng, wq, wkt, wv, tables, angles_t):
    tpb = st.tiles_per_batch
    tm = st.tile
    tab = pl.BlockSpec((tm, LANES), lambda t: (t % tpb, 0))
    tab_t = pl.BlockSpec((ROT_DIM // 2, tm), lambda t: (0, t % tpb))
    w = _full_spec((D_MODEL, D_MODEL))
    return pl.pallas_call(
        _qkv_t_kernel,
        grid=(st.grid,),
        in_specs=[st.rows_spec(), st.mod_spec(0), st.mod_spec(1), _full_spec((1, D_MODEL)), w, w, w,
                  tab, tab, tab, tab_t, tab_t],
        out_specs=[st.rows_spec(),
                   pl.BlockSpec((None, D_MODEL, tm), lambda t: (t // tpb, 0, t % tpb)),
                   pl.BlockSpec((None, D_MODEL, tm), lambda t: (t // tpb, 0, t % tpb)),
                   st.rows_spec(), st.rows_spec()],
        out_shape=[jax.ShapeDtypeStruct((st.rows, D_MODEL), BF16),
                   jax.ShapeDtypeStruct((BATCH, D_MODEL, SEQ), F32),
                   jax.ShapeDtypeStruct((BATCH, D_MODEL, SEQ), BF16),
                   jax.ShapeDtypeStruct((st.rows, D_MODEL), F32),
                   jax.ShapeDtypeStruct((st.rows, D_MODEL), BF16)],
        compiler_params=_params("arbitrary"),
        name="qkv_rope_prompt",
    )(x, mods, mods, ng, wq, wkt, wv, *tables, *angles_t)


def _lambda(lq1_ref, lk1_ref, lq2_ref, lk2_ref, lambda_init):
    a = jnp.sum(lq1_ref[...] * lk1_ref[...], axis=-1, keepdims=True)
    b = jnp.sum(lq2_ref[...] * lk2_ref[...], axis=-1, keepdims=True)
    return jnp.exp(a) - jnp.exp(b) + lambda_init


def _attn_kernel(q_ref, kt_ref, v_ref, lq1_ref, lk1_ref, lq2_ref, lk2_ref, sg_ref, o_ref, *, lambda_init):
    tq = q_ref.shape[0]
    qi = pl.program_id(2)
    q = q_ref[...]
    first = lax.broadcasted_iota(jnp.int32, (1, LANES), 1) < HEAD_DIM
    zero = jnp.zeros_like(q)
    qs = (jnp.where(first, q, zero), jnp.where(first, zero, q))
    lam = _lambda(lq1_ref, lk1_ref, lq2_ref, lk2_ref, lambda_init)

    def query_tile(n):
        tv = (n + 1) * tq
        past = n * tq
        visible = (lax.broadcasted_iota(jnp.int32, (tq, tq), 1)
                   <= lax.broadcasted_iota(jnp.int32, (tq, tq), 0))
        probs = []
        for comp in range(2):
            s = _bdot(qs[comp], kt_ref[:, :tv])
            s_diag = jnp.where(visible, s[:, past:], -jnp.inf)
            m = jnp.max(s_diag, axis=-1, keepdims=True)
            if n:
                m = jnp.maximum(m, jnp.max(s[:, :past], axis=-1, keepdims=True))
            e_diag = jnp.exp(s_diag - m)
            total = jnp.sum(e_diag, axis=-1, keepdims=True)
            if n:
                e_past = jnp.exp(s[:, :past] - m)
                total = total + jnp.sum(e_past, axis=-1, keepdims=True)
                e = jnp.concatenate([e_past, e_diag], axis=1)
            else:
                e = e_diag
            probs.append(e * (1.0 / total))
        a = (probs[0] - lam * probs[1]).astype(BF16)
        o = _bdot(a, v_ref[:tv, :])
        o_ref[...] = (_rms(o) * sg_ref[...] * (1.0 - lambda_init)).astype(o_ref.dtype)

    for n in range(SEQ // tq):
        pl.when(qi == n)(functools.partial(query_tile, n))


def _attn_prompt(q, ktb, vb, lq1, lk1, lq2, lk2, subln_g, lambda_init):
    nq = SEQ // ATTN_Q_TILE
    kern = functools.partial(_attn_kernel, lambda_init=lambda_init)
    vec = pl.BlockSpec((1, HEAD_DIM), lambda b, h, i: (0, 0))
    return pl.pallas_call(
        kern,
        grid=(BATCH, N_HEADS, nq),
        in_specs=[pl.BlockSpec((ATTN_Q_TILE, LANES), lambda b, h, i: (b * nq + i, h)),
                  pl.BlockSpec((None, LANES, SEQ), lambda b, h, i: (b, h, 0)),
                  pl.BlockSpec((SEQ, LANES), lambda b, h, i: (b, h)),
                  vec, vec, vec, vec,
                  pl.BlockSpec((1, 2 * HEAD_DIM), lambda b, h, i: (0, 0))],
        out_specs=pl.BlockSpec((ATTN_Q_TILE, LANES), lambda b, h, i: (b * nq + i, h)),
        out_shape=jax.ShapeDtypeStruct((N_PROMPT, D_MODEL), BF16),
        compiler_params=_params("arbitrary", "arbitrary", "arbitrary"),
        name="diff_attn_prompt",
    )(q, ktb, vb, lq1, lk1, lq2, lk2, subln_g)


DEC_ROWS = N_HEADS * 2 * DEC_SEQ


def _decode_kernel(pt_ref, q_ref, kn_ref, vn_ref, lq1_ref, lk1_ref, lq2_ref, lk2_ref, sg_ref, *rest,
                   lambda_init):
    k_refs = rest[:PAGES_PER_STEP]
    v_refs = rest[PAGES_PER_STEP:2 * PAGES_PER_STEP]
    o_ref, m_ref, l_ref, acc_ref = rest[2 * PAGES_PER_STEP:]
    step = pl.program_id(1)
    rows_per_head = 2 * DEC_SEQ
    lane = lax.broadcasted_iota(jnp.int32, (rows_per_head, D_MODEL), 1)
    comp = _div_pow2(lax.broadcasted_iota(jnp.int32, (rows_per_head, D_MODEL), 0), DEC_SEQ)
    q8 = q_ref[...] * (HEAD_DIM ** -0.5)
    blk = _div_pow2(lane, HEAD_DIM)
    qbd = jnp.concatenate([jnp.where(blk == 2 * hd + comp, q8, 0.0) for hd in range(N_HEADS)],
                          axis=0).astype(BF16)

    @pl.when(step == 0)
    def _():
        m_ref[...] = jnp.full(m_ref.shape, -jnp.inf, F32)
        l_ref[...] = jnp.zeros(l_ref.shape, F32)
        acc_ref[...] = jnp.zeros(acc_ref.shape, F32)

    m, l, acc = m_ref[...], l_ref[...], acc_ref[...]
    s = jnp.concatenate([_bdot(qbd, kr[...].astype(BF16)) for kr in k_refs], axis=1)
    m_new = jnp.maximum(m, jnp.max(s, axis=-1, keepdims=True))
    alpha = jnp.exp(m - m_new)
    p = jnp.exp(s - m_new)
    l = l * alpha + jnp.sum(p, axis=-1, keepdims=True)
    pv = []
    for hd in range(N_HEADS):
        rows = slice(hd * rows_per_head, (hd + 1) * rows_per_head)
        v_h = jnp.concatenate([vr[pl.ds(hd, PAGE_SIZE, stride=N_HEADS), :].astype(BF16) for vr in v_refs],
                              axis=0)
        pv.append(_bdot(p[rows].astype(BF16), v_h))
    acc = acc * alpha + jnp.concatenate(pv, axis=0)
    m = m_new
    m_ref[...], l_ref[...], acc_ref[...] = m, l, acc

    @pl.when(step == pl.num_programs(1) - 1)
    def _():
        kn = kn_ref[...].astype(BF16).astype(F32)
        vn = vn_ref[...].astype(BF16).astype(F32)
        qf = qbd.astype(F32)
        row_q = _mod_pow2(lax.broadcasted_iota(jnp.int32, (DEC_ROWS, 1), 0), DEC_SEQ)
        s_new = []
        for j in range(DEC_SEQ):
            sj = jnp.sum(qf * kn[j:j + 1, :], axis=-1, keepdims=True)
            s_new.append(jnp.where(row_q >= j, sj, -jnp.inf))
        m2 = m
        for sj in s_new:
            m2 = jnp.maximum(m2, sj)
        alpha = jnp.exp(m - m2)
        l2 = l * alpha
        acc2 = acc * alpha
        for j, sj in enumerate(s_new):
            pj = jnp.exp(sj - m2)
            l2 = l2 + pj
            v_rows = jnp.concatenate(
                [jnp.broadcast_to(vn[j:j + 1, hd * LANES:(hd + 1) * LANES], (rows_per_head, LANES))
                 for hd in range(N_HEADS)], axis=0)
            acc2 = acc2 + pj.astype(BF16).astype(F32) * v_rows
        outn = acc2 / l2
        lam = _lambda(lq1_ref, lk1_ref, lq2_ref, lk2_ref, lambda_init)
        for hd in range(N_HEADS):
            r0 = hd * rows_per_head
            o = outn[r0:r0 + DEC_SEQ] - lam * outn[r0 + DEC_SEQ:r0 + rows_per_head]
            o_ref[:, hd * LANES:(hd + 1) * LANES] = _rms(o) * sg_ref[...] * (1.0 - lambda_init)


def _attn_decode(page_table, q8, k_new, v_new, cache_kt, cache_v, layer, lq1, lk1, lq2, lk2, subln_g, lambda_init):
    n_steps = N_PAGES // PAGES_PER_STEP
    kern = functools.partial(_decode_kernel, lambda_init=lambda_init)
    tok = pl.BlockSpec((None, DEC_SEQ, D_MODEL), lambda b, p, pt: (b, 0, 0))
    vec = pl.BlockSpec((1, HEAD_DIM), lambda b, p, pt: (0, 0))

    def page_spec(i):
        return pl.BlockSpec((None, None, N_HEADS * PAGE_SIZE, LANES),
                            lambda b, p, pt: (pt[b * N_PAGES + p * PAGES_PER_STEP + i], layer, 0, 0))

    grid_spec = pltpu.PrefetchScalarGridSpec(
        num_scalar_prefetch=1,
        grid=(DEC_BATCH, n_steps),
        in_specs=[pl.BlockSpec((None, 2 * DEC_SEQ, D_MODEL), lambda b, p, pt: (b, 0, 0)), tok, tok,
                  vec, vec, vec, vec, pl.BlockSpec((1, 2 * HEAD_DIM), lambda b, p, pt: (0, 0))]
                 + [page_spec(i) for i in range(PAGES_PER_STEP)] * 2,
        out_specs=tok,
        scratch_shapes=[pltpu.VMEM((DEC_ROWS, 1), F32), pltpu.VMEM((DEC_ROWS, 1), F32),
                        pltpu.VMEM((DEC_ROWS, LANES), F32)],
    )
    return pl.pallas_call(
        kern,
        grid_spec=grid_spec,
        out_shape=jax.ShapeDtypeStruct((DEC_BATCH, DEC_SEQ, D_MODEL), F32),
        compiler_params=_params("arbitrary", "arbitrary"),
        name="diff_attn_decode",
    )(page_table.reshape(-1), q8, k_new, v_new, lq1, lk1, lq2, lk2, subln_g,
      *([cache_kt] * PAGES_PER_STEP), *([cache_v] * PAGES_PER_STEP))


def _outproj_kernel(o_ref, x_ref, gt_ref, w_ref, xo_ref):
    xo_ref[...] = x_ref[...] + gt_ref[...] * _bdot(o_ref[...].astype(BF16), w_ref[...])


def _outproj_layer(st, o, x, mods, w_o):
    return pl.pallas_call(
        _outproj_kernel,
        grid=(st.grid,),
        in_specs=[st.rows_spec(), st.rows_spec(), st.mod_spec(2), _full_spec((D_MODEL, D_MODEL))],
        out_specs=st.rows_spec(),
        out_shape=jax.ShapeDtypeStruct((st.rows, D_MODEL), F32),
        compiler_params=_params("arbitrary"),
        name="attn_out_proj",
    )(o, x, mods, w_o)


N_ALL = N_PROMPT + N_SAMPLE
ROW_W = D_MODEL + LANES
PAIRS_PER_GROUP = EXPERTS_PER_GROUP * (EXPERTS_PER_GROUP - 1) // 2
N_BUCKETS = N_EXPERT_GROUPS * PAIRS_PER_GROUP
SORT_TILE = 256
N_SORT_TILES = (N_ALL + N_BUCKETS * (SORT_TILE - 1)) // SORT_TILE
N_SORT_ROWS = N_SORT_TILES * SORT_TILE
DMA_UNROLL = 8


def _router_kernel(x_ref, sh_ref, sc_ref, ng_ref, rwt_ref, rb_ref, route_ref, gates_ref):
    tm = x_ref.shape[0]
    h = _modulate(x_ref[...], ng_ref[...], sh_ref[...], sc_ref[...])
    logits = lax.dot_general(rwt_ref[...], h, (((1,), (1,)), ((), ())), preferred_element_type=F32,
                             precision=lax.Precision.HIGHEST)
    scores = jax.nn.sigmoid(logits)
    sel = scores + rb_ref[...]
    eid = lax.broadcasted_iota(jnp.int32, sel.shape, 0)
    grp = _div_pow2(eid, EXPERTS_PER_GROUP)
    neg = -jnp.inf

    def first_max(vals):
        m = jnp.max(vals, axis=0, keepdims=True)
        idx = jnp.min(jnp.where(vals == m, eid, N_EXPERTS), axis=0, keepdims=True)
        return m, idx

    best = None
    for g in range(N_EXPERT_GROUPS):
        vg = jnp.where(grp == g, sel, neg)
        m1, i1 = first_max(vg)
        m2, _ = first_max(jnp.where(eid == i1, neg, vg))
        gs = m1 + m2
        if best is None:
            best, best_score = jnp.zeros_like(i1), gs
        else:
            better = gs > best_score
            best = jnp.where(better, g, best)
            best_score = jnp.where(better, gs, best_score)
    masked = jnp.where(grp == best, sel, neg)
    _, ia = first_max(masked)
    _, ib = first_max(jnp.where(eid == ia, neg, masked))
    ga = jnp.sum(jnp.where(eid == ia, scores, 0.0), axis=0, keepdims=True)
    gb = jnp.sum(jnp.where(eid == ib, scores, 0.0), axis=0, keepdims=True)
    den = ga + gb
    a_first = ia < ib
    lo = jnp.where(a_first, ia, ib).astype(F32)
    hi = jnp.where(a_first, ib, ia).astype(F32)
    g_lo = jnp.where(a_first, ga, gb) / den
    g_hi = jnp.where(a_first, gb, ga) / den
    r8 = lax.broadcasted_iota(jnp.int32, route_ref.shape, 0)
    route_ref[...] = jnp.where(r8 == 0, lo, jnp.where(r8 == 1, hi, 0.0))
    r128 = lax.broadcasted_iota(jnp.int32, (LANES, tm), 0)
    gates_t = jnp.where(r128 == 0, g_lo, jnp.where(r128 == 1, g_hi, 0.0))
    gates_ref[...] = gates_t.T


ROUTE_ROWS = 8


def _router_layer(st, x, mods, ng, router_wt, router_b):
    return pl.pallas_call(
        _router_kernel,
        grid=(st.grid,),
        in_specs=[st.rows_spec(), st.mod_spec(3), st.mod_spec(4), _full_spec((1, D_MODEL)),
                  _full_spec((N_EXPERTS, D_MODEL)), _full_spec((N_EXPERTS, 1))],
        out_specs=[pl.BlockSpec((ROUTE_ROWS, st.tile), lambda i: (0, i)), st.rows_spec(LANES)],
        out_shape=[jax.ShapeDtypeStruct((ROUTE_ROWS, st.rows), F32),
                   jax.ShapeDtypeStruct((st.rows, LANES), F32)],
        compiler_params=_params("arbitrary"),
        name="moe_router",
    )(x, mods, mods, ng, router_wt, router_b)


def _moe_plan(route):
    lo = route[0].astype(jnp.int32)
    hi = route[1].astype(jnp.int32)
    i, j = lo % EXPERTS_PER_GROUP, hi % EXPERTS_PER_GROUP
    pair = i * (2 * EXPERTS_PER_GROUP - 1 - i) // 2 + (j - i - 1)
    bucket = (lo // EXPERTS_PER_GROUP) * PAIRS_PER_GROUP + pair
    onehot = (bucket[:, None] == jnp.arange(N_BUCKETS)[None, :]).astype(jnp.int32)
    csum = jnp.cumsum(onehot, axis=0)
    counts = csum[-1]
    rank = jnp.sum(onehot * csum, axis=1) - 1
    tiles = (counts + SORT_TILE - 1) // SORT_TILE
    ends = jnp.cumsum(tiles)
    starts = ends - tiles
    n_used = ends[-1]
    pos = jnp.sum(onehot * starts[None, :], axis=1) * SORT_TILE + rank
    tile_id = jnp.arange(N_SORT_TILES)
    tile_bucket = jnp.sum((jnp.minimum(tile_id, n_used - 1)[:, None] >= ends[None, :]).astype(jnp.int32), axis=1)
    tile_bucket = jnp.minimum(tile_bucket, N_BUCKETS - 1)
    pair_lo = jnp.array([a for a in range(EXPERTS_PER_GROUP) for _ in range(a + 1, EXPERTS_PER_GROUP)], jnp.int32)
    pair_hi = jnp.array([b for a in range(EXPERTS_PER_GROUP) for b in range(a + 1, EXPERTS_PER_GROUP)], jnp.int32)
    base = (tile_bucket // PAIRS_PER_GROUP) * EXPERTS_PER_GROUP
    tile_lo = base + pair_lo[tile_bucket % PAIRS_PER_GROUP]
    tile_hi = base + pair_hi[tile_bucket % PAIRS_PER_GROUP]
    return pos.astype(jnp.int32), tile_lo, tile_hi, n_used.reshape(1).astype(jnp.int32)


def _bucket_kernel(pos_ref, x_ref, sh_ref, sc_ref, ng_ref, gates_ref, dst_in_ref, dst_ref, buf, sem, *, row0):
    del dst_in_ref
    tm = x_ref.shape[0]
    t = pl.program_id(0)
    nt = pl.num_programs(0)
    slot = t % 2

    def wait(s):
        pltpu.make_async_copy(buf.at[s], dst_ref.at[pl.ds(0, tm), :], sem.at[s]).wait()

    @pl.when(t >= 2)
    def _():
        wait(slot)

    buf[slot, :, :D_MODEL] = _modulate(x_ref[...], ng_ref[...], sh_ref[...], sc_ref[...])
    buf[slot, :, D_MODEL:] = gates_ref[...]
    base = row0 + t * tm

    def body(i, carry):
        for u in range(DMA_UNROLL):
            r = i * DMA_UNROLL + u
            pltpu.make_async_copy(buf.at[slot, pl.ds(r, 1), :],
                                  dst_ref.at[pl.ds(pos_ref[base + r], 1), :], sem.at[slot]).start()
        return carry

    lax.fori_loop(0, tm // DMA_UNROLL, body, 0)

    @pl.when(t == nt - 1)
    def _():
        wait(slot)

    @pl.when((t == nt - 1) & (nt > 1))
    def _():
        wait(1 - slot)


def _bucket_rows(st, pos, x, mods, ng, gates, dst, row0):
    kern = functools.partial(_bucket_kernel, row0=row0)
    if st.per_row_mods:
        def mod_spec(k):
            return pl.BlockSpec((st.tile, D_MODEL), lambda i, p: (i, k))
    else:
        tpb = st.tiles_per_batch

        def mod_spec(k):
            return pl.BlockSpec((None, None, 1, D_MODEL), lambda i, p: (i // tpb, k, 0, 0))
    hbm = pl.BlockSpec(memory_space=pl.ANY)
    grid_spec = pltpu.PrefetchScalarGridSpec(
        num_scalar_prefetch=1,
        grid=(st.grid,),
        in_specs=[pl.BlockSpec((st.tile, D_MODEL), lambda i, p: (i, 0)), mod_spec(3), mod_spec(4),
                  pl.BlockSpec((1, D_MODEL), lambda i, p: (0, 0)),
                  pl.BlockSpec((st.tile, LANES), lambda i, p: (i, 0)), hbm],
        out_specs=hbm,
        scratch_shapes=[pltpu.VMEM((2, st.tile, ROW_W), F32), pltpu.SemaphoreType.DMA((2,))],
    )
    return pl.pallas_call(
        kern,
        grid_spec=grid_spec,
        out_shape=jax.ShapeDtypeStruct(dst.shape, dst.dtype),
        input_output_aliases={6: 0},
        compiler_params=_params("arbitrary"),
        name="moe_bucket_rows",
    )(pos, x, mods, mods, ng, gates, dst)


def _moe_kernel(lo_ref, hi_ref, nu_ref, xs_ref, wg_lo, wu_lo, wd_lo, wg_hi, wu_hi, wd_hi, ys_ref):
    del lo_ref, hi_ref
    t = pl.program_id(0)

    @pl.when(t < nu_ref[0])
    def _():
        row = xs_ref[...]
        x = row[:, :D_MODEL].astype(BF16)

        def expert(wg, wu, wd, gate):
            g = _bdot(x, wg[...])
            u = _bdot(x, wu[...])
            return _bdot((jax.nn.silu(g) * u * gate).astype(BF16), wd[...])

        ys_ref[...] = (expert(wg_lo, wu_lo, wd_lo, row[:, D_MODEL:D_MODEL + 1])
                       + expert(wg_hi, wu_hi, wd_hi, row[:, D_MODEL + 1:D_MODEL + 2]))

    @pl.when(t >= nu_ref[0])
    def _():
        ys_ref[...] = jnp.zeros(ys_ref.shape, ys_ref.dtype)


def _moe_experts(xs, tile_lo, tile_hi, n_used, wg, wu, wd, layer):
    def w_spec(shape, which):
        return pl.BlockSpec((None, None) + shape,
                            lambda t, lo, hi, nu: (layer, (lo, hi)[which][t], 0, 0))

    up, down = (D_MODEL, D_EXPERT), (D_EXPERT, D_MODEL)
    grid_spec = pltpu.PrefetchScalarGridSpec(
        num_scalar_prefetch=3,
        grid=(N_SORT_TILES,),
        in_specs=[pl.BlockSpec((SORT_TILE, ROW_W), lambda t, lo, hi, nu: (t, 0)),
                  w_spec(up, 0), w_spec(up, 0), w_spec(down, 0),
                  w_spec(up, 1), w_spec(up, 1), w_spec(down, 1)],
        out_specs=pl.BlockSpec((SORT_TILE, D_MODEL), lambda t, lo, hi, nu: (t, 0)),
    )
    return pl.pallas_call(
        _moe_kernel,
        grid_spec=grid_spec,
        out_shape=jax.ShapeDtypeStruct((N_SORT_ROWS, D_MODEL), F32),
        compiler_params=_params("arbitrary"),
        name="moe_experts",
    )(tile_lo, tile_hi, n_used, xs, wg, wu, wd, wg, wu, wd)


def _unsort_kernel(pos_ref, ys_ref, x_ref, g2_ref, o_ref, buf, sem, *, row0):
    tm = x_ref.shape[0]
    t = pl.program_id(0)

    def fetch(tile, slot):
        base = row0 + tile * tm

        def body(i, carry):
            for u in range(DMA_UNROLL):
                r = i * DMA_UNROLL + u
                pltpu.make_async_copy(ys_ref.at[pl.ds(pos_ref[base + r], 1), :],
                                      buf.at[slot, pl.ds(r, 1), :], sem.at[slot]).start()
            return carry

        lax.fori_loop(0, tm // DMA_UNROLL, body, 0)

    @pl.when(t == 0)
    def _():
        fetch(0, 0)

    @pl.when(t + 1 < pl.num_programs(0))
    def _():
        fetch(t + 1, (t + 1) % 2)

    slot = t % 2
    pltpu.make_async_copy(ys_ref.at[pl.ds(0, tm), :], buf.at[slot], sem.at[slot]).wait()
    o_ref[...] = x_ref[...] + g2_ref[...] * buf[slot]


def _unsort_layer(st, pos, ys, x, mods, row0):
    kern = functools.partial(_unsort_kernel, row0=row0)
    if st.per_row_mods:
        g2_spec = pl.BlockSpec((st.tile, D_MODEL), lambda i, p: (i, 5))
    else:
        tpb = st.tiles_per_batch
        g2_spec = pl.BlockSpec((None, None, 1, D_MODEL), lambda i, p: (i // tpb, 5, 0, 0))
    rows = pl.BlockSpec((st.tile, D_MODEL), lambda i, p: (i, 0))
    grid_spec = pltpu.PrefetchScalarGridSpec(
        num_scalar_prefetch=1,
        grid=(st.grid,),
        in_specs=[pl.BlockSpec(memory_space=pl.ANY), rows, g2_spec],
        out_specs=rows,
        scratch_shapes=[pltpu.VMEM((2, st.tile, D_MODEL), F32), pltpu.SemaphoreType.DMA((2,))],
    )
    return pl.pallas_call(
        kern,
        grid_spec=grid_spec,
        out_shape=jax.ShapeDtypeStruct((st.rows, D_MODEL), F32),
        compiler_params=_params("arbitrary"),
        name="moe_unsort_residual",
    )(pos, ys, x, mods)


def _final_kernel(x_ref, g_ref, o_ref):
    o_ref[...] = _rms(x_ref[...]) * g_ref[...]


def _final_norm(st, x, g):
    return pl.pallas_call(
        _final_kernel,
        grid=(st.grid,),
        in_specs=[st.rows_spec(), _full_spec((1, D_MODEL))],
        out_specs=st.rows_spec(),
        out_shape=jax.ShapeDtypeStruct((st.rows, D_MODEL), F32),
        compiler_params=_params("arbitrary"),
        name="final_norm",
    )(x, g)


def kernel(x_prompt, x_sample, cache_k, cache_v, state_pool, page_table, c_prompt, c_sample,
           norm1_g, norm2_g, ada_w, ada_b, a_w_in, a_ln_g, a_ln_b, a_w_s, a_b_s, a_w_out,
           b_w_grp, b_scale, c_w_qkv, c_lq1, c_lk1, c_lq2, c_lk2, c_subln_g, c_w_o,
           router_w, router_bias, e_w_gate, e_w_up, e_w_down, final_g):
    sp = _Stream(N_PROMPT, PROMPT_TILE, SEQ // PROMPT_TILE, per_row_mods=False)
    ss = _Stream(N_SAMPLE, N_SAMPLE, 1, per_row_mods=True)

    xp = x_prompt.reshape(N_PROMPT, D_MODEL)
    xs = x_sample.reshape(N_SAMPLE, D_MODEL)
    mods_all = _ada(jnp.concatenate([c_prompt, c_sample], axis=0), ada_w, ada_b)

    wg_b, wu_b, wd_b = e_w_gate.astype(BF16), e_w_up.astype(BF16), e_w_down.astype(BF16)
    rwt = router_w.T
    rb = router_bias.reshape(N_EXPERTS, 1)
    bucketed = jnp.zeros((N_SORT_ROWS, ROW_W), F32)
    n_phys = cache_k.shape[0]
    n_c_layers = cache_k.shape[1]
    ckt = jnp.transpose(cache_k, (0, 1, 3, 4, 5, 2)).reshape(n_phys, n_c_layers, D_MODEL, PAGE_SIZE)
    cv = cache_v.reshape(n_phys, n_c_layers, PAGE_SIZE * N_HEADS, 2 * HEAD_DIM)

    av_p, av_s, pr_p, pr_s, kp_l, vp_l, ks_l, vs_l = [], [], [], [], [], [], [], []
    for i in range(DEPTH):
        kind, j = i % N_MIXERS, i // N_MIXERS
        mp = mods_all[i, :BATCH].reshape(BATCH, N_MODS, 1, D_MODEL)
        ms = jnp.repeat(mods_all[i, BATCH:], DEC_SEQ, axis=0)
        ng1 = norm1_g[i].reshape(1, D_MODEL)
        ng2 = norm2_g[i].reshape(1, D_MODEL)
        if kind == 0:
            w_in = a_w_in[j].astype(BF16)
            w_out = a_w_out[j].astype(BF16)
            ln_g, ln_b = a_ln_g[j].reshape(1, D_GATE), a_ln_b[j].reshape(1, D_GATE)
            bias = jnp.repeat(a_b_s[j].T, D_GATE // A_GROUPS, axis=1)
            xp, vrow_p = _gmlp_layer(sp, xp, mp, ng1, w_in, ln_g, ln_b, a_w_s[j], bias, w_out, CHUNK, BATCH, False)
            reps = CHUNK // DEC_SEQ
            wmix_s = jnp.tile(a_w_s[j][:, :DEC_SEQ, :DEC_SEQ], (1, reps, reps))
            bias_s = jnp.tile(bias[:DEC_SEQ], (reps, 1))
            xs, vrow_s = _gmlp_layer(ss, xs, ms, ng1, a_w_in[j], ln_g, ln_b, wmix_s, bias_s, a_w_out[j],
                                     DEC_SEQ, 1, True)
            av_p.append(vrow_p.reshape(BATCH, CHUNK, D_GATE))
            av_s.append(vrow_s.reshape(DEC_BATCH, DEC_SEQ, D_GATE))
        elif kind == 1:
            w_grp = b_w_grp[j].astype(BF16)
            scale = b_scale[j].reshape(1, D_MODEL)
            hb = PROMPT_TILE // POOL_HALO
            halo_spec = pl.BlockSpec((POOL_HALO, D_MODEL), lambda t: (jnp.maximum(t * hb - 1, 0), 0))
            xp, rows_p = _pool_layer(sp, xp, xp, halo_spec, mp, ng1, w_grp, scale, False, 0, POOL_HALO, BATCH)
            pr_p.append(rows_p.reshape(BATCH, POOL_HALO, D_MODEL)[:, 1:])
            pad_t = 2 * DEC_SEQ
            s8 = _Stream(DEC_BATCH * pad_t, pad_t, 1, per_row_mods=True)
            xs8 = jnp.pad(xs.reshape(DEC_BATCH, DEC_SEQ, D_MODEL), ((0, 0), (0, DEC_SEQ), (0, 0)))
            ms8 = jnp.repeat(mods_all[i, BATCH:], pad_t, axis=0)
            prev = jnp.pad(state_pool[:, j], ((0, 0), (1, 0), (0, 0))).reshape(DEC_BATCH * POOL_HALO, D_MODEL)
            prev_spec = pl.BlockSpec((POOL_HALO, D_MODEL), lambda t: (t, 0))
            n_out = POOL_HALO + pad_t
            xs8, rows_s = _pool_layer(s8, xs8.reshape(-1, D_MODEL), prev, prev_spec, ms8, ng1, w_grp, scale,
                                      True, PAST_LEN, n_out, DEC_BATCH)
            xs = xs8.reshape(DEC_BATCH, pad_t, D_MODEL)[:, :DEC_SEQ].reshape(N_SAMPLE, D_MODEL)
            lo = 1 + DEC_SEQ
            pr_s.append(rows_s.reshape(DEC_BATCH, n_out, D_MODEL)[:, lo:lo + POOL_BUF])
        else:
            lambda_init = 0.8 - 0.6 * math.exp(-0.3 * i)
            w_qkv = c_w_qkv[j].astype(BF16)
            w_o = c_w_o[j].astype(BF16)
            lq1, lk1 = c_lq1[j].reshape(1, HEAD_DIM), c_lk1[j].reshape(1, HEAD_DIM)
            lq2, lk2 = c_lq2[j].reshape(1, HEAD_DIM), c_lk2[j].reshape(1, HEAD_DIM)
            sg = c_subln_g[j].reshape(1, 2 * HEAD_DIM)
            pos_p = jnp.arange(SEQ)
            cos_p, sin_p = _rope_angles(pos_p)
            qp, kt, ktb, vp, vpb = _qkv_t_layer(
                sp, xp, mp, ng1, w_qkv[:, :D_MODEL], w_qkv[:, D_MODEL:2 * D_MODEL].T, w_qkv[:, 2 * D_MODEL:],
                _rope_tables(pos_p), (cos_p.T, sin_p.T))
            op = _attn_prompt(qp, ktb, vpb, lq1, lk1, lq2, lk2, sg, lambda_init)
            xp = _outproj_layer(sp, op, xp, mp, w_o)
            tab_s = _rope_tables(jnp.tile(PAST_LEN + jnp.arange(DEC_SEQ), DEC_BATCH))
            qs, ks, vs = _qkv_layer(ss, xs, ms, ng1, w_qkv, tab_s,
                                    pl.BlockSpec((N_SAMPLE, LANES), lambda t: (0, 0)))
            tok = (DEC_BATCH, DEC_SEQ, D_MODEL)
            q8 = jnp.concatenate([qs.reshape(tok)] * 2, axis=1)
            os_ = _attn_decode(page_table, q8, ks.reshape(tok), vs.reshape(tok), ckt, cv, j,
                               lq1, lk1, lq2, lk2, sg, lambda_init)
            xs = _outproj_layer(ss, os_.reshape(N_SAMPLE, D_MODEL), xs, ms, w_o)
            kp = jnp.transpose(kt.reshape(BATCH, N_HEADS, 2, HEAD_DIM, SEQ), (0, 4, 1, 2, 3))
            kp_l.append(kp)
            vp_l.append(vp.reshape(BATCH, SEQ, N_HEADS, 2 * HEAD_DIM))
            ks_l.append(ks.reshape(DEC_BATCH, DEC_SEQ, N_HEADS, 2, HEAD_DIM))
            vs_l.append(vs.reshape(DEC_BATCH, DEC_SEQ, N_HEADS, 2 * HEAD_DIM))

        route_p, gates_p = _router_layer(sp, xp, mp, ng2, rwt, rb)
        route_s, gates_s = _router_layer(ss, xs, ms, ng2, rwt, rb)
        pos, tile_lo, tile_hi, n_used = _moe_plan(jnp.concatenate([route_p, route_s], axis=1))
        bucketed = _bucket_rows(sp, pos, xp, mp, ng2, gates_p, bucketed, 0)
        bucketed = _bucket_rows(ss, pos, xs, ms, ng2, gates_s, bucketed, N_PROMPT)
        ys = _moe_experts(bucketed, tile_lo, tile_hi, n_used, wg_b, wu_b, wd_b, i)
        xp = _unsort_layer(sp, pos, ys, xp, mp, 0)
        xs = _unsort_layer(ss, pos, ys, xs, ms, N_PROMPT)

    fg = final_g.reshape(1, D_MODEL)
    y_prompt = _final_norm(sp, xp, fg).reshape(BATCH, SEQ, D_MODEL)
    y_sample = _final_norm(ss, xs, fg).reshape(DEC_BATCH, DEC_SEQ, D_MODEL)
    return (y_prompt, y_sample,
            jnp.stack(av_p, axis=1), jnp.stack(av_s, axis=1),
            jnp.stack(pr_p, axis=1), jnp.stack(pr_s, axis=1),
            jnp.stack(kp_l, axis=1), jnp.stack(vp_l, axis=1),
            jnp.stack(ks_l, axis=1), jnp.stack(vs_l, axis=1))
```

```python
import functools
import math

import jax
import jax.numpy as jnp
from jax import lax
from jax.experimental import pallas as pl
from jax.experimental.pallas import tpu as pltpu

F32 = jnp.float32
BF16 = jnp.bfloat16

D_MODEL = 1024
BATCH = 8
SEQ = 2048
DEPTH = 4
DEC_BATCH = 32
DEC_SEQ = 4
PAST_LEN = 8192
PAGE_SIZE = 128
N_PAGES = PAST_LEN // PAGE_SIZE
N_MIXERS = 3
CHUNK = 128
D_GATE = D_MODEL
A_GROUPS = 4
POOL_WINDOWS = (2, 4, 8, 16)
B_GROUPS = len(POOL_WINDOWS)
POOL_BUF = max(POOL_WINDOWS) - 1
POOL_HALO = POOL_BUF + 1
N_HEADS = 8
HEAD_DIM = D_MODEL // N_HEADS // 2
ROT_DIM = HEAD_DIM // 4
ROPE_THETA = 500000.0
N_EXPERTS = 16
N_EXPERT_GROUPS = 4
EXPERTS_PER_GROUP = N_EXPERTS // N_EXPERT_GROUPS
D_EXPERT = 512
EPS = 1e-6
N_MODS = 6

N_PROMPT = BATCH * SEQ
N_SAMPLE = DEC_BATCH * DEC_SEQ
LANES = 128
VMEM_LIMIT = 56 * 1024 * 1024

PROMPT_TILE = 256
ATTN_Q_TILE = 256
PAGES_PER_STEP = 8


def _params(*sem):
    return pltpu.CompilerParams(dimension_semantics=sem, vmem_limit_bytes=VMEM_LIMIT)


class _Stream:
    def __init__(self, rows, tile, tiles_per_batch, per_row_mods):
        self.rows = rows
        self.tile = tile
        self.tiles_per_batch = tiles_per_batch
        self.per_row_mods = per_row_mods
        self.grid = rows // tile

    def rows_spec(self, width=D_MODEL):
        return pl.BlockSpec((self.tile, width), lambda i: (i, 0))

    def mod_spec(self, k):
        if self.per_row_mods:
            return pl.BlockSpec((self.tile, D_MODEL), lambda i: (i, k))
        tpb = self.tiles_per_batch
        return pl.BlockSpec((None, None, 1, D_MODEL), lambda i: (i // tpb, k, 0, 0))


def _full_spec(shape):
    nd = len(shape)
    return pl.BlockSpec(shape, lambda i: (0,) * nd)


def _rms(x):
    return x * lax.rsqrt(jnp.mean(x * x, axis=-1, keepdims=True) + EPS)


def _modulate(x, g, sh, sc):
    return _rms(x) * g * (1.0 + sc) + sh


def _bdot(a, b):
    return jnp.dot(a, b, preferred_element_type=F32)


def _div_pow2(x, n):
    assert n & (n - 1) == 0
    return lax.shift_right_logical(x, n.bit_length() - 1)


def _mod_pow2(x, n):
    assert n & (n - 1) == 0
    return x & (n - 1)


def _ada_kernel(c_ref, w_ref, b_ref, o_ref):
    o_ref[...] = jnp.dot(jax.nn.silu(c_ref[...]), w_ref[...], preferred_element_type=F32,
                         precision=lax.Precision.HIGHEST) + b_ref[...]


def _ada(c_all, ada_w, ada_b):
    nb = c_all.shape[0]
    tn = 1536
    return pl.pallas_call(
        _ada_kernel,
        grid=(DEPTH, N_MODS * D_MODEL // tn),
        in_specs=[pl.BlockSpec((nb, D_MODEL), lambda l, n: (0, 0)),
                  pl.BlockSpec((None, D_MODEL, tn), lambda l, n: (l, 0, n)),
                  pl.BlockSpec((None, 1, tn), lambda l, n: (l, 0, n))],
        out_specs=pl.BlockSpec((None, nb, tn), lambda l, n: (l, 0, n)),
        out_shape=jax.ShapeDtypeStruct((DEPTH, nb, N_MODS * D_MODEL), F32),
        compiler_params=_params("arbitrary", "arbitrary"),
        name="ada_mod",
    )(c_all, ada_w, ada_b.reshape(DEPTH, 1, N_MODS * D_MODEL))


def _mm(a, b, precise):
    if precise:
        return jnp.dot(a.astype(F32), b.astype(F32), preferred_element_type=F32, precision=lax.Precision.HIGHEST)
    return _bdot(a.astype(BF16), b.astype(BF16))


def _gmlp_kernel(x_ref, sh_ref, sc_ref, gt_ref, ng_ref, win_ref, lng_ref, lnb_ref, wmix_ref,
                 bmix_ref, wout_ref, *rest, period, precise):
    route_in = rest[:N_ROUTE_IN]
    xo_ref, v_ref, route_ref, gates_ref, mix_ref = rest[N_ROUTE_IN:]
    tm = x_ref.shape[0]
    x = x_ref[...]
    h = _modulate(x, ng_ref[...], sh_ref[...], sc_ref[...])
    z = _mm(h, win_ref[...], precise)
    z = 0.5 * z * (1.0 + lax.erf(z * (2.0 ** -0.5)))
    u = z[:, :D_GATE]
    v = z[:, D_GATE:]
    mu = jnp.mean(v, axis=-1, keepdims=True)
    vc = v - mu
    var = jnp.mean(vc * vc, axis=-1, keepdims=True)
    v = vc * lax.rsqrt(var + EPS) * lng_ref[...] + lnb_ref[...]
    v_ref[...] = v[tm - CHUNK:, :]
    vb = v if precise else v.astype(BF16)
    t = lax.broadcasted_iota(jnp.int32, (CHUNK, CHUNK), 0)
    s = lax.broadcasted_iota(jnp.int32, (CHUNK, CHUNK), 1)
    keep = (s <= t) & (_div_pow2(t, period) == _div_pow2(s, period))
    cg = D_GATE // A_GROUPS
    for g in range(A_GROUPS):
        wm = jnp.where(keep, wmix_ref[g], 0.0)
        for c in range(tm // CHUNK):
            rows = slice(c * CHUNK, (c + 1) * CHUNK)
            cols = slice(g * cg, (g + 1) * cg)
            mix_ref[rows, cols] = _mm(wm, vb[rows, cols], precise) + bmix_ref[:, cols]
    y = _mm(u * mix_ref[...], wout_ref[...], precise)
    x_new = x + gt_ref[...] * y
    xo_ref[...] = x_new
    _route_store(x_new, route_in, route_ref, gates_ref)


def _gmlp_layer(st, x, mods, ng, w_in, ln_g, ln_b, wmix, bmix, w_out, period, n_vblocks, precise, routing):
    kern = functools.partial(_gmlp_kernel, period=period, precise=precise)
    vb_per = st.grid // n_vblocks
    return pl.pallas_call(
        kern,
        grid=(st.grid,),
        in_specs=[st.rows_spec(), st.mod_spec(0), st.mod_spec(1), st.mod_spec(2),
                  _full_spec((1, D_MODEL)), _full_spec((D_MODEL, 2 * D_GATE)),
                  _full_spec((1, D_GATE)), _full_spec((1, D_GATE)),
                  _full_spec((A_GROUPS, CHUNK, CHUNK)), _full_spec((CHUNK, D_GATE)),
                  _full_spec((D_GATE, D_MODEL))] + routing.in_specs,
        out_specs=[st.rows_spec(),
                   pl.BlockSpec((CHUNK, D_GATE), lambda i: (i // vb_per, 0))] + routing.out_specs,
        out_shape=[jax.ShapeDtypeStruct((st.rows, D_MODEL), F32),
                   jax.ShapeDtypeStruct((n_vblocks * CHUNK, D_GATE), F32)] + routing.out_shape,
        scratch_shapes=[pltpu.VMEM((st.tile, D_GATE), F32)],
        compiler_params=_params("arbitrary"),
        name="gmlp_mixer",
    )(x, mods, mods, mods, ng, w_in, ln_g, ln_b, wmix, bmix, w_out, *routing.args)


def _pool_kernel(x_ref, halo_ref, sh_ref, sc_ref, gt_ref, ng_ref, wgrp_ref, scale_ref, *rest,
                 halo_is_h, tiles_per_batch, pos0, n_rows_out, routed):
    if routed:
        route_in = rest[:N_ROUTE_IN]
        xo_ref, rows_ref, route_ref, gates_ref = rest[N_ROUTE_IN:]
    else:
        xo_ref, rows_ref = rest
    tm = x_ref.shape[0]
    x = x_ref[...]
    ng, sh, sc = ng_ref[...], sh_ref[...], sc_ref[...]
    h = _modulate(x, ng, sh, sc)
    if halo_is_h:
        halo = halo_ref[...]
    else:
        sh_h = sh if sh.shape[0] == 1 else sh[:POOL_HALO]
        sc_h = sc if sc.shape[0] == 1 else sc[:POOL_HALO]
        first = (pl.program_id(0) % tiles_per_batch) == 0
        halo = jnp.where(first, 0.0, _modulate(halo_ref[...], ng, sh_h, sc_h))
    ext = jnp.concatenate([halo, h], axis=0)
    rows_ref[...] = ext[POOL_HALO + tm - n_rows_out:, :]
    t0 = (pl.program_id(0) % tiles_per_batch) * tm
    pos = pos0 + t0 + lax.broadcasted_iota(jnp.int32, (tm, 1), 0)
    cg = D_MODEL // B_GROUPS
    ys = []
    for g, w in enumerate(POOL_WINDOWS):
        cols = slice(g * cg, (g + 1) * cg)
        acc = ext[:, cols]
        span = 1
        while span < w:
            acc = acc + pltpu.roll(acc, span, 0)
            span *= 2
        cnt = jnp.minimum(pos + 1, w).astype(F32)
        pooled = acc[POOL_HALO:, :] / cnt - h[:, cols]
        ys.append(_bdot(pooled.astype(BF16), wgrp_ref[g]))
    y = jnp.concatenate(ys, axis=-1) * scale_ref[...]
    x_new = x + gt_ref[...] * y
    xo_ref[...] = x_new
    if routed:
        _route_store(x_new, route_in, route_ref, gates_ref)


def _pool_layer(st, x, halo, halo_spec, mods, ng, w_grp, scale, halo_is_h, pos0, n_rows_out, n_batches,
                routing=None):
    kern = functools.partial(_pool_kernel, halo_is_h=halo_is_h, tiles_per_batch=st.tiles_per_batch,
                             pos0=pos0, n_rows_out=n_rows_out, routed=routing is not None)
    cg = D_MODEL // B_GROUPS
    tpb = st.tiles_per_batch
    extra_in = routing.in_specs if routing else []
    extra_out = routing.out_specs if routing else []
    extra_shape = routing.out_shape if routing else []
    extra_args = routing.args if routing else []
    return pl.pallas_call(
        kern,
        grid=(st.grid,),
        in_specs=[st.rows_spec(), halo_spec, st.mod_spec(0), st.mod_spec(1), st.mod_spec(2),
                  _full_spec((1, D_MODEL)), _full_spec((B_GROUPS, cg, cg)), _full_spec((1, D_MODEL))] + extra_in,
        out_specs=[st.rows_spec(),
                   pl.BlockSpec((n_rows_out, D_MODEL), lambda i: (i // tpb, 0))] + extra_out,
        out_shape=[jax.ShapeDtypeStruct((st.rows, D_MODEL), F32),
                   jax.ShapeDtypeStruct((n_batches * n_rows_out, D_MODEL), F32)] + extra_shape,
        compiler_params=_params("arbitrary"),
        name="pool_mixer",
    )(x, halo, mods, mods, mods, ng, w_grp, scale, *extra_args)


def _rope_angles(pos):
    half = ROT_DIM // 2
    inv = ROPE_THETA ** (-(jnp.arange(half, dtype=F32) * 2.0) / ROT_DIM)
    ang = pos.astype(F32)[:, None] * inv[None, :]
    return jnp.cos(ang), jnp.sin(ang)


def _rope_tables(pos):
    cos, sin = _rope_angles(pos)
    n = pos.shape[0]
    pad = jnp.zeros((n, HEAD_DIM - ROT_DIM), F32)
    c = jnp.concatenate([cos, cos, pad + 1.0], axis=1)
    sa = jnp.concatenate([-sin, jnp.zeros_like(sin), pad], axis=1)
    sb = jnp.concatenate([jnp.zeros_like(sin), sin, pad], axis=1)
    return tuple(jnp.concatenate([t, t], axis=1) for t in (c, sa, sb))


def _rope_rows(x, c, sa, sb):
    half = ROT_DIM // 2
    out = []
    for hd in range(N_HEADS):
        xh = x[:, hd * LANES:(hd + 1) * LANES]
        out.append(xh * c + pltpu.roll(xh, LANES - half, 1) * sa + pltpu.roll(xh, half, 1) * sb)
    return out


def _qkv_kernel(x_ref, sh_ref, sc_ref, ng_ref, w_ref, c_ref, sa_ref, sb_ref, q_ref, k_ref, v_ref):
    h = _modulate(x_ref[...], ng_ref[...], sh_ref[...], sc_ref[...]).astype(BF16)
    qkv = _bdot(h, w_ref[...])
    c, sa, sb = c_ref[...], sa_ref[...], sb_ref[...]
    for which, ref in ((0, q_ref), (1, k_ref)):
        rot = _rope_rows(qkv[:, which * D_MODEL:(which + 1) * D_MODEL], c, sa, sb)
        for hd in range(N_HEADS):
            ref[:, hd * LANES:(hd + 1) * LANES] = rot[hd]
    v_ref[...] = qkv[:, 2 * D_MODEL:]


def _qkv_layer(st, x, mods, ng, w_qkv, tables, table_spec):
    return pl.pallas_call(
        _qkv_kernel,
        grid=(st.grid,),
        in_specs=[st.rows_spec(), st.mod_spec(0), st.mod_spec(1), _full_spec((1, D_MODEL)),
                  _full_spec((D_MODEL, 3 * D_MODEL)), table_spec, table_spec, table_spec],
        out_specs=[st.rows_spec(), st.rows_spec(), st.rows_spec()],
        out_shape=[jax.ShapeDtypeStruct((st.rows, D_MODEL), F32)] * 3,
        compiler_params=_params("arbitrary"),
        name="qkv_rope",
    )(x, mods, mods, ng, w_qkv, *tables)


def _qkv_t_kernel(x_ref, sh_ref, sc_ref, ng_ref, wq_ref, wkt_ref, wv_ref, c_ref, sa_ref, sb_ref,
                  ct_ref, st_ref, q_ref, kt_ref, ktb_ref, v_ref, vb_ref):
    h = _modulate(x_ref[...], ng_ref[...], sh_ref[...], sc_ref[...]).astype(BF16)
    rot = _rope_rows(_bdot(h, wq_ref[...]), c_ref[...], sa_ref[...], sb_ref[...])
    for hd in range(N_HEADS):
        q_ref[:, hd * LANES:(hd + 1) * LANES] = (rot[hd] * (HEAD_DIM ** -0.5)).astype(q_ref.dtype)
    kt = lax.dot_general(wkt_ref[...], h, (((1,), (1,)), ((), ())), preferred_element_type=F32)
    ct, st = ct_ref[...], st_ref[...]
    half = ROT_DIM // 2
    for blk in range(2 * N_HEADS):
        base = blk * HEAD_DIM
        x1 = kt[base:base + half]
        x2 = kt[base + half:base + ROT_DIM]
        full = jnp.concatenate([x1 * ct - x2 * st, x2 * ct + x1 * st, kt[base + ROT_DIM:base + HEAD_DIM]], axis=0)
        kt_ref[base:base + HEAD_DIM, :] = full
        ktb_ref[base:base + HEAD_DIM, :] = full.astype(ktb_ref.dtype)
    v = _bdot(h, wv_ref[...])
    v_ref[...] = v
    vb_ref[...] = v.astype(vb_ref.dtype)


def _qkv_t_layer(st, x, mods, ng, wq, wkt, wv, tables, angles_t):
    tpb = st.tiles_per_batch
    tm = st.tile
    tab = pl.BlockSpec((tm, LANES), lambda t: (t % tpb, 0))
    tab_t = pl.BlockSpec((ROT_DIM // 2, tm), lambda t: (0, t % tpb))
    w = _full_spec((D_MODEL, D_MODEL))
    return pl.pallas_call(
        _qkv_t_kernel,
        grid=(st.grid,),
        in_specs=[st.rows_spec(), st.mod_spec(0), st.mod_spec(1), _full_spec((1, D_MODEL)), w, w, w,
                  tab, tab, tab, tab_t, tab_t],
        out_specs=[st.rows_spec(),
                   pl.BlockSpec((None, D_MODEL, tm), lambda t: (t // tpb, 0, t % tpb)),
                   pl.BlockSpec((None, D_MODEL, tm), lambda t: (t // tpb, 0, t % tpb)),
                   st.rows_spec(), st.rows_spec()],
        out_shape=[jax.ShapeDtypeStruct((st.rows, D_MODEL), BF16),
                   jax.ShapeDtypeStruct((BATCH, D_MODEL, SEQ), F32),
                   jax.ShapeDtypeStruct((BATCH, D_MODEL, SEQ), BF16),
                   jax.ShapeDtypeStruct((st.rows, D_MODEL), F32),
                   jax.ShapeDtypeStruct((st.rows, D_MODEL), BF16)],
        compiler_params=_params("arbitrary"),
        name="qkv_rope_prompt",
    )(x, mods, mods, ng, wq, wkt, wv, *tables, *angles_t)


def _lambda(lq1_ref, lk1_ref, lq2_ref, lk2_ref, lambda_init):
    a = jnp.sum(lq1_ref[...] * lk1_ref[...], axis=-1, keepdims=True)
    b = jnp.sum(lq2_ref[...] * lk2_ref[...], axis=-1, keepdims=True)
    return jnp.exp(a) - jnp.exp(b) + lambda_init


def _attn_kernel(q_ref, kt_ref, v_ref, lq1_ref, lk1_ref, lq2_ref, lk2_ref, sg_ref, o_ref, *, lambda_init):
    tq = q_ref.shape[0]
    qi = pl.program_id(2)
    q = q_ref[...]
    first = lax.broadcasted_iota(jnp.int32, (1, LANES), 1) < HEAD_DIM
    zero = jnp.zeros_like(q)
    qs = (jnp.where(first, q, zero), jnp.where(first, zero, q))
    lam = _lambda(lq1_ref, lk1_ref, lq2_ref, lk2_ref, lambda_init)

    def query_tile(n):
        tv = (n + 1) * tq
        past = n * tq
        visible = (lax.broadcasted_iota(jnp.int32, (tq, tq), 1)
                   <= lax.broadcasted_iota(jnp.int32, (tq, tq), 0))
        es, totals = [], []
        for comp in range(2):
            s = _bdot(qs[comp], kt_ref[:, :tv])
            s_diag = jnp.where(visible, s[:, past:], -jnp.inf)
            m = jnp.max(s_diag, axis=-1, keepdims=True)
            if n:
                m = jnp.maximum(m, jnp.max(s[:, :past], axis=-1, keepdims=True))
            e = jnp.exp(s_diag - m)
            total = jnp.sum(e, axis=-1, keepdims=True)
            if n:
                e_past = jnp.exp(s[:, :past] - m)
                total = total + jnp.sum(e_past, axis=-1, keepdims=True)
                e = jnp.concatenate([e_past, e], axis=1)
            es.append(e)
            totals.append(total)
        a = (es[0] * (1.0 / totals[0]) - es[1] * (lam / totals[1])).astype(BF16)
        o = _bdot(a, v_ref[:tv, :])
        o_ref[...] = (_rms(o) * sg_ref[...] * (1.0 - lambda_init)).astype(o_ref.dtype)

    for n in range(SEQ // tq):
        pl.when(qi == n)(functools.partial(query_tile, n))


def _attn_prompt(q, ktb, vb, lq1, lk1, lq2, lk2, subln_g, lambda_init):
    nq = SEQ // ATTN_Q_TILE
    kern = functools.partial(_attn_kernel, lambda_init=lambda_init)
    vec = pl.BlockSpec((1, HEAD_DIM), lambda b, h, i: (0, 0))
    return pl.pallas_call(
        kern,
        grid=(BATCH, N_HEADS, nq),
        in_specs=[pl.BlockSpec((ATTN_Q_TILE, LANES), lambda b, h, i: (b * nq + i, h)),
                  pl.BlockSpec((None, LANES, SEQ), lambda b, h, i: (b, h, 0)),
                  pl.BlockSpec((SEQ, LANES), lambda b, h, i: (b, h)),
                  vec, vec, vec, vec,
                  pl.BlockSpec((1, 2 * HEAD_DIM), lambda b, h, i: (0, 0))],
        out_specs=pl.BlockSpec((ATTN_Q_TILE, LANES), lambda b, h, i: (b * nq + i, h)),
        out_shape=jax.ShapeDtypeStruct((N_PROMPT, D_MODEL), BF16),
        compiler_params=_params("arbitrary", "arbitrary", "arbitrary"),
        name="diff_attn_prompt",
    )(q, ktb, vb, lq1, lk1, lq2, lk2, subln_g)


DEC_ROWS = N_HEADS * 2 * DEC_SEQ


def _decode_kernel(pt_ref, q_ref, kn_ref, vn_ref, lq1_ref, lk1_ref, lq2_ref, lk2_ref, sg_ref, *rest,
                   lambda_init):
    k_refs = rest[:PAGES_PER_STEP]
    v_refs = rest[PAGES_PER_STEP:2 * PAGES_PER_STEP]
    o_ref, m_ref, l_ref, acc_ref = rest[2 * PAGES_PER_STEP:]
    step = pl.program_id(1)
    rows_per_head = 2 * DEC_SEQ
    lane = lax.broadcasted_iota(jnp.int32, (rows_per_head, D_MODEL), 1)
    comp = _div_pow2(lax.broadcasted_iota(jnp.int32, (rows_per_head, D_MODEL), 0), DEC_SEQ)
    q8 = q_ref[...] * (HEAD_DIM ** -0.5)
    blk = _div_pow2(lane, HEAD_DIM)
    qbd = jnp.concatenate([jnp.where(blk == 2 * hd + comp, q8, 0.0) for hd in range(N_HEADS)],
                          axis=0).astype(BF16)

    @pl.when(step == 0)
    def _():
        m_ref[...] = jnp.full(m_ref.shape, -jnp.inf, F32)
        l_ref[...] = jnp.zeros(l_ref.shape, F32)
        acc_ref[...] = jnp.zeros(acc_ref.shape, F32)

    m, l, acc = m_ref[...], l_ref[...], acc_ref[...]
    s = jnp.concatenate([_bdot(qbd, kr[...].astype(BF16)) for kr in k_refs], axis=1)
    m_new = jnp.maximum(m, jnp.max(s, axis=-1, keepdims=True))
    alpha = jnp.exp(m - m_new)
    p = jnp.exp(s - m_new)
    l = l * alpha + jnp.sum(p, axis=-1, keepdims=True)
    pv = []
    for hd in range(N_HEADS):
        rows = slice(hd * rows_per_head, (hd + 1) * rows_per_head)
        v_h = jnp.concatenate([vr[pl.ds(hd, PAGE_SIZE, stride=N_HEADS), :].astype(BF16) for vr in v_refs],
                              axis=0)
        pv.append(_bdot(p[rows].astype(BF16), v_h))
    acc = acc * alpha + jnp.concatenate(pv, axis=0)
    m = m_new
    m_ref[...], l_ref[...], acc_ref[...] = m, l, acc

    @pl.when(step == pl.num_programs(1) - 1)
    def _():
        kn = kn_ref[...].astype(BF16).astype(F32)
        vn = vn_ref[...].astype(BF16).astype(F32)
        qf = qbd.astype(F32)
        row_q = _mod_pow2(lax.broadcasted_iota(jnp.int32, (DEC_ROWS, 1), 0), DEC_SEQ)
        s_new = []
        for j in range(DEC_SEQ):
            sj = jnp.sum(qf * kn[j:j + 1, :], axis=-1, keepdims=True)
            s_new.append(jnp.where(row_q >= j, sj, -jnp.inf))
        m2 = m
        for sj in s_new:
            m2 = jnp.maximum(m2, sj)
        alpha = jnp.exp(m - m2)
        l2 = l * alpha
        acc2 = acc * alpha
        for j, sj in enumerate(s_new):
            pj = jnp.exp(sj - m2)
            l2 = l2 + pj
            v_rows = jnp.concatenate(
                [jnp.broadcast_to(vn[j:j + 1, hd * LANES:(hd + 1) * LANES], (rows_per_head, LANES))
                 for hd in range(N_HEADS)], axis=0)
            acc2 = acc2 + pj.astype(BF16).astype(F32) * v_rows
        outn = acc2 / l2
        lam = _lambda(lq1_ref, lk1_ref, lq2_ref, lk2_ref, lambda_init)
        for hd in range(N_HEADS):
            r0 = hd * rows_per_head
            o = outn[r0:r0 + DEC_SEQ] - lam * outn[r0 + DEC_SEQ:r0 + rows_per_head]
            o_ref[:, hd * LANES:(hd + 1) * LANES] = _rms(o) * sg_ref[...] * (1.0 - lambda_init)


def _attn_decode(page_table, q8, k_new, v_new, cache_kt, cache_v, layer, lq1, lk1, lq2, lk2, subln_g, lambda_init):
    n_steps = N_PAGES // PAGES_PER_STEP
    kern = functools.partial(_decode_kernel, lambda_init=lambda_init)
    tok = pl.BlockSpec((None, DEC_SEQ, D_MODEL), lambda b, p, pt: (b, 0, 0))
    vec = pl.BlockSpec((1, HEAD_DIM), lambda b, p, pt: (0, 0))

    def page_spec(i):
        return pl.BlockSpec((None, None, N_HEADS * PAGE_SIZE, LANES),
                            lambda b, p, pt: (pt[b * N_PAGES + p * PAGES_PER_STEP + i], layer, 0, 0))

    grid_spec = pltpu.PrefetchScalarGridSpec(
        num_scalar_prefetch=1,
        grid=(DEC_BATCH, n_steps),
        in_specs=[pl.BlockSpec((None, 2 * DEC_SEQ, D_MODEL), lambda b, p, pt: (b, 0, 0)), tok, tok,
                  vec, vec, vec, vec, pl.BlockSpec((1, 2 * HEAD_DIM), lambda b, p, pt: (0, 0))]
                 + [page_spec(i) for i in range(PAGES_PER_STEP)] * 2,
        out_specs=tok,
        scratch_shapes=[pltpu.VMEM((DEC_ROWS, 1), F32), pltpu.VMEM((DEC_ROWS, 1), F32),
                        pltpu.VMEM((DEC_ROWS, LANES), F32)],
    )
    return pl.pallas_call(
        kern,
        grid_spec=grid_spec,
        out_shape=jax.ShapeDtypeStruct((DEC_BATCH, DEC_SEQ, D_MODEL), F32),
        compiler_params=_params("arbitrary", "arbitrary"),
        name="diff_attn_decode",
    )(page_table.reshape(-1), q8, k_new, v_new, lq1, lk1, lq2, lk2, subln_g,
      *([cache_kt] * PAGES_PER_STEP), *([cache_v] * PAGES_PER_STEP))


def _outproj_kernel(o_ref, x_ref, gt_ref, w_ref, *rest):
    route_in = rest[:N_ROUTE_IN]
    xo_ref, route_ref, gates_ref = rest[N_ROUTE_IN:]
    x_new = x_ref[...] + gt_ref[...] * _bdot(o_ref[...].astype(BF16), w_ref[...])
    xo_ref[...] = x_new
    _route_store(x_new, route_in, route_ref, gates_ref)


def _outproj_layer(st, o, x, mods, w_o, routing):
    return pl.pallas_call(
        _outproj_kernel,
        grid=(st.grid,),
        in_specs=[st.rows_spec(), st.rows_spec(), st.mod_spec(2), _full_spec((D_MODEL, D_MODEL))]
                 + routing.in_specs,
        out_specs=[st.rows_spec()] + routing.out_specs,
        out_shape=[jax.ShapeDtypeStruct((st.rows, D_MODEL), F32)] + routing.out_shape,
        compiler_params=_params("arbitrary"),
        name="attn_out_proj",
    )(o, x, mods, w_o, *routing.args)


N_ALL = N_PROMPT + N_SAMPLE
ROW_W = D_MODEL + LANES
PAIRS_PER_GROUP = EXPERTS_PER_GROUP * (EXPERTS_PER_GROUP - 1) // 2
N_BUCKETS = N_EXPERT_GROUPS * PAIRS_PER_GROUP
SORT_TILE = 256
N_SORT_TILES = (N_ALL + N_BUCKETS * (SORT_TILE - 1)) // SORT_TILE
N_SORT_ROWS = N_SORT_TILES * SORT_TILE
DMA_UNROLL = 8


def _route_store(x, route_in_refs, route_ref, gates_ref):
    sh_ref, sc_ref, ng_ref, rwt_ref, rb_ref = route_in_refs
    tm = x.shape[0]
    h = _modulate(x, ng_ref[...], sh_ref[...], sc_ref[...])
    logits = lax.dot_general(rwt_ref[...], h, (((1,), (1,)), ((), ())), preferred_element_type=F32,
                             precision=lax.Precision.HIGHEST)
    scores = jax.nn.sigmoid(logits)
    sel = scores + rb_ref[...]
    eid = lax.broadcasted_iota(jnp.int32, sel.shape, 0)
    grp = _div_pow2(eid, EXPERTS_PER_GROUP)
    neg = -jnp.inf

    def first_max(vals):
        m = jnp.max(vals, axis=0, keepdims=True)
        idx = jnp.min(jnp.where(vals == m, eid, N_EXPERTS), axis=0, keepdims=True)
        return m, idx

    best = None
    for g in range(N_EXPERT_GROUPS):
        vg = jnp.where(grp == g, sel, neg)
        m1, i1 = first_max(vg)
        m2, _ = first_max(jnp.where(eid == i1, neg, vg))
        gs = m1 + m2
        if best is None:
            best, best_score = jnp.zeros_like(i1), gs
        else:
            better = gs > best_score
            best = jnp.where(better, g, best)
            best_score = jnp.where(better, gs, best_score)
    masked = jnp.where(grp == best, sel, neg)
    _, ia = first_max(masked)
    _, ib = first_max(jnp.where(eid == ia, neg, masked))
    ga = jnp.sum(jnp.where(eid == ia, scores, 0.0), axis=0, keepdims=True)
    gb = jnp.sum(jnp.where(eid == ib, scores, 0.0), axis=0, keepdims=True)
    den = ga + gb
    a_first = ia < ib
    lo = jnp.where(a_first, ia, ib).astype(F32)
    hi = jnp.where(a_first, ib, ia).astype(F32)
    g_lo = jnp.where(a_first, ga, gb) / den
    g_hi = jnp.where(a_first, gb, ga) / den
    r8 = lax.broadcasted_iota(jnp.int32, route_ref.shape, 0)
    route_ref[...] = jnp.where(r8 == 0, lo, jnp.where(r8 == 1, hi, 0.0))
    r128 = lax.broadcasted_iota(jnp.int32, (LANES, tm), 0)
    gates_t = jnp.where(r128 == 0, g_lo, jnp.where(r128 == 1, g_hi, 0.0))
    gates_ref[...] = gates_t.T


ROUTE_ROWS = 8


class _Routing:
    def __init__(self, st, mods, ng, router_wt, router_b):
        self.args = [mods, mods, ng, router_wt, router_b]
        self.in_specs = [st.mod_spec(3), st.mod_spec(4), _full_spec((1, D_MODEL)),
                         _full_spec((N_EXPERTS, D_MODEL)), _full_spec((N_EXPERTS, 1))]
        self.out_specs = [pl.BlockSpec((ROUTE_ROWS, st.tile), lambda i: (0, i)), st.rows_spec(LANES)]
        self.out_shape = [jax.ShapeDtypeStruct((ROUTE_ROWS, st.rows), F32),
                          jax.ShapeDtypeStruct((st.rows, LANES), F32)]


N_ROUTE_IN = 5


def _router_kernel(x_ref, *refs):
    _route_store(x_ref[...], refs[:N_ROUTE_IN], *refs[N_ROUTE_IN:])


def _router_layer(st, x, routing):
    return pl.pallas_call(
        _router_kernel,
        grid=(st.grid,),
        in_specs=[st.rows_spec()] + routing.in_specs,
        out_specs=routing.out_specs,
        out_shape=routing.out_shape,
        compiler_params=_params("arbitrary"),
        name="moe_router",
    )(x, *routing.args)


def _moe_plan(route):
    lo = route[0].astype(jnp.int32)
    hi = route[1].astype(jnp.int32)
    i, j = lo % EXPERTS_PER_GROUP, hi % EXPERTS_PER_GROUP
    pair = i * (2 * EXPERTS_PER_GROUP - 1 - i) // 2 + (j - i - 1)
    bucket = (lo // EXPERTS_PER_GROUP) * PAIRS_PER_GROUP + pair
    onehot = (bucket[:, None] == jnp.arange(N_BUCKETS)[None, :]).astype(jnp.int32)
    csum = jnp.cumsum(onehot, axis=0)
    counts = csum[-1]
    rank = jnp.sum(onehot * csum, axis=1) - 1
    tiles = (counts + SORT_TILE - 1) // SORT_TILE
    ends = jnp.cumsum(tiles)
    starts = ends - tiles
    n_used = ends[-1]
    pos = jnp.sum(onehot * starts[None, :], axis=1) * SORT_TILE + rank
    tile_id = jnp.arange(N_SORT_TILES)
    tile_bucket = jnp.sum((jnp.minimum(tile_id, n_used - 1)[:, None] >= ends[None, :]).astype(jnp.int32), axis=1)
    tile_bucket = jnp.minimum(tile_bucket, N_BUCKETS - 1)
    pair_lo = jnp.array([a for a in range(EXPERTS_PER_GROUP) for _ in range(a + 1, EXPERTS_PER_GROUP)], jnp.int32)
    pair_hi = jnp.array([b for a in range(EXPERTS_PER_GROUP) for b in range(a + 1, EXPERTS_PER_GROUP)], jnp.int32)
    base = (tile_bucket // PAIRS_PER_GROUP) * EXPERTS_PER_GROUP
    tile_lo = base + pair_lo[tile_bucket % PAIRS_PER_GROUP]
    tile_hi = base + pair_hi[tile_bucket % PAIRS_PER_GROUP]
    return pos.astype(jnp.int32), tile_lo, tile_hi, n_used.reshape(1).astype(jnp.int32)


def _bucket_kernel(pos_ref, x_ref, sh_ref, sc_ref, ng_ref, gates_ref, dst_in_ref, dst_ref, buf, sem, *, row0):
    del dst_in_ref
    tm = x_ref.shape[0]
    t = pl.program_id(0)
    nt = pl.num_programs(0)
    slot = t % 2

    def wait(s):
        pltpu.make_async_copy(buf.at[s], dst_ref.at[pl.ds(0, tm), :], sem.at[s]).wait()

    @pl.when(t >= 2)
    def _():
        wait(slot)

    buf[slot, :, :D_MODEL] = _modulate(x_ref[...], ng_ref[...], sh_ref[...], sc_ref[...])
    buf[slot, :, D_MODEL:] = gates_ref[...]
    base = row0 + t * tm

    def body(i, carry):
        for u in range(DMA_UNROLL):
            r = i * DMA_UNROLL + u
            pltpu.make_async_copy(buf.at[slot, pl.ds(r, 1), :],
                                  dst_ref.at[pl.ds(pos_ref[base + r], 1), :], sem.at[slot]).start()
        return carry

    lax.fori_loop(0, tm // DMA_UNROLL, body, 0)

    @pl.when(t == nt - 1)
    def _():
        wait(slot)

    @pl.when((t == nt - 1) & (nt > 1))
    def _():
        wait(1 - slot)


def _bucket_rows(st, pos, x, mods, ng, gates, dst, row0):
    kern = functools.partial(_bucket_kernel, row0=row0)
    if st.per_row_mods:
        def mod_spec(k):
            return pl.BlockSpec((st.tile, D_MODEL), lambda i, p: (i, k))
    else:
        tpb = st.tiles_per_batch

        def mod_spec(k):
            return pl.BlockSpec((None, None, 1, D_MODEL), lambda i, p: (i // tpb, k, 0, 0))
    hbm = pl.BlockSpec(memory_space=pl.ANY)
    grid_spec = pltpu.PrefetchScalarGridSpec(
        num_scalar_prefetch=1,
        grid=(st.grid,),
        in_specs=[pl.BlockSpec((st.tile, D_MODEL), lambda i, p: (i, 0)), mod_spec(3), mod_spec(4),
                  pl.BlockSpec((1, D_MODEL), lambda i, p: (0, 0)),
                  pl.BlockSpec((st.tile, LANES), lambda i, p: (i, 0)), hbm],
        out_specs=hbm,
        scratch_shapes=[pltpu.VMEM((2, st.tile, ROW_W), F32), pltpu.SemaphoreType.DMA((2,))],
    )
    return pl.pallas_call(
        kern,
        grid_spec=grid_spec,
        out_shape=jax.ShapeDtypeStruct(dst.shape, dst.dtype),
        input_output_aliases={6: 0},
        compiler_params=_params("arbitrary"),
        name="moe_bucket_rows",
    )(pos, x, mods, mods, ng, gates, dst)


def _moe_kernel(lo_ref, hi_ref, nu_ref, xs_ref, wg_lo, wu_lo, wd_lo, wg_hi, wu_hi, wd_hi, ys_ref):
    del lo_ref, hi_ref
    t = pl.program_id(0)

    @pl.when(t < nu_ref[0])
    def _():
        row = xs_ref[...]
        x = row[:, :D_MODEL].astype(BF16)

        def expert(wg, wu, wd, gate):
            g = _bdot(x, wg[...])
            u = _bdot(x, wu[...])
            return _bdot((jax.nn.silu(g) * u * gate).astype(BF16), wd[...])

        ys_ref[...] = (expert(wg_lo, wu_lo, wd_lo, row[:, D_MODEL:D_MODEL + 1])
                       + expert(wg_hi, wu_hi, wd_hi, row[:, D_MODEL + 1:D_MODEL + 2]))

    @pl.when(t >= nu_ref[0])
    def _():
        ys_ref[...] = jnp.zeros(ys_ref.shape, ys_ref.dtype)


def _moe_experts(xs, tile_lo, tile_hi, n_used, wg, wu, wd, layer):
    def w_spec(shape, which):
        return pl.BlockSpec((None, None) + shape,
                            lambda t, lo, hi, nu: (layer, (lo, hi)[which][t], 0, 0))

    up, down = (D_MODEL, D_EXPERT), (D_EXPERT, D_MODEL)
    grid_spec = pltpu.PrefetchScalarGridSpec(
        num_scalar_prefetch=3,
        grid=(N_SORT_TILES,),
        in_specs=[pl.BlockSpec((SORT_TILE, ROW_W), lambda t, lo, hi, nu: (t, 0)),
                  w_spec(up, 0), w_spec(up, 0), w_spec(down, 0),
                  w_spec(up, 1), w_spec(up, 1), w_spec(down, 1)],
        out_specs=pl.BlockSpec((SORT_TILE, D_MODEL), lambda t, lo, hi, nu: (t, 0)),
    )
    return pl.pallas_call(
        _moe_kernel,
        grid_spec=grid_spec,
        out_shape=jax.ShapeDtypeStruct((N_SORT_ROWS, D_MODEL), F32),
        compiler_params=_params("arbitrary"),
        name="moe_experts",
    )(tile_lo, tile_hi, n_used, xs, wg, wu, wd, wg, wu, wd)


def _unsort_kernel(pos_ref, ys_ref, x_ref, g2_ref, fg_ref, o_ref, buf, sem, *, row0, final):
    tm = x_ref.shape[0]
    t = pl.program_id(0)

    def fetch(tile, slot):
        base = row0 + tile * tm

        def body(i, carry):
            for u in range(DMA_UNROLL):
                r = i * DMA_UNROLL + u
                pltpu.make_async_copy(ys_ref.at[pl.ds(pos_ref[base + r], 1), :],
                                      buf.at[slot, pl.ds(r, 1), :], sem.at[slot]).start()
            return carry

        lax.fori_loop(0, tm // DMA_UNROLL, body, 0)

    @pl.when(t == 0)
    def _():
        fetch(0, 0)

    @pl.when(t + 1 < pl.num_programs(0))
    def _():
        fetch(t + 1, (t + 1) % 2)

    slot = t % 2
    pltpu.make_async_copy(ys_ref.at[pl.ds(0, tm), :], buf.at[slot], sem.at[slot]).wait()
    y = x_ref[...] + g2_ref[...] * buf[slot]
    o_ref[...] = _rms(y) * fg_ref[...] if final else y


def _unsort_layer(st, pos, ys, x, mods, row0, final_g, final):
    kern = functools.partial(_unsort_kernel, row0=row0, final=final)
    if st.per_row_mods:
        g2_spec = pl.BlockSpec((st.tile, D_MODEL), lambda i, p: (i, 5))
    else:
        tpb = st.tiles_per_batch
        g2_spec = pl.BlockSpec((None, None, 1, D_MODEL), lambda i, p: (i // tpb, 5, 0, 0))
    rows = pl.BlockSpec((st.tile, D_MODEL), lambda i, p: (i, 0))
    grid_spec = pltpu.PrefetchScalarGridSpec(
        num_scalar_prefetch=1,
        grid=(st.grid,),
        in_specs=[pl.BlockSpec(memory_space=pl.ANY), rows, g2_spec,
                  pl.BlockSpec((1, D_MODEL), lambda i, p: (0, 0))],
        out_specs=rows,
        scratch_shapes=[pltpu.VMEM((2, st.tile, D_MODEL), F32), pltpu.SemaphoreType.DMA((2,))],
    )
    return pl.pallas_call(
        kern,
        grid_spec=grid_spec,
        out_shape=jax.ShapeDtypeStruct((st.rows, D_MODEL), F32),
        compiler_params=_params("arbitrary"),
        name="moe_unsort_residual",
    )(pos, ys, x, mods, final_g)


def kernel(x_prompt, x_sample, cache_k, cache_v, state_pool, page_table, c_prompt, c_sample,
           norm1_g, norm2_g, ada_w, ada_b, a_w_in, a_ln_g, a_ln_b, a_w_s, a_b_s, a_w_out,
           b_w_grp, b_scale, c_w_qkv, c_lq1, c_lk1, c_lq2, c_lk2, c_subln_g, c_w_o,
           router_w, router_bias, e_w_gate, e_w_up, e_w_down, final_g):
    sp = _Stream(N_PROMPT, PROMPT_TILE, SEQ // PROMPT_TILE, per_row_mods=False)
    ss = _Stream(N_SAMPLE, N_SAMPLE, 1, per_row_mods=True)

    xp = x_prompt.reshape(N_PROMPT, D_MODEL)
    xs = x_sample.reshape(N_SAMPLE, D_MODEL)
    mods_all = _ada(jnp.concatenate([c_prompt, c_sample], axis=0), ada_w, ada_b)

    wg_b, wu_b, wd_b = e_w_gate.astype(BF16), e_w_up.astype(BF16), e_w_down.astype(BF16)
    rwt = router_w.T
    rb = router_bias.reshape(N_EXPERTS, 1)
    fg = final_g.reshape(1, D_MODEL)
    bucketed = jnp.zeros((N_SORT_ROWS, ROW_W), F32)
    n_phys = cache_k.shape[0]
    n_c_layers = cache_k.shape[1]
    ckt = jnp.transpose(cache_k, (0, 1, 3, 4, 5, 2)).reshape(n_phys, n_c_layers, D_MODEL, PAGE_SIZE)
    cv = cache_v.reshape(n_phys, n_c_layers, PAGE_SIZE * N_HEADS, 2 * HEAD_DIM)

    av_p, av_s, pr_p, pr_s, kp_l, vp_l, ks_l, vs_l = [], [], [], [], [], [], [], []
    for i in range(DEPTH):
        kind, j = i % N_MIXERS, i // N_MIXERS
        mp = mods_all[i, :BATCH].reshape(BATCH, N_MODS, 1, D_MODEL)
        ms = jnp.repeat(mods_all[i, BATCH:], DEC_SEQ, axis=0)
        ng1 = norm1_g[i].reshape(1, D_MODEL)
        ng2 = norm2_g[i].reshape(1, D_MODEL)
        rt_p = _Routing(sp, mp, ng2, rwt, rb)
        rt_s = _Routing(ss, ms, ng2, rwt, rb)
        if kind == 0:
            w_in = a_w_in[j].astype(BF16)
            w_out = a_w_out[j].astype(BF16)
            ln_g, ln_b = a_ln_g[j].reshape(1, D_GATE), a_ln_b[j].reshape(1, D_GATE)
            bias = jnp.repeat(a_b_s[j].T, D_GATE // A_GROUPS, axis=1)
            xp, vrow_p, route_p, gates_p = _gmlp_layer(sp, xp, mp, ng1, w_in, ln_g, ln_b, a_w_s[j], bias, w_out,
                                                       CHUNK, BATCH, False, rt_p)
            reps = CHUNK // DEC_SEQ
            wmix_s = jnp.tile(a_w_s[j][:, :DEC_SEQ, :DEC_SEQ], (1, reps, reps))
            bias_s = jnp.tile(bias[:DEC_SEQ], (reps, 1))
            xs, vrow_s, route_s, gates_s = _gmlp_layer(ss, xs, ms, ng1, a_w_in[j], ln_g, ln_b, wmix_s, bias_s,
                                                       a_w_out[j], DEC_SEQ, 1, True, rt_s)
            av_p.append(vrow_p.reshape(BATCH, CHUNK, D_GATE))
            av_s.append(vrow_s.reshape(DEC_BATCH, DEC_SEQ, D_GATE))
        elif kind == 1:
            w_grp = b_w_grp[j].astype(BF16)
            scale = b_scale[j].reshape(1, D_MODEL)
            hb = PROMPT_TILE // POOL_HALO
            halo_spec = pl.BlockSpec((POOL_HALO, D_MODEL), lambda t: (jnp.maximum(t * hb - 1, 0), 0))
            xp, rows_p, route_p, gates_p = _pool_layer(sp, xp, xp, halo_spec, mp, ng1, w_grp, scale, False, 0,
                                                       POOL_HALO, BATCH, rt_p)
            pr_p.append(rows_p.reshape(BATCH, POOL_HALO, D_MODEL)[:, 1:])
            pad_t = 2 * DEC_SEQ
            s8 = _Stream(DEC_BATCH * pad_t, pad_t, 1, per_row_mods=True)
            xs8 = jnp.pad(xs.reshape(DEC_BATCH, DEC_SEQ, D_MODEL), ((0, 0), (0, DEC_SEQ), (0, 0)))
            ms8 = jnp.repeat(mods_all[i, BATCH:], pad_t, axis=0)
            prev = jnp.pad(state_pool[:, j], ((0, 0), (1, 0), (0, 0))).reshape(DEC_BATCH * POOL_HALO, D_MODEL)
            prev_spec = pl.BlockSpec((POOL_HALO, D_MODEL), lambda t: (t, 0))
            n_out = POOL_HALO + pad_t
            xs8, rows_s = _pool_layer(s8, xs8.reshape(-1, D_MODEL), prev, prev_spec, ms8, ng1, w_grp, scale,
                                      True, PAST_LEN, n_out, DEC_BATCH)
            xs = xs8.reshape(DEC_BATCH, pad_t, D_MODEL)[:, :DEC_SEQ].reshape(N_SAMPLE, D_MODEL)
            route_s, gates_s = _router_layer(ss, xs, rt_s)
            lo = 1 + DEC_SEQ
            pr_s.append(rows_s.reshape(DEC_BATCH, n_out, D_MODEL)[:, lo:lo + POOL_BUF])
        else:
            lambda_init = 0.8 - 0.6 * math.exp(-0.3 * i)
            w_qkv = c_w_qkv[j].astype(BF16)
            w_o = c_w_o[j].astype(BF16)
            lq1, lk1 = c_lq1[j].reshape(1, HEAD_DIM), c_lk1[j].reshape(1, HEAD_DIM)
            lq2, lk2 = c_lq2[j].reshape(1, HEAD_DIM), c_lk2[j].reshape(1, HEAD_DIM)
            sg = c_subln_g[j].reshape(1, 2 * HEAD_DIM)
            pos_p = jnp.arange(SEQ)
            cos_p, sin_p = _rope_angles(pos_p)
            qp, kt, ktb, vp, vpb = _qkv_t_layer(
                sp, xp, mp, ng1, w_qkv[:, :D_MODEL], w_qkv[:, D_MODEL:2 * D_MODEL].T, w_qkv[:, 2 * D_MODEL:],
                _rope_tables(pos_p), (cos_p.T, sin_p.T))
            op = _attn_prompt(qp, ktb, vpb, lq1, lk1, lq2, lk2, sg, lambda_init)
            xp, route_p, gates_p = _outproj_layer(sp, op, xp, mp, w_o, rt_p)
            tab_s = _rope_tables(jnp.tile(PAST_LEN + jnp.arange(DEC_SEQ), DEC_BATCH))
            qs, ks, vs = _qkv_layer(ss, xs, ms, ng1, w_qkv, tab_s,
                                    pl.BlockSpec((N_SAMPLE, LANES), lambda t: (0, 0)))
            tok = (DEC_BATCH, DEC_SEQ, D_MODEL)
            q8 = jnp.concatenate([qs.reshape(tok)] * 2, axis=1)
            os_ = _attn_decode(page_table, q8, ks.reshape(tok), vs.reshape(tok), ckt, cv, j,
                               lq1, lk1, lq2, lk2, sg, lambda_init)
            xs, route_s, gates_s = _outproj_layer(ss, os_.reshape(N_SAMPLE, D_MODEL), xs, ms, w_o, rt_s)
            kp = jnp.transpose(kt.reshape(BATCH, N_HEADS, 2, HEAD_DIM, SEQ), (0, 4, 1, 2, 3))
            kp_l.append(kp)
            vp_l.append(vp.reshape(BATCH, SEQ, N_HEADS, 2 * HEAD_DIM))
            ks_l.append(ks.reshape(DEC_BATCH, DEC_SEQ, N_HEADS, 2, HEAD_DIM))
            vs_l.append(vs.reshape(DEC_BATCH, DEC_SEQ, N_HEADS, 2 * HEAD_DIM))

        pos, tile_lo, tile_hi, n_used = _moe_plan(jnp.concatenate([route_p, route_s], axis=1))
        bucketed = _bucket_rows(sp, pos, xp, mp, ng2, gates_p, bucketed, 0)
        bucketed = _bucket_rows(ss, pos, xs, ms, ng2, gates_s, bucketed, N_PROMPT)
        ys = _moe_experts(bucketed, tile_lo, tile_hi, n_used, wg_b, wu_b, wd_b, i)
        last = i == DEPTH - 1
        xp = _unsort_layer(sp, pos, ys, xp, mp, 0, fg, last)
        xs = _unsort_layer(ss, pos, ys, xs, ms, N_PROMPT, fg, last)

    y_prompt = xp.reshape(BATCH, SEQ, D_MODEL)
    y_sample = xs.reshape(DEC_BATCH, DEC_SEQ, D_MODEL)
    return (y_prompt, y_sample,
            jnp.stack(av_p, axis=1), jnp.stack(av_s, axis=1),
            jnp.stack(pr_p, axis=1), jnp.stack(pr_s, axis=1),
            jnp.stack(kp_l, axis=1), jnp.stack(vp_l, axis=1),
            jnp.stack(ks_l, axis=1), jnp.stack(vs_l, axis=1))
```

```python
import functools
import math

import jax
import jax.numpy as jnp
from jax import lax
from jax.experimental import pallas as pl
from jax.experimental.pallas import tpu as pltpu

F32 = jnp.float32
BF16 = jnp.bfloat16

D_MODEL = 1024
BATCH = 8
SEQ = 2048
DEPTH = 4
DEC_BATCH = 32
DEC_SEQ = 4
PAST_LEN = 8192
PAGE_SIZE = 128
N_PAGES = PAST_LEN // PAGE_SIZE
N_MIXERS = 3
CHUNK = 128
D_GATE = D_MODEL
A_GROUPS = 4
POOL_WINDOWS = (2, 4, 8, 16)
B_GROUPS = len(POOL_WINDOWS)
POOL_BUF = max(POOL_WINDOWS) - 1
POOL_HALO = POOL_BUF + 1
N_HEADS = 8
HEAD_DIM = D_MODEL // N_HEADS // 2
ROT_DIM = HEAD_DIM // 4
ROPE_THETA = 500000.0
N_EXPERTS = 16
N_EXPERT_GROUPS = 4
EXPERTS_PER_GROUP = N_EXPERTS // N_EXPERT_GROUPS
D_EXPERT = 512
EPS = 1e-6
N_MODS = 6

N_PROMPT = BATCH * SEQ
N_SAMPLE = DEC_BATCH * DEC_SEQ
LANES = 128
VMEM_LIMIT = 56 * 1024 * 1024

PROMPT_TILE = 256
ROUTER_TILE = 512
ATTN_Q_TILE = 256
PAGES_PER_STEP = 16


def _params(*sem):
    return pltpu.CompilerParams(dimension_semantics=sem, vmem_limit_bytes=VMEM_LIMIT)


class _Stream:
    def __init__(self, rows, tile, tiles_per_batch, per_row_mods):
        self.rows = rows
        self.tile = tile
        self.tiles_per_batch = tiles_per_batch
        self.per_row_mods = per_row_mods
        self.grid = rows // tile

    def rows_spec(self, width=D_MODEL):
        return pl.BlockSpec((self.tile, width), lambda i: (i, 0))

    def mod_spec(self, k):
        if self.per_row_mods:
            return pl.BlockSpec((self.tile, D_MODEL), lambda i: (i, k))
        tpb = self.tiles_per_batch
        return pl.BlockSpec((None, None, 1, D_MODEL), lambda i: (i // tpb, k, 0, 0))


def _full_spec(shape):
    nd = len(shape)
    return pl.BlockSpec(shape, lambda i: (0,) * nd)


def _rms(x):
    return x * lax.rsqrt(jnp.mean(x * x, axis=-1, keepdims=True) + EPS)


def _modulate(x, g, sh, sc):
    return _rms(x) * g * (1.0 + sc) + sh


def _bdot(a, b):
    return jnp.dot(a, b, preferred_element_type=F32)


def _div_pow2(x, n):
    assert n & (n - 1) == 0
    return lax.shift_right_logical(x, n.bit_length() - 1)


def _mod_pow2(x, n):
    assert n & (n - 1) == 0
    return x & (n - 1)


def _ada_kernel(c_ref, w_ref, b_ref, o_ref):
    o_ref[...] = jnp.dot(jax.nn.silu(c_ref[...]), w_ref[...], preferred_element_type=F32,
                         precision=lax.Precision.HIGHEST) + b_ref[...]


def _ada(c_all, ada_w, ada_b):
    nb = c_all.shape[0]
    tn = 1536
    return pl.pallas_call(
        _ada_kernel,
        grid=(DEPTH, N_MODS * D_MODEL // tn),
        in_specs=[pl.BlockSpec((nb, D_MODEL), lambda l, n: (0, 0)),
                  pl.BlockSpec((None, D_MODEL, tn), lambda l, n: (l, 0, n)),
                  pl.BlockSpec((None, 1, tn), lambda l, n: (l, 0, n))],
        out_specs=pl.BlockSpec((None, nb, tn), lambda l, n: (l, 0, n)),
        out_shape=jax.ShapeDtypeStruct((DEPTH, nb, N_MODS * D_MODEL), F32),
        compiler_params=_params("arbitrary", "arbitrary"),
        name="ada_mod",
    )(c_all, ada_w, ada_b.reshape(DEPTH, 1, N_MODS * D_MODEL))


def _mm(a, b, precise):
    if precise:
        return jnp.dot(a.astype(F32), b.astype(F32), preferred_element_type=F32, precision=lax.Precision.HIGHEST)
    return _bdot(a.astype(BF16), b.astype(BF16))


def _gmlp_kernel(x_ref, sh_ref, sc_ref, gt_ref, ng_ref, win_ref, lng_ref, lnb_ref, wmix_ref,
                 bmix_ref, wout_ref, xo_ref, v_ref, mix_ref, *, period, precise):
    tm = x_ref.shape[0]
    x = x_ref[...]
    h = _modulate(x, ng_ref[...], sh_ref[...], sc_ref[...])
    z = _mm(h, win_ref[...], precise)
    z = 0.5 * z * (1.0 + lax.erf(z * (2.0 ** -0.5)))
    u = z[:, :D_GATE]
    v = z[:, D_GATE:]
    mu = jnp.mean(v, axis=-1, keepdims=True)
    vc = v - mu
    var = jnp.mean(vc * vc, axis=-1, keepdims=True)
    v = vc * lax.rsqrt(var + EPS) * lng_ref[...] + lnb_ref[...]
    v_ref[...] = v[tm - CHUNK:, :]
    vb = v if precise else v.astype(BF16)
    t = lax.broadcasted_iota(jnp.int32, (CHUNK, CHUNK), 0)
    s = lax.broadcasted_iota(jnp.int32, (CHUNK, CHUNK), 1)
    keep = (s <= t) & (_div_pow2(t, period) == _div_pow2(s, period))
    cg = D_GATE // A_GROUPS
    for g in range(A_GROUPS):
        wm = jnp.where(keep, wmix_ref[g], 0.0)
        for c in range(tm // CHUNK):
            rows = slice(c * CHUNK, (c + 1) * CHUNK)
            cols = slice(g * cg, (g + 1) * cg)
            mix_ref[rows, cols] = _mm(wm, vb[rows, cols], precise) + bmix_ref[:, cols]
    y = _mm(u * mix_ref[...], wout_ref[...], precise)
    xo_ref[...] = x + gt_ref[...] * y


def _gmlp_layer(st, x, mods, ng, w_in, ln_g, ln_b, wmix, bmix, w_out, period, n_vblocks, precise):
    kern = functools.partial(_gmlp_kernel, period=period, precise=precise)
    vb_per = st.grid // n_vblocks
    return pl.pallas_call(
        kern,
        grid=(st.grid,),
        in_specs=[st.rows_spec(), st.mod_spec(0), st.mod_spec(1), st.mod_spec(2),
                  _full_spec((1, D_MODEL)), _full_spec((D_MODEL, 2 * D_GATE)),
                  _full_spec((1, D_GATE)), _full_spec((1, D_GATE)),
                  _full_spec((A_GROUPS, CHUNK, CHUNK)), _full_spec((CHUNK, D_GATE)),
                  _full_spec((D_GATE, D_MODEL))],
        out_specs=[st.rows_spec(),
                   pl.BlockSpec((CHUNK, D_GATE), lambda i: (i // vb_per, 0))],
        out_shape=[jax.ShapeDtypeStruct((st.rows, D_MODEL), F32),
                   jax.ShapeDtypeStruct((n_vblocks * CHUNK, D_GATE), F32)],
        scratch_shapes=[pltpu.VMEM((st.tile, D_GATE), F32)],
        compiler_params=_params("arbitrary"),
        name="gmlp_mixer",
    )(x, mods, mods, mods, ng, w_in, ln_g, ln_b, wmix, bmix, w_out)


def _pool_kernel(x_ref, halo_ref, sh_ref, sc_ref, gt_ref, ng_ref, wgrp_ref, scale_ref,
                 xo_ref, rows_ref, *, halo_is_h, tiles_per_batch, pos0, n_rows_out):
    tm = x_ref.shape[0]
    x = x_ref[...]
    ng, sh, sc = ng_ref[...], sh_ref[...], sc_ref[...]
    h = _modulate(x, ng, sh, sc)
    if halo_is_h:
        halo = halo_ref[...]
    else:
        sh_h = sh if sh.shape[0] == 1 else sh[:POOL_HALO]
        sc_h = sc if sc.shape[0] == 1 else sc[:POOL_HALO]
        first = (pl.program_id(0) % tiles_per_batch) == 0
        halo = jnp.where(first, 0.0, _modulate(halo_ref[...], ng, sh_h, sc_h))
    ext = jnp.concatenate([halo, h], axis=0)
    rows_ref[...] = ext[POOL_HALO + tm - n_rows_out:, :]
    t0 = (pl.program_id(0) % tiles_per_batch) * tm
    pos = pos0 + t0 + lax.broadcasted_iota(jnp.int32, (tm, 1), 0)
    cg = D_MODEL // B_GROUPS
    ys = []
    for g, w in enumerate(POOL_WINDOWS):
        cols = slice(g * cg, (g + 1) * cg)
        acc = ext[:, cols]
        span = 1
        while span < w:
            acc = acc + pltpu.roll(acc, span, 0)
            span *= 2
        cnt = jnp.minimum(pos + 1, w).astype(F32)
        pooled = acc[POOL_HALO:, :] / cnt - h[:, cols]
        ys.append(_bdot(pooled.astype(BF16), wgrp_ref[g]))
    y = jnp.concatenate(ys, axis=-1) * scale_ref[...]
    xo_ref[...] = x + gt_ref[...] * y


def _pool_layer(st, x, halo, halo_spec, mods, ng, w_grp, scale, halo_is_h, pos0, n_rows_out, n_batches):
    kern = functools.partial(_pool_kernel, halo_is_h=halo_is_h, tiles_per_batch=st.tiles_per_batch,
                             pos0=pos0, n_rows_out=n_rows_out)
    cg = D_MODEL // B_GROUPS
    tpb = st.tiles_per_batch
    return pl.pallas_call(
        kern,
        grid=(st.grid,),
        in_specs=[st.rows_spec(), halo_spec, st.mod_spec(0), st.mod_spec(1), st.mod_spec(2),
                  _full_spec((1, D_MODEL)), _full_spec((B_GROUPS, cg, cg)), _full_spec((1, D_MODEL))],
        out_specs=[st.rows_spec(),
                   pl.BlockSpec((n_rows_out, D_MODEL), lambda i: (i // tpb, 0))],
        out_shape=[jax.ShapeDtypeStruct((st.rows, D_MODEL), F32),
                   jax.ShapeDtypeStruct((n_batches * n_rows_out, D_MODEL), F32)],
        compiler_params=_params("arbitrary"),
        name="pool_mixer",
    )(x, halo, mods, mods, mods, ng, w_grp, scale)


def _rope_angles(pos):
    half = ROT_DIM // 2
    inv = ROPE_THETA ** (-(jnp.arange(half, dtype=F32) * 2.0) / ROT_DIM)
    ang = pos.astype(F32)[:, None] * inv[None, :]
    return jnp.cos(ang), jnp.sin(ang)


def _rope_tables(pos):
    cos, sin = _rope_angles(pos)
    n = pos.shape[0]
    pad = jnp.zeros((n, HEAD_DIM - ROT_DIM), F32)
    c = jnp.concatenate([cos, cos, pad + 1.0], axis=1)
    sa = jnp.concatenate([-sin, jnp.zeros_like(sin), pad], axis=1)
    sb = jnp.concatenate([jnp.zeros_like(sin), sin, pad], axis=1)
    return tuple(jnp.concatenate([t, t], axis=1) for t in (c, sa, sb))


def _rope_rows(x, c, sa, sb):
    half = ROT_DIM // 2
    out = []
    for hd in range(N_HEADS):
        xh = x[:, hd * LANES:(hd + 1) * LANES]
        out.append(xh * c + pltpu.roll(xh, LANES - half, 1) * sa + pltpu.roll(xh, half, 1) * sb)
    return out


def _qkv_kernel(x_ref, sh_ref, sc_ref, ng_ref, w_ref, c_ref, sa_ref, sb_ref, q_ref, k_ref, v_ref):
    h = _modulate(x_ref[...], ng_ref[...], sh_ref[...], sc_ref[...]).astype(BF16)
    qkv = _bdot(h, w_ref[...])
    c, sa, sb = c_ref[...], sa_ref[...], sb_ref[...]
    for which, ref in ((0, q_ref), (1, k_ref)):
        rot = _rope_rows(qkv[:, which * D_MODEL:(which + 1) * D_MODEL], c, sa, sb)
        for hd in range(N_HEADS):
            ref[:, hd * LANES:(hd + 1) * LANES] = rot[hd]
    v_ref[...] = qkv[:, 2 * D_MODEL:]


def _qkv_layer(st, x, mods, ng, w_qkv, tables, table_spec):
    return pl.pallas_call(
        _qkv_kernel,
        grid=(st.grid,),
        in_specs=[st.rows_spec(), st.mod_spec(0), st.mod_spec(1), _full_spec((1, D_MODEL)),
                  _full_spec((D_MODEL, 3 * D_MODEL)), table_spec, table_spec, table_spec],
        out_specs=[st.rows_spec(), st.rows_spec(), st.rows_spec()],
        out_shape=[jax.ShapeDtypeStruct((st.rows, D_MODEL), F32)] * 3,
        compiler_params=_params("arbitrary"),
        name="qkv_rope",
    )(x, mods, mods, ng, w_qkv, *tables)


def _qkv_t_kernel(x_ref, sh_ref, sc_ref, ng_ref, wq_ref, wkt_ref, wv_ref, c_ref, sa_ref, sb_ref,
                  ct_ref, st_ref, q_ref, kt_ref, ktb_ref, v_ref, vb_ref):
    h = _modulate(x_ref[...], ng_ref[...], sh_ref[...], sc_ref[...]).astype(BF16)
    rot = _rope_rows(_bdot(h, wq_ref[...]), c_ref[...], sa_ref[...], sb_ref[...])
    for hd in range(N_HEADS):
        q_ref[:, hd * LANES:(hd + 1) * LANES] = (rot[hd] * (HEAD_DIM ** -0.5)).astype(q_ref.dtype)
    kt = lax.dot_general(wkt_ref[...], h, (((1,), (1,)), ((), ())), preferred_element_type=F32)
    ct, st = ct_ref[...], st_ref[...]
    half = ROT_DIM // 2
    for blk in range(2 * N_HEADS):
        base = blk * HEAD_DIM
        x1 = kt[base:base + half]
        x2 = kt[base + half:base + ROT_DIM]
        full = jnp.concatenate([x1 * ct - x2 * st, x2 * ct + x1 * st, kt[base + ROT_DIM:base + HEAD_DIM]], axis=0)
        kt_ref[base:base + HEAD_DIM, :] = full
        ktb_ref[base:base + HEAD_DIM, :] = full.astype(ktb_ref.dtype)
    v = _bdot(h, wv_ref[...])
    v_ref[...] = v
    vb_ref[...] = v.astype(vb_ref.dtype)


def _qkv_t_layer(st, x, mods, ng, wq, wkt, wv, tables, angles_t):
    tpb = st.tiles_per_batch
    tm = st.tile
    tab = pl.BlockSpec((tm, LANES), lambda t: (t % tpb, 0))
    tab_t = pl.BlockSpec((ROT_DIM // 2, tm), lambda t: (0, t % tpb))
    w = _full_spec((D_MODEL, D_MODEL))
    return pl.pallas_call(
        _qkv_t_kernel,
        grid=(st.grid,),
        in_specs=[st.rows_spec(), st.mod_spec(0), st.mod_spec(1), _full_spec((1, D_MODEL)), w, w, w,
                  tab, tab, tab, tab_t, tab_t],
        out_specs=[st.rows_spec(),
                   pl.BlockSpec((None, D_MODEL, tm), lambda t: (t // tpb, 0, t % tpb)),
                   pl.BlockSpec((None, D_MODEL, tm), lambda t: (t // tpb, 0, t % tpb)),
                   st.rows_spec(), st.rows_spec()],
        out_shape=[jax.ShapeDtypeStruct((st.rows, D_MODEL), BF16),
                   jax.ShapeDtypeStruct((BATCH, D_MODEL, SEQ), F32),
                   jax.ShapeDtypeStruct((BATCH, D_MODEL, SEQ), BF16),
                   jax.ShapeDtypeStruct((st.rows, D_MODEL), F32),
                   jax.ShapeDtypeStruct((st.rows, D_MODEL), BF16)],
        compiler_params=_params("arbitrary"),
        name="qkv_rope_prompt",
    )(x, mods, mods, ng, wq, wkt, wv, *tables, *angles_t)


def _lambda(lq1_ref, lk1_ref, lq2_ref, lk2_ref, lambda_init):
    a = jnp.sum(lq1_ref[...] * lk1_ref[...], axis=-1, keepdims=True)
    b = jnp.sum(lq2_ref[...] * lk2_ref[...], axis=-1, keepdims=True)
    return jnp.exp(a) - jnp.exp(b) + lambda_init


def _attn_kernel(q_ref, kt_ref, v_ref, lq1_ref, lk1_ref, lq2_ref, lk2_ref, sg_ref, o_ref, *, lambda_init):
    tq = q_ref.shape[0]
    qi = pl.program_id(2)
    q = q_ref[...]
    first = lax.broadcasted_iota(jnp.int32, (1, LANES), 1) < HEAD_DIM
    zero = jnp.zeros_like(q)
    qs = (jnp.where(first, q, zero), jnp.where(first, zero, q))
    lam = _lambda(lq1_ref, lk1_ref, lq2_ref, lk2_ref, lambda_init)

    def query_tile(n):
        tv = (n + 1) * tq
        past = n * tq
        visible = (lax.broadcasted_iota(jnp.int32, (tq, tq), 1)
                   <= lax.broadcasted_iota(jnp.int32, (tq, tq), 0))
        es, totals = [], []
        for comp in range(2):
            s = _bdot(qs[comp], kt_ref[:, :tv])
            s_diag = jnp.where(visible, s[:, past:], -jnp.inf)
            m = jnp.max(s_diag, axis=-1, keepdims=True)
            if n:
                m = jnp.maximum(m, jnp.max(s[:, :past], axis=-1, keepdims=True))
            e = jnp.exp(s_diag - m)
            total = jnp.sum(e, axis=-1, keepdims=True)
            if n:
                e_past = jnp.exp(s[:, :past] - m)
                total = total + jnp.sum(e_past, axis=-1, keepdims=True)
                e = jnp.concatenate([e_past, e], axis=1)
            es.append(e)
            totals.append(total)
        a = (es[0] * (1.0 / totals[0]) - es[1] * (lam / totals[1])).astype(BF16)
        o = _bdot(a, v_ref[:tv, :])
        o_ref[...] = (_rms(o) * sg_ref[...] * (1.0 - lambda_init)).astype(o_ref.dtype)

    for n in range(SEQ // tq):
        pl.when(qi == n)(functools.partial(query_tile, n))


def _attn_prompt(q, ktb, vb, lq1, lk1, lq2, lk2, subln_g, lambda_init):
    nq = SEQ // ATTN_Q_TILE
    kern = functools.partial(_attn_kernel, lambda_init=lambda_init)
    vec = pl.BlockSpec((1, HEAD_DIM), lambda b, h, i: (0, 0))
    return pl.pallas_call(
        kern,
        grid=(BATCH, N_HEADS, nq),
        in_specs=[pl.BlockSpec((ATTN_Q_TILE, LANES), lambda b, h, i: (b * nq + i, h)),
                  pl.BlockSpec((None, LANES, SEQ), lambda b, h, i: (b, h, 0)),
                  pl.BlockSpec((SEQ, LANES), lambda b, h, i: (b, h)),
                  vec, vec, vec, vec,
                  pl.BlockSpec((1, 2 * HEAD_DIM), lambda b, h, i: (0, 0))],
        out_specs=pl.BlockSpec((ATTN_Q_TILE, LANES), lambda b, h, i: (b * nq + i, h)),
        out_shape=jax.ShapeDtypeStruct((N_PROMPT, D_MODEL), BF16),
        compiler_params=_params("arbitrary", "arbitrary", "arbitrary"),
        name="diff_attn_prompt",
    )(q, ktb, vb, lq1, lk1, lq2, lk2, subln_g)


DEC_ROWS = N_HEADS * 2 * DEC_SEQ


def _decode_kernel(pt_ref, q_ref, kn_ref, vn_ref, lq1_ref, lk1_ref, lq2_ref, lk2_ref, sg_ref, *rest,
                   lambda_init):
    k_refs = rest[:PAGES_PER_STEP]
    v_refs = rest[PAGES_PER_STEP:2 * PAGES_PER_STEP]
    o_ref, m_ref, l_ref, acc_ref = rest[2 * PAGES_PER_STEP:]
    step = pl.program_id(1)
    rows_per_head = 2 * DEC_SEQ
    lane = lax.broadcasted_iota(jnp.int32, (rows_per_head, D_MODEL), 1)
    comp = _div_pow2(lax.broadcasted_iota(jnp.int32, (rows_per_head, D_MODEL), 0), DEC_SEQ)
    q8 = q_ref[...] * (HEAD_DIM ** -0.5)
    blk = _div_pow2(lane, HEAD_DIM)
    qbd = jnp.concatenate([jnp.where(blk == 2 * hd + comp, q8, 0.0) for hd in range(N_HEADS)],
                          axis=0).astype(BF16)

    @pl.when(step == 0)
    def _():
        m_ref[...] = jnp.full(m_ref.shape, -jnp.inf, F32)
        l_ref[...] = jnp.zeros(l_ref.shape, F32)
        acc_ref[...] = jnp.zeros(acc_ref.shape, F32)

    m, l, acc = m_ref[...], l_ref[...], acc_ref[...]
    s = jnp.concatenate([_bdot(qbd, kr[...].astype(BF16)) for kr in k_refs], axis=1)
    m_new = jnp.maximum(m, jnp.max(s, axis=-1, keepdims=True))
    alpha = jnp.exp(m - m_new)
    p = jnp.exp(s - m_new)
    l = l * alpha + jnp.sum(p, axis=-1, keepdims=True)
    pv = []
    for hd in range(N_HEADS):
        rows = slice(hd * rows_per_head, (hd + 1) * rows_per_head)
        v_h = jnp.concatenate([vr[pl.ds(hd, PAGE_SIZE, stride=N_HEADS), :].astype(BF16) for vr in v_refs],
                              axis=0)
        pv.append(_bdot(p[rows].astype(BF16), v_h))
    acc = acc * alpha + jnp.concatenate(pv, axis=0)
    m = m_new
    m_ref[...], l_ref[...], acc_ref[...] = m, l, acc

    @pl.when(step == pl.num_programs(1) - 1)
    def _():
        kn = kn_ref[...].astype(BF16).astype(F32)
        vn = vn_ref[...].astype(BF16).astype(F32)
        qf = qbd.astype(F32)
        row_q = _mod_pow2(lax.broadcasted_iota(jnp.int32, (DEC_ROWS, 1), 0), DEC_SEQ)
        s_new = []
        for j in range(DEC_SEQ):
            sj = jnp.sum(qf * kn[j:j + 1, :], axis=-1, keepdims=True)
            s_new.append(jnp.where(row_q >= j, sj, -jnp.inf))
        m2 = m
        for sj in s_new:
            m2 = jnp.maximum(m2, sj)
        alpha = jnp.exp(m - m2)
        l2 = l * alpha
        acc2 = acc * alpha
        for j, sj in enumerate(s_new):
            pj = jnp.exp(sj - m2)
            l2 = l2 + pj
            v_rows = jnp.concatenate(
                [jnp.broadcast_to(vn[j:j + 1, hd * LANES:(hd + 1) * LANES], (rows_per_head, LANES))
                 for hd in range(N_HEADS)], axis=0)
            acc2 = acc2 + pj.astype(BF16).astype(F32) * v_rows
        outn = acc2 / l2
        lam = _lambda(lq1_ref, lk1_ref, lq2_ref, lk2_ref, lambda_init)
        for hd in range(N_HEADS):
            r0 = hd * rows_per_head
            o = outn[r0:r0 + DEC_SEQ] - lam * outn[r0 + DEC_SEQ:r0 + rows_per_head]
            o_ref[:, hd * LANES:(hd + 1) * LANES] = _rms(o) * sg_ref[...] * (1.0 - lambda_init)


def _attn_decode(page_table, q8, k_new, v_new, cache_kt, cache_v, layer, lq1, lk1, lq2, lk2, subln_g, lambda_init):
    n_steps = N_PAGES // PAGES_PER_STEP
    kern = functools.partial(_decode_kernel, lambda_init=lambda_init)
    tok = pl.BlockSpec((None, DEC_SEQ, D_MODEL), lambda b, p, pt: (b, 0, 0))
    vec = pl.BlockSpec((1, HEAD_DIM), lambda b, p, pt: (0, 0))

    def page_spec(i):
        return pl.BlockSpec((None, None, N_HEADS * PAGE_SIZE, LANES),
                            lambda b, p, pt: (pt[b * N_PAGES + p * PAGES_PER_STEP + i], layer, 0, 0))

    grid_spec = pltpu.PrefetchScalarGridSpec(
        num_scalar_prefetch=1,
        grid=(DEC_BATCH, n_steps),
        in_specs=[pl.BlockSpec((None, 2 * DEC_SEQ, D_MODEL), lambda b, p, pt: (b, 0, 0)), tok, tok,
                  vec, vec, vec, vec, pl.BlockSpec((1, 2 * HEAD_DIM), lambda b, p, pt: (0, 0))]
                 + [page_spec(i) for i in range(PAGES_PER_STEP)] * 2,
        out_specs=tok,
        scratch_shapes=[pltpu.VMEM((DEC_ROWS, 1), F32), pltpu.VMEM((DEC_ROWS, 1), F32),
                        pltpu.VMEM((DEC_ROWS, LANES), F32)],
    )
    return pl.pallas_call(
        kern,
        grid_spec=grid_spec,
        out_shape=jax.ShapeDtypeStruct((DEC_BATCH, DEC_SEQ, D_MODEL), F32),
        compiler_params=_params("arbitrary", "arbitrary"),
        name="diff_attn_decode",
    )(page_table.reshape(-1), q8, k_new, v_new, lq1, lk1, lq2, lk2, subln_g,
      *([cache_kt] * PAGES_PER_STEP), *([cache_v] * PAGES_PER_STEP))


def _outproj_kernel(o_ref, x_ref, gt_ref, w_ref, xo_ref):
    xo_ref[...] = x_ref[...] + gt_ref[...] * _bdot(o_ref[...].astype(BF16), w_ref[...])


def _outproj_layer(st, o, x, mods, w_o):
    return pl.pallas_call(
        _outproj_kernel,
        grid=(st.grid,),
        in_specs=[st.rows_spec(), st.rows_spec(), st.mod_spec(2), _full_spec((D_MODEL, D_MODEL))],
        out_specs=st.rows_spec(),
        out_shape=jax.ShapeDtypeStruct((st.rows, D_MODEL), F32),
        compiler_params=_params("arbitrary"),
        name="attn_out_proj",
    )(o, x, mods, w_o)


N_ALL = N_PROMPT + N_SAMPLE
ROW_W = D_MODEL + LANES
PAIRS_PER_GROUP = EXPERTS_PER_GROUP * (EXPERTS_PER_GROUP - 1) // 2
N_BUCKETS = N_EXPERT_GROUPS * PAIRS_PER_GROUP
SORT_TILE = 256
N_SORT_TILES = (N_ALL + N_BUCKETS * (SORT_TILE - 1)) // SORT_TILE
N_SORT_ROWS = N_SORT_TILES * SORT_TILE
DMA_UNROLL = 8


def _route_store(x, route_in_refs, route_ref, gates_ref):
    sh_ref, sc_ref, ng_ref, rwt_ref, rb_ref = route_in_refs
    tm = x.shape[0]
    h = _modulate(x, ng_ref[...], sh_ref[...], sc_ref[...])
    logits = lax.dot_general(rwt_ref[...], h, (((1,), (1,)), ((), ())), preferred_element_type=F32,
                             precision=lax.Precision.HIGHEST)
    scores = jax.nn.sigmoid(logits)
    sel = scores + rb_ref[...]
    eid = lax.broadcasted_iota(jnp.int32, sel.shape, 0)
    grp = _div_pow2(eid, EXPERTS_PER_GROUP)
    neg = -jnp.inf

    def first_max(vals):
        m = jnp.max(vals, axis=0, keepdims=True)
        idx = jnp.min(jnp.where(vals == m, eid, N_EXPERTS), axis=0, keepdims=True)
        return m, idx

    best = None
    for g in range(N_EXPERT_GROUPS):
        vg = jnp.where(grp == g, sel, neg)
        m1, i1 = first_max(vg)
        m2, _ = first_max(jnp.where(eid == i1, neg, vg))
        gs = m1 + m2
        if best is None:
            best, best_score = jnp.zeros_like(i1), gs
        else:
            better = gs > best_score
            best = jnp.where(better, g, best)
            best_score = jnp.where(better, gs, best_score)
    masked = jnp.where(grp == best, sel, neg)
    _, ia = first_max(masked)
    _, ib = first_max(jnp.where(eid == ia, neg, masked))
    ga = jnp.sum(jnp.where(eid == ia, scores, 0.0), axis=0, keepdims=True)
    gb = jnp.sum(jnp.where(eid == ib, scores, 0.0), axis=0, keepdims=True)
    den = ga + gb
    a_first = ia < ib
    lo = jnp.where(a_first, ia, ib).astype(F32)
    hi = jnp.where(a_first, ib, ia).astype(F32)
    g_lo = jnp.where(a_first, ga, gb) / den
    g_hi = jnp.where(a_first, gb, ga) / den
    r8 = lax.broadcasted_iota(jnp.int32, route_ref.shape, 0)
    route_ref[...] = jnp.where(r8 == 0, lo, jnp.where(r8 == 1, hi, 0.0))
    r128 = lax.broadcasted_iota(jnp.int32, (LANES, tm), 0)
    gates_t = jnp.where(r128 == 0, g_lo, jnp.where(r128 == 1, g_hi, 0.0))
    gates_ref[...] = gates_t.T


ROUTE_ROWS = 8


class _Routing:
    def __init__(self, st, mods, ng, router_wt, router_b):
        self.args = [mods, mods, ng, router_wt, router_b]
        self.in_specs = [st.mod_spec(3), st.mod_spec(4), _full_spec((1, D_MODEL)),
                         _full_spec((N_EXPERTS, D_MODEL)), _full_spec((N_EXPERTS, 1))]
        self.out_specs = [pl.BlockSpec((ROUTE_ROWS, st.tile), lambda i: (0, i)), st.rows_spec(LANES)]
        self.out_shape = [jax.ShapeDtypeStruct((ROUTE_ROWS, st.rows), F32),
                          jax.ShapeDtypeStruct((st.rows, LANES), F32)]


N_ROUTE_IN = 5


def _router_kernel(x_ref, *refs):
    _route_store(x_ref[...], refs[:N_ROUTE_IN], *refs[N_ROUTE_IN:])


def _router_layer(st, x, routing):
    return pl.pallas_call(
        _router_kernel,
        grid=(st.grid,),
        in_specs=[st.rows_spec()] + routing.in_specs,
        out_specs=routing.out_specs,
        out_shape=routing.out_shape,
        compiler_params=_params("arbitrary"),
        name="moe_router",
    )(x, *routing.args)


def _moe_plan(route):
    lo = route[0].astype(jnp.int32)
    hi = route[1].astype(jnp.int32)
    i, j = lo % EXPERTS_PER_GROUP, hi % EXPERTS_PER_GROUP
    pair = i * (2 * EXPERTS_PER_GROUP - 1 - i) // 2 + (j - i - 1)
    bucket = (lo // EXPERTS_PER_GROUP) * PAIRS_PER_GROUP + pair
    onehot = (bucket[:, None] == jnp.arange(N_BUCKETS)[None, :]).astype(jnp.int32)
    csum = jnp.cumsum(onehot, axis=0)
    counts = csum[-1]
    rank = jnp.sum(onehot * csum, axis=1) - 1
    tiles = (counts + SORT_TILE - 1) // SORT_TILE
    ends = jnp.cumsum(tiles)
    starts = ends - tiles
    n_used = ends[-1]
    pos = jnp.sum(onehot * starts[None, :], axis=1) * SORT_TILE + rank
    tile_id = jnp.arange(N_SORT_TILES)
    tile_bucket = jnp.sum((jnp.minimum(tile_id, n_used - 1)[:, None] >= ends[None, :]).astype(jnp.int32), axis=1)
    tile_bucket = jnp.minimum(tile_bucket, N_BUCKETS - 1)
    pair_lo = jnp.array([a for a in range(EXPERTS_PER_GROUP) for _ in range(a + 1, EXPERTS_PER_GROUP)], jnp.int32)
    pair_hi = jnp.array([b for a in range(EXPERTS_PER_GROUP) for b in range(a + 1, EXPERTS_PER_GROUP)], jnp.int32)
    base = (tile_bucket // PAIRS_PER_GROUP) * EXPERTS_PER_GROUP
    tile_lo = base + pair_lo[tile_bucket % PAIRS_PER_GROUP]
    tile_hi = base + pair_hi[tile_bucket % PAIRS_PER_GROUP]
    return pos.astype(jnp.int32), tile_lo, tile_hi, n_used.reshape(1).astype(jnp.int32)


def _bucket_kernel(pos_ref, x_ref, sh_ref, sc_ref, ng_ref, gates_ref, dst_in_ref, dst_ref, buf, sem, *, row0):
    del dst_in_ref
    tm = x_ref.shape[0]
    t = pl.program_id(0)
    nt = pl.num_programs(0)
    slot = t % 2

    def wait(s):
        pltpu.make_async_copy(buf.at[s], dst_ref.at[pl.ds(0, tm), :], sem.at[s]).wait()

    @pl.when(t >= 2)
    def _():
        wait(slot)

    buf[slot, :, :D_MODEL] = _modulate(x_ref[...], ng_ref[...], sh_ref[...], sc_ref[...])
    buf[slot, :, D_MODEL:] = gates_ref[...]
    base = row0 + t * tm

    def body(i, carry):
        for u in range(DMA_UNROLL):
            r = i * DMA_UNROLL + u
            pltpu.make_async_copy(buf.at[slot, pl.ds(r, 1), :],
                                  dst_ref.at[pl.ds(pos_ref[base + r], 1), :], sem.at[slot]).start()
        return carry

    lax.fori_loop(0, tm // DMA_UNROLL, body, 0)

    @pl.when(t == nt - 1)
    def _():
        wait(slot)

    @pl.when((t == nt - 1) & (nt > 1))
    def _():
        wait(1 - slot)


def _bucket_rows(st, pos, x, mods, ng, gates, dst, row0):
    kern = functools.partial(_bucket_kernel, row0=row0)
    if st.per_row_mods:
        def mod_spec(k):
            return pl.BlockSpec((st.tile, D_MODEL), lambda i, p: (i, k))
    else:
        tpb = st.tiles_per_batch

        def mod_spec(k):
            return pl.BlockSpec((None, None, 1, D_MODEL), lambda i, p: (i // tpb, k, 0, 0))
    hbm = pl.BlockSpec(memory_space=pl.ANY)
    grid_spec = pltpu.PrefetchScalarGridSpec(
        num_scalar_prefetch=1,
        grid=(st.grid,),
        in_specs=[pl.BlockSpec((st.tile, D_MODEL), lambda i, p: (i, 0)), mod_spec(3), mod_spec(4),
                  pl.BlockSpec((1, D_MODEL), lambda i, p: (0, 0)),
                  pl.BlockSpec((st.tile, LANES), lambda i, p: (i, 0)), hbm],
        out_specs=hbm,
        scratch_shapes=[pltpu.VMEM((2, st.tile, ROW_W), F32), pltpu.SemaphoreType.DMA((2,))],
    )
    return pl.pallas_call(
        kern,
        grid_spec=grid_spec,
        out_shape=jax.ShapeDtypeStruct(dst.shape, dst.dtype),
        input_output_aliases={6: 0},
        compiler_params=_params("arbitrary"),
        name="moe_bucket_rows",
    )(pos, x, mods, mods, ng, gates, dst)


def _moe_kernel(lo_ref, hi_ref, nu_ref, xs_ref, wg_lo, wu_lo, wd_lo, wg_hi, wu_hi, wd_hi, ys_ref):
    del lo_ref, hi_ref
    t = pl.program_id(0)

    @pl.when(t < nu_ref[0])
    def _():
        row = xs_ref[...]
        x = row[:, :D_MODEL].astype(BF16)

        def expert(wg, wu, wd, gate):
            g = _bdot(x, wg[...])
            u = _bdot(x, wu[...])
            return _bdot((jax.nn.silu(g) * u * gate).astype(BF16), wd[...])

        ys_ref[...] = (expert(wg_lo, wu_lo, wd_lo, row[:, D_MODEL:D_MODEL + 1])
                       + expert(wg_hi, wu_hi, wd_hi, row[:, D_MODEL + 1:D_MODEL + 2]))

    @pl.when(t >= nu_ref[0])
    def _():
        ys_ref[...] = jnp.zeros(ys_ref.shape, ys_ref.dtype)


def _moe_experts(xs, tile_lo, tile_hi, n_used, wg, wu, wd, layer):
    def w_spec(shape, which):
        return pl.BlockSpec((None, None) + shape,
                            lambda t, lo, hi, nu: (layer, (lo, hi)[which][t], 0, 0))

    up, down = (D_MODEL, D_EXPERT), (D_EXPERT, D_MODEL)
    grid_spec = pltpu.PrefetchScalarGridSpec(
        num_scalar_prefetch=3,
        grid=(N_SORT_TILES,),
        in_specs=[pl.BlockSpec((SORT_TILE, ROW_W), lambda t, lo, hi, nu: (t, 0)),
                  w_spec(up, 0), w_spec(up, 0), w_spec(down, 0),
                  w_spec(up, 1), w_spec(up, 1), w_spec(down, 1)],
        out_specs=pl.BlockSpec((SORT_TILE, D_MODEL), lambda t, lo, hi, nu: (t, 0)),
    )
    return pl.pallas_call(
        _moe_kernel,
        grid_spec=grid_spec,
        out_shape=jax.ShapeDtypeStruct((N_SORT_ROWS, D_MODEL), F32),
        compiler_params=_params("arbitrary"),
        name="moe_experts",
    )(tile_lo, tile_hi, n_used, xs, wg, wu, wd, wg, wu, wd)


def _unsort_kernel(pos_ref, ys_ref, x_ref, g2_ref, fg_ref, o_ref, buf, sem, *, row0, final):
    tm = x_ref.shape[0]
    t = pl.program_id(0)

    def fetch(tile, slot):
        base = row0 + tile * tm

        def body(i, carry):
            for u in range(DMA_UNROLL):
                r = i * DMA_UNROLL + u
                pltpu.make_async_copy(ys_ref.at[pl.ds(pos_ref[base + r], 1), :],
                                      buf.at[slot, pl.ds(r, 1), :], sem.at[slot]).start()
            return carry

        lax.fori_loop(0, tm // DMA_UNROLL, body, 0)

    @pl.when(t == 0)
    def _():
        fetch(0, 0)

    @pl.when(t + 1 < pl.num_programs(0))
    def _():
        fetch(t + 1, (t + 1) % 2)

    slot = t % 2
    pltpu.make_async_copy(ys_ref.at[pl.ds(0, tm), :], buf.at[slot], sem.at[slot]).wait()
    y = x_ref[...] + g2_ref[...] * buf[slot]
    o_ref[...] = _rms(y) * fg_ref[...] if final else y


def _unsort_layer(st, pos, ys, x, mods, row0, final_g, final):
    kern = functools.partial(_unsort_kernel, row0=row0, final=final)
    if st.per_row_mods:
        g2_spec = pl.BlockSpec((st.tile, D_MODEL), lambda i, p: (i, 5))
    else:
        tpb = st.tiles_per_batch
        g2_spec = pl.BlockSpec((None, None, 1, D_MODEL), lambda i, p: (i // tpb, 5, 0, 0))
    rows = pl.BlockSpec((st.tile, D_MODEL), lambda i, p: (i, 0))
    grid_spec = pltpu.PrefetchScalarGridSpec(
        num_scalar_prefetch=1,
        grid=(st.grid,),
        in_specs=[pl.BlockSpec(memory_space=pl.ANY), rows, g2_spec,
                  pl.BlockSpec((1, D_MODEL), lambda i, p: (0, 0))],
        out_specs=rows,
        scratch_shapes=[pltpu.VMEM((2, st.tile, D_MODEL), F32), pltpu.SemaphoreType.DMA((2,))],
    )
    return pl.pallas_call(
        kern,
        grid_spec=grid_spec,
        out_shape=jax.ShapeDtypeStruct((st.rows, D_MODEL), F32),
        compiler_params=_params("arbitrary"),
        name="moe_unsort_residual",
    )(pos, ys, x, mods, final_g)


def kernel(x_prompt, x_sample, cache_k, cache_v, state_pool, page_table, c_prompt, c_sample,
           norm1_g, norm2_g, ada_w, ada_b, a_w_in, a_ln_g, a_ln_b, a_w_s, a_b_s, a_w_out,
           b_w_grp, b_scale, c_w_qkv, c_lq1, c_lk1, c_lq2, c_lk2, c_subln_g, c_w_o,
           router_w, router_bias, e_w_gate, e_w_up, e_w_down, final_g):
    sp = _Stream(N_PROMPT, PROMPT_TILE, SEQ // PROMPT_TILE, per_row_mods=False)
    ss = _Stream(N_SAMPLE, N_SAMPLE, 1, per_row_mods=True)
    sr = _Stream(N_PROMPT, ROUTER_TILE, SEQ // ROUTER_TILE, per_row_mods=False)

    xp = x_prompt.reshape(N_PROMPT, D_MODEL)
    xs = x_sample.reshape(N_SAMPLE, D_MODEL)
    mods_all = _ada(jnp.concatenate([c_prompt, c_sample], axis=0), ada_w, ada_b)

    wg_b, wu_b, wd_b = e_w_gate.astype(BF16), e_w_up.astype(BF16), e_w_down.astype(BF16)
    rwt = router_w.T
    rb = router_bias.reshape(N_EXPERTS, 1)
    fg = final_g.reshape(1, D_MODEL)
    bucketed = jnp.zeros((N_SORT_ROWS, ROW_W), F32)
    n_phys = cache_k.shape[0]
    n_c_layers = cache_k.shape[1]
    ckt = jnp.transpose(cache_k, (0, 1, 3, 4, 5, 2)).reshape(n_phys, n_c_layers, D_MODEL, PAGE_SIZE)
    cv = cache_v.reshape(n_phys, n_c_layers, PAGE_SIZE * N_HEADS, 2 * HEAD_DIM)

    av_p, av_s, pr_p, pr_s, kp_l, vp_l, ks_l, vs_l = [], [], [], [], [], [], [], []
    for i in range(DEPTH):
        kind, j = i % N_MIXERS, i // N_MIXERS
        mp = mods_all[i, :BATCH].reshape(BATCH, N_MODS, 1, D_MODEL)
        ms = jnp.repeat(mods_all[i, BATCH:], DEC_SEQ, axis=0)
        ng1 = norm1_g[i].reshape(1, D_MODEL)
        ng2 = norm2_g[i].reshape(1, D_MODEL)
        if kind == 0:
            w_in = a_w_in[j].astype(BF16)
            w_out = a_w_out[j].astype(BF16)
            ln_g, ln_b = a_ln_g[j].reshape(1, D_GATE), a_ln_b[j].reshape(1, D_GATE)
            bias = jnp.repeat(a_b_s[j].T, D_GATE // A_GROUPS, axis=1)
            xp, vrow_p = _gmlp_layer(sp, xp, mp, ng1, w_in, ln_g, ln_b, a_w_s[j], bias, w_out, CHUNK, BATCH, False)
            reps = CHUNK // DEC_SEQ
            wmix_s = jnp.tile(a_w_s[j][:, :DEC_SEQ, :DEC_SEQ], (1, reps, reps))
            bias_s = jnp.tile(bias[:DEC_SEQ], (reps, 1))
            xs, vrow_s = _gmlp_layer(ss, xs, ms, ng1, a_w_in[j], ln_g, ln_b, wmix_s, bias_s, a_w_out[j],
                                     DEC_SEQ, 1, True)
            av_p.append(vrow_p.reshape(BATCH, CHUNK, D_GATE))
            av_s.append(vrow_s.reshape(DEC_BATCH, DEC_SEQ, D_GATE))
        elif kind == 1:
            w_grp = b_w_grp[j].astype(BF16)
            scale = b_scale[j].reshape(1, D_MODEL)
            hb = PROMPT_TILE // POOL_HALO
            halo_spec = pl.BlockSpec((POOL_HALO, D_MODEL), lambda t: (jnp.maximum(t * hb - 1, 0), 0))
            xp, rows_p = _pool_layer(sp, xp, xp, halo_spec, mp, ng1, w_grp, scale, False, 0, POOL_HALO, BATCH)
            pr_p.append(rows_p.reshape(BATCH, POOL_HALO, D_MODEL)[:, 1:])
            pad_t = 2 * DEC_SEQ
            s8 = _Stream(DEC_BATCH * pad_t, pad_t, 1, per_row_mods=True)
            xs8 = jnp.pad(xs.reshape(DEC_BATCH, DEC_SEQ, D_MODEL), ((0, 0), (0, DEC_SEQ), (0, 0)))
            ms8 = jnp.repeat(mods_all[i, BATCH:], pad_t, axis=0)
            prev = jnp.pad(state_pool[:, j], ((0, 0), (1, 0), (0, 0))).reshape(DEC_BATCH * POOL_HALO, D_MODEL)
            prev_spec = pl.BlockSpec((POOL_HALO, D_MODEL), lambda t: (t, 0))
            n_out = POOL_HALO + pad_t
            xs8, rows_s = _pool_layer(s8, xs8.reshape(-1, D_MODEL), prev, prev_spec, ms8, ng1, w_grp, scale,
                                      True, PAST_LEN, n_out, DEC_BATCH)
            xs = xs8.reshape(DEC_BATCH, pad_t, D_MODEL)[:, :DEC_SEQ].reshape(N_SAMPLE, D_MODEL)
            lo = 1 + DEC_SEQ
            pr_s.append(rows_s.reshape(DEC_BATCH, n_out, D_MODEL)[:, lo:lo + POOL_BUF])
        else:
            lambda_init = 0.8 - 0.6 * math.exp(-0.3 * i)
            w_qkv = c_w_qkv[j].astype(BF16)
            w_o = c_w_o[j].astype(BF16)
            lq1, lk1 = c_lq1[j].reshape(1, HEAD_DIM), c_lk1[j].reshape(1, HEAD_DIM)
            lq2, lk2 = c_lq2[j].reshape(1, HEAD_DIM), c_lk2[j].reshape(1, HEAD_DIM)
            sg = c_subln_g[j].reshape(1, 2 * HEAD_DIM)
            pos_p = jnp.arange(SEQ)
            cos_p, sin_p = _rope_angles(pos_p)
            qp, kt, ktb, vp, vpb = _qkv_t_layer(
                sp, xp, mp, ng1, w_qkv[:, :D_MODEL], w_qkv[:, D_MODEL:2 * D_MODEL].T, w_qkv[:, 2 * D_MODEL:],
                _rope_tables(pos_p), (cos_p.T, sin_p.T))
            op = _attn_prompt(qp, ktb, vpb, lq1, lk1, lq2, lk2, sg, lambda_init)
            xp = _outproj_layer(sp, op, xp, mp, w_o)
            tab_s = _rope_tables(jnp.tile(PAST_LEN + jnp.arange(DEC_SEQ), DEC_BATCH))
            qs, ks, vs = _qkv_layer(ss, xs, ms, ng1, w_qkv, tab_s,
                                    pl.BlockSpec((N_SAMPLE, LANES), lambda t: (0, 0)))
            tok = (DEC_BATCH, DEC_SEQ, D_MODEL)
            q8 = jnp.concatenate([qs.reshape(tok)] * 2, axis=1)
            os_ = _attn_decode(page_table, q8, ks.reshape(tok), vs.reshape(tok), ckt, cv, j,
                               lq1, lk1, lq2, lk2, sg, lambda_init)
            xs = _outproj_layer(ss, os_.reshape(N_SAMPLE, D_MODEL), xs, ms, w_o)
            kp = jnp.transpose(kt.reshape(BATCH, N_HEADS, 2, HEAD_DIM, SEQ), (0, 4, 1, 2, 3))
            kp_l.append(kp)
            vp_l.append(vp.reshape(BATCH, SEQ, N_HEADS, 2 * HEAD_DIM))
            ks_l.append(ks.reshape(DEC_BATCH, DEC_SEQ, N_HEADS, 2, HEAD_DIM))
            vs_l.append(vs.reshape(DEC_BATCH, DEC_SEQ, N_HEADS, 2 * HEAD_DIM))

        route_p, gates_p = _router_layer(sr, xp, _Routing(sr, mp, ng2, rwt, rb))
        route_s, gates_s = _router_layer(ss, xs, _Routing(ss, ms, ng2, rwt, rb))
        pos, tile_lo, tile_hi, n_used = _moe_plan(jnp.concatenate([route_p, route_s], axis=1))
        bucketed = _bucket_rows(sr, pos, xp, mp, ng2, gates_p, bucketed, 0)
        bucketed = _bucket_rows(ss, pos, xs, ms, ng2, gates_s, bucketed, N_PROMPT)
        ys = _moe_experts(bucketed, tile_lo, tile_hi, n_used, wg_b, wu_b, wd_b, i)
        last = i == DEPTH - 1
        xp = _unsort_layer(sr, pos, ys, xp, mp, 0, fg, last)
        xs = _unsort_layer(ss, pos, ys, xs, ms, N_PROMPT, fg, last)

    y_prompt = xp.reshape(BATCH, SEQ, D_MODEL)
    y_sample = xs.reshape(DEC_BATCH, DEC_SEQ, D_MODEL)
    return (y_prompt, y_sample,
            jnp.stack(av_p, axis=1), jnp.stack(av_s, axis=1),
            jnp.stack(pr_p, axis=1), jnp.stack(pr_s, axis=1),
            jnp.stack(kp_l, axis=1), jnp.stack(vp_l, axis=1),
            jnp.stack(ks_l, axis=1), jnp.stack(vs_l, axis=1))
```

```python
import functools
import math

import jax
import jax.numpy as jnp
from jax import lax
from jax.experimental import pallas as pl
from jax.experimental.pallas import tpu as pltpu

F32 = jnp.float32
BF16 = jnp.bfloat16

D_MODEL = 1024
BATCH = 8
SEQ = 2048
DEPTH = 4
DEC_BATCH = 32
DEC_SEQ = 4
PAST_LEN = 8192
PAGE_SIZE = 128
N_PAGES = PAST_LEN // PAGE_SIZE
N_MIXERS = 3
CHUNK = 128
D_GATE = D_MODEL
A_GROUPS = 4
POOL_WINDOWS = (2, 4, 8, 16)
B_GROUPS = len(POOL_WINDOWS)
POOL_BUF = max(POOL_WINDOWS) - 1
POOL_HALO = POOL_BUF + 1
N_HEADS = 8
HEAD_DIM = D_MODEL // N_HEADS // 2
ROT_DIM = HEAD_DIM // 4
ROPE_THETA = 500000.0
N_EXPERTS = 16
N_EXPERT_GROUPS = 4
EXPERTS_PER_GROUP = N_EXPERTS // N_EXPERT_GROUPS
D_EXPERT = 512
EPS = 1e-6
N_MODS = 6

N_PROMPT = BATCH * SEQ
N_SAMPLE = DEC_BATCH * DEC_SEQ
LANES = 128
VMEM_LIMIT = 56 * 1024 * 1024

PROMPT_TILE = 512
ROUTER_TILE = 512
ATTN_Q_TILE = 512
PAGES_PER_STEP = 16


def _params(*sem):
    return pltpu.CompilerParams(dimension_semantics=sem, vmem_limit_bytes=VMEM_LIMIT)


class _Stream:
    def __init__(self, rows, tile, tiles_per_batch, per_row_mods):
        self.rows = rows
        self.tile = tile
        self.tiles_per_batch = tiles_per_batch
        self.per_row_mods = per_row_mods
        self.grid = rows // tile

    def rows_spec(self, width=D_MODEL):
        return pl.BlockSpec((self.tile, width), lambda i: (i, 0))

    def mod_spec(self, k):
        if self.per_row_mods:
            return pl.BlockSpec((self.tile, D_MODEL), lambda i: (i, k))
        tpb = self.tiles_per_batch
        return pl.BlockSpec((None, None, 1, D_MODEL), lambda i: (i // tpb, k, 0, 0))


def _full_spec(shape):
    nd = len(shape)
    return pl.BlockSpec(shape, lambda i: (0,) * nd)


def _rms(x):
    return x * lax.rsqrt(jnp.mean(x * x, axis=-1, keepdims=True) + EPS)


def _modulate(x, g, sh, sc):
    return _rms(x) * g * (1.0 + sc) + sh


def _bdot(a, b):
    return jnp.dot(a, b, preferred_element_type=F32)


def _div_pow2(x, n):
    assert n & (n - 1) == 0
    return lax.shift_right_logical(x, n.bit_length() - 1)


def _mod_pow2(x, n):
    assert n & (n - 1) == 0
    return x & (n - 1)


def _ada_kernel(c_ref, w_ref, b_ref, o_ref):
    o_ref[...] = jnp.dot(jax.nn.silu(c_ref[...]), w_ref[...], preferred_element_type=F32,
                         precision=lax.Precision.HIGHEST) + b_ref[...]


def _ada(c_all, ada_w, ada_b):
    nb = c_all.shape[0]
    tn = 1536
    return pl.pallas_call(
        _ada_kernel,
        grid=(DEPTH, N_MODS * D_MODEL // tn),
        in_specs=[pl.BlockSpec((nb, D_MODEL), lambda l, n: (0, 0)),
                  pl.BlockSpec((None, D_MODEL, tn), lambda l, n: (l, 0, n)),
                  pl.BlockSpec((None, 1, tn), lambda l, n: (l, 0, n))],
        out_specs=pl.BlockSpec((None, nb, tn), lambda l, n: (l, 0, n)),
        out_shape=jax.ShapeDtypeStruct((DEPTH, nb, N_MODS * D_MODEL), F32),
        compiler_params=_params("arbitrary", "arbitrary"),
        name="ada_mod",
    )(c_all, ada_w, ada_b.reshape(DEPTH, 1, N_MODS * D_MODEL))


def _mm(a, b, precise):
    if precise:
        return jnp.dot(a.astype(F32), b.astype(F32), preferred_element_type=F32, precision=lax.Precision.HIGHEST)
    return _bdot(a.astype(BF16), b.astype(BF16))


def _gmlp_kernel(x_ref, sh_ref, sc_ref, gt_ref, ng_ref, win_ref, lng_ref, lnb_ref, wmix_ref,
                 bmix_ref, wout_ref, xo_ref, v_ref, mix_ref, *, period, precise):
    tm = x_ref.shape[0]
    x = x_ref[...]
    h = _modulate(x, ng_ref[...], sh_ref[...], sc_ref[...])
    z = _mm(h, win_ref[...], precise)
    z = 0.5 * z * (1.0 + lax.erf(z * (2.0 ** -0.5)))
    u = z[:, :D_GATE]
    v = z[:, D_GATE:]
    mu = jnp.mean(v, axis=-1, keepdims=True)
    vc = v - mu
    var = jnp.mean(vc * vc, axis=-1, keepdims=True)
    v = vc * lax.rsqrt(var + EPS) * lng_ref[...] + lnb_ref[...]
    v_ref[...] = v[tm - CHUNK:, :]
    vb = v if precise else v.astype(BF16)
    t = lax.broadcasted_iota(jnp.int32, (CHUNK, CHUNK), 0)
    s = lax.broadcasted_iota(jnp.int32, (CHUNK, CHUNK), 1)
    keep = (s <= t) & (_div_pow2(t, period) == _div_pow2(s, period))
    cg = D_GATE // A_GROUPS
    for g in range(A_GROUPS):
        wm = jnp.where(keep, wmix_ref[g], 0.0)
        for c in range(tm // CHUNK):
            rows = slice(c * CHUNK, (c + 1) * CHUNK)
            cols = slice(g * cg, (g + 1) * cg)
            mix_ref[rows, cols] = _mm(wm, vb[rows, cols], precise) + bmix_ref[:, cols]
    y = _mm(u * mix_ref[...], wout_ref[...], precise)
    xo_ref[...] = x + gt_ref[...] * y


def _gmlp_layer(st, x, mods, ng, w_in, ln_g, ln_b, wmix, bmix, w_out, period, n_vblocks, precise):
    kern = functools.partial(_gmlp_kernel, period=period, precise=precise)
    vb_per = st.grid // n_vblocks
    return pl.pallas_call(
        kern,
        grid=(st.grid,),
        in_specs=[st.rows_spec(), st.mod_spec(0), st.mod_spec(1), st.mod_spec(2),
                  _full_spec((1, D_MODEL)), _full_spec((D_MODEL, 2 * D_GATE)),
                  _full_spec((1, D_GATE)), _full_spec((1, D_GATE)),
                  _full_spec((A_GROUPS, CHUNK, CHUNK)), _full_spec((CHUNK, D_GATE)),
                  _full_spec((D_GATE, D_MODEL))],
        out_specs=[st.rows_spec(),
                   pl.BlockSpec((CHUNK, D_GATE), lambda i: (i // vb_per, 0))],
        out_shape=[jax.ShapeDtypeStruct((st.rows, D_MODEL), F32),
                   jax.ShapeDtypeStruct((n_vblocks * CHUNK, D_GATE), F32)],
        scratch_shapes=[pltpu.VMEM((st.tile, D_GATE), F32)],
        compiler_params=_params("arbitrary"),
        name="gmlp_mixer",
    )(x, mods, mods, mods, ng, w_in, ln_g, ln_b, wmix, bmix, w_out)


def _pool_kernel(x_ref, halo_ref, sh_ref, sc_ref, gt_ref, ng_ref, wgrp_ref, scale_ref,
                 xo_ref, rows_ref, *, halo_is_h, tiles_per_batch, pos0, n_rows_out):
    tm = x_ref.shape[0]
    x = x_ref[...]
    ng, sh, sc = ng_ref[...], sh_ref[...], sc_ref[...]
    h = _modulate(x, ng, sh, sc)
    if halo_is_h:
        halo = halo_ref[...]
    else:
        sh_h = sh if sh.shape[0] == 1 else sh[:POOL_HALO]
        sc_h = sc if sc.shape[0] == 1 else sc[:POOL_HALO]
        first = (pl.program_id(0) % tiles_per_batch) == 0
        halo = jnp.where(first, 0.0, _modulate(halo_ref[...], ng, sh_h, sc_h))
    ext = jnp.concatenate([halo, h], axis=0)
    rows_ref[...] = ext[POOL_HALO + tm - n_rows_out:, :]
    t0 = (pl.program_id(0) % tiles_per_batch) * tm
    pos = pos0 + t0 + lax.broadcasted_iota(jnp.int32, (tm, 1), 0)
    cg = D_MODEL // B_GROUPS
    ys = []
    for g, w in enumerate(POOL_WINDOWS):
        cols = slice(g * cg, (g + 1) * cg)
        acc = ext[:, cols]
        span = 1
        while span < w:
            acc = acc + pltpu.roll(acc, span, 0)
            span *= 2
        cnt = jnp.minimum(pos + 1, w).astype(F32)
        pooled = acc[POOL_HALO:, :] / cnt - h[:, cols]
        ys.append(_bdot(pooled.astype(BF16), wgrp_ref[g]))
    y = jnp.concatenate(ys, axis=-1) * scale_ref[...]
    xo_ref[...] = x + gt_ref[...] * y


def _pool_layer(st, x, halo, halo_spec, mods, ng, w_grp, scale, halo_is_h, pos0, n_rows_out, n_batches):
    kern = functools.partial(_pool_kernel, halo_is_h=halo_is_h, tiles_per_batch=st.tiles_per_batch,
                             pos0=pos0, n_rows_out=n_rows_out)
    cg = D_MODEL // B_GROUPS
    tpb = st.tiles_per_batch
    return pl.pallas_call(
        kern,
        grid=(st.grid,),
        in_specs=[st.rows_spec(), halo_spec, st.mod_spec(0), st.mod_spec(1), st.mod_spec(2),
                  _full_spec((1, D_MODEL)), _full_spec((B_GROUPS, cg, cg)), _full_spec((1, D_MODEL))],
        out_specs=[st.rows_spec(),
                   pl.BlockSpec((n_rows_out, D_MODEL), lambda i: (i // tpb, 0))],
        out_shape=[jax.ShapeDtypeStruct((st.rows, D_MODEL), F32),
                   jax.ShapeDtypeStruct((n_batches * n_rows_out, D_MODEL), F32)],
        compiler_params=_params("arbitrary"),
        name="pool_mixer",
    )(x, halo, mods, mods, mods, ng, w_grp, scale)


def _rope_angles(pos):
    half = ROT_DIM // 2
    inv = ROPE_THETA ** (-(jnp.arange(half, dtype=F32) * 2.0) / ROT_DIM)
    ang = pos.astype(F32)[:, None] * inv[None, :]
    return jnp.cos(ang), jnp.sin(ang)


def _rope_tables(pos):
    cos, sin = _rope_angles(pos)
    n = pos.shape[0]
    pad = jnp.zeros((n, HEAD_DIM - ROT_DIM), F32)
    c = jnp.concatenate([cos, cos, pad + 1.0], axis=1)
    sa = jnp.concatenate([-sin, jnp.zeros_like(sin), pad], axis=1)
    sb = jnp.concatenate([jnp.zeros_like(sin), sin, pad], axis=1)
    return tuple(jnp.concatenate([t, t], axis=1) for t in (c, sa, sb))


def _rope_rows(x, c, sa, sb):
    half = ROT_DIM // 2
    out = []
    for hd in range(N_HEADS):
        xh = x[:, hd * LANES:(hd + 1) * LANES]
        out.append(xh * c + pltpu.roll(xh, LANES - half, 1) * sa + pltpu.roll(xh, half, 1) * sb)
    return out


def _qkv_kernel(x_ref, sh_ref, sc_ref, ng_ref, w_ref, c_ref, sa_ref, sb_ref, q_ref, k_ref, v_ref):
    h = _modulate(x_ref[...], ng_ref[...], sh_ref[...], sc_ref[...]).astype(BF16)
    qkv = _bdot(h, w_ref[...])
    c, sa, sb = c_ref[...], sa_ref[...], sb_ref[...]
    for which, ref in ((0, q_ref), (1, k_ref)):
        rot = _rope_rows(qkv[:, which * D_MODEL:(which + 1) * D_MODEL], c, sa, sb)
        for hd in range(N_HEADS):
            ref[:, hd * LANES:(hd + 1) * LANES] = rot[hd]
    v_ref[...] = qkv[:, 2 * D_MODEL:]


def _qkv_layer(st, x, mods, ng, w_qkv, tables, table_spec):
    return pl.pallas_call(
        _qkv_kernel,
        grid=(st.grid,),
        in_specs=[st.rows_spec(), st.mod_spec(0), st.mod_spec(1), _full_spec((1, D_MODEL)),
                  _full_spec((D_MODEL, 3 * D_MODEL)), table_spec, table_spec, table_spec],
        out_specs=[st.rows_spec(), st.rows_spec(), st.rows_spec()],
        out_shape=[jax.ShapeDtypeStruct((st.rows, D_MODEL), F32)] * 3,
        compiler_params=_params("arbitrary"),
        name="qkv_rope",
    )(x, mods, mods, ng, w_qkv, *tables)


def _qkv_t_kernel(x_ref, sh_ref, sc_ref, ng_ref, wq_ref, wkt_ref, wv_ref, c_ref, sa_ref, sb_ref,
                  ct_ref, st_ref, q_ref, kt_ref, ktb_ref, v_ref, vb_ref):
    h = _modulate(x_ref[...], ng_ref[...], sh_ref[...], sc_ref[...]).astype(BF16)
    rot = _rope_rows(_bdot(h, wq_ref[...]), c_ref[...], sa_ref[...], sb_ref[...])
    for hd in range(N_HEADS):
        q_ref[:, hd * LANES:(hd + 1) * LANES] = (rot[hd] * (HEAD_DIM ** -0.5)).astype(q_ref.dtype)
    kt = lax.dot_general(wkt_ref[...], h, (((1,), (1,)), ((), ())), preferred_element_type=F32)
    ct, st = ct_ref[...], st_ref[...]
    half = ROT_DIM // 2
    for blk in range(2 * N_HEADS):
        base = blk * HEAD_DIM
        x1 = kt[base:base + half]
        x2 = kt[base + half:base + ROT_DIM]
        full = jnp.concatenate([x1 * ct - x2 * st, x2 * ct + x1 * st, kt[base + ROT_DIM:base + HEAD_DIM]], axis=0)
        kt_ref[base:base + HEAD_DIM, :] = full
        ktb_ref[base:base + HEAD_DIM, :] = full.astype(ktb_ref.dtype)
    v = _bdot(h, wv_ref[...])
    v_ref[...] = v
    vb_ref[...] = v.astype(vb_ref.dtype)


def _qkv_t_layer(st, x, mods, ng, wq, wkt, wv, tables, angles_t):
    tpb = st.tiles_per_batch
    tm = st.tile
    tab = pl.BlockSpec((tm, LANES), lambda t: (t % tpb, 0))
    tab_t = pl.BlockSpec((ROT_DIM // 2, tm), lambda t: (0, t % tpb))
    w = _full_spec((D_MODEL, D_MODEL))
    return pl.pallas_call(
        _qkv_t_kernel,
        grid=(st.grid,),
        in_specs=[st.rows_spec(), st.mod_spec(0), st.mod_spec(1), _full_spec((1, D_MODEL)), w, w, w,
                  tab, tab, tab, tab_t, tab_t],
        out_specs=[st.rows_spec(),
                   pl.BlockSpec((None, D_MODEL, tm), lambda t: (t // tpb, 0, t % tpb)),
                   pl.BlockSpec((None, D_MODEL, tm), lambda t: (t // tpb, 0, t % tpb)),
                   st.rows_spec(), st.rows_spec()],
        out_shape=[jax.ShapeDtypeStruct((st.rows, D_MODEL), BF16),
                   jax.ShapeDtypeStruct((BATCH, D_MODEL, SEQ), F32),
                   jax.ShapeDtypeStruct((BATCH, D_MODEL, SEQ), BF16),
                   jax.ShapeDtypeStruct((st.rows, D_MODEL), F32),
                   jax.ShapeDtypeStruct((st.rows, D_MODEL), BF16)],
        compiler_params=_params("arbitrary"),
        name="qkv_rope_prompt",
    )(x, mods, mods, ng, wq, wkt, wv, *tables, *angles_t)


def _lambda(lq1_ref, lk1_ref, lq2_ref, lk2_ref, lambda_init):
    a = jnp.sum(lq1_ref[...] * lk1_ref[...], axis=-1, keepdims=True)
    b = jnp.sum(lq2_ref[...] * lk2_ref[...], axis=-1, keepdims=True)
    return jnp.exp(a) - jnp.exp(b) + lambda_init


def _attn_kernel(q_ref, kt_ref, v_ref, lq1_ref, lk1_ref, lq2_ref, lk2_ref, sg_ref, o_ref, *, lambda_init):
    tq = q_ref.shape[0]
    qi = pl.program_id(2)
    q = q_ref[...]
    first = lax.broadcasted_iota(jnp.int32, (1, LANES), 1) < HEAD_DIM
    zero = jnp.zeros_like(q)
    qs = (jnp.where(first, q, zero), jnp.where(first, zero, q))
    lam = _lambda(lq1_ref, lk1_ref, lq2_ref, lk2_ref, lambda_init)

    def query_tile(n):
        tv = (n + 1) * tq
        past = n * tq
        visible = (lax.broadcasted_iota(jnp.int32, (tq, tq), 1)
                   <= lax.broadcasted_iota(jnp.int32, (tq, tq), 0))
        es, totals = [], []
        for comp in range(2):
            s = _bdot(qs[comp], kt_ref[:, :tv])
            s_diag = jnp.where(visible, s[:, past:], -jnp.inf)
            m = jnp.max(s_diag, axis=-1, keepdims=True)
            if n:
                m = jnp.maximum(m, jnp.max(s[:, :past], axis=-1, keepdims=True))
            e = jnp.exp(s_diag - m)
            total = jnp.sum(e, axis=-1, keepdims=True)
            if n:
                e_past = jnp.exp(s[:, :past] - m)
                total = total + jnp.sum(e_past, axis=-1, keepdims=True)
                e = jnp.concatenate([e_past, e], axis=1)
            es.append(e)
            totals.append(total)
        a = (es[0] * (1.0 / totals[0]) - es[1] * (lam / totals[1])).astype(BF16)
        o = _bdot(a, v_ref[:tv, :])
        o_ref[...] = (_rms(o) * sg_ref[...] * (1.0 - lambda_init)).astype(o_ref.dtype)

    for n in range(SEQ // tq):
        pl.when(qi == n)(functools.partial(query_tile, n))


def _attn_prompt(q, ktb, vb, lq1, lk1, lq2, lk2, subln_g, lambda_init):
    nq = SEQ // ATTN_Q_TILE
    kern = functools.partial(_attn_kernel, lambda_init=lambda_init)
    vec = pl.BlockSpec((1, HEAD_DIM), lambda b, h, i: (0, 0))
    return pl.pallas_call(
        kern,
        grid=(BATCH, N_HEADS, nq),
        in_specs=[pl.BlockSpec((ATTN_Q_TILE, LANES), lambda b, h, i: (b * nq + i, h)),
                  pl.BlockSpec((None, LANES, SEQ), lambda b, h, i: (b, h, 0)),
                  pl.BlockSpec((SEQ, LANES), lambda b, h, i: (b, h)),
                  vec, vec, vec, vec,
                  pl.BlockSpec((1, 2 * HEAD_DIM), lambda b, h, i: (0, 0))],
        out_specs=pl.BlockSpec((ATTN_Q_TILE, LANES), lambda b, h, i: (b * nq + i, h)),
        out_shape=jax.ShapeDtypeStruct((N_PROMPT, D_MODEL), BF16),
        compiler_params=_params("arbitrary", "arbitrary", "arbitrary"),
        name="diff_attn_prompt",
    )(q, ktb, vb, lq1, lk1, lq2, lk2, subln_g)


DEC_ROWS = N_HEADS * 2 * DEC_SEQ


def _decode_kernel(pt_ref, q_ref, kn_ref, vn_ref, lq1_ref, lk1_ref, lq2_ref, lk2_ref, sg_ref, *rest,
                   lambda_init):
    k_refs = rest[:PAGES_PER_STEP]
    v_refs = rest[PAGES_PER_STEP:2 * PAGES_PER_STEP]
    o_ref, m_ref, l_ref, acc_ref = rest[2 * PAGES_PER_STEP:]
    step = pl.program_id(1)
    rows_per_head = 2 * DEC_SEQ
    lane = lax.broadcasted_iota(jnp.int32, (rows_per_head, D_MODEL), 1)
    comp = _div_pow2(lax.broadcasted_iota(jnp.int32, (rows_per_head, D_MODEL), 0), DEC_SEQ)
    q8 = q_ref[...] * (HEAD_DIM ** -0.5)
    blk = _div_pow2(lane, HEAD_DIM)
    qbd = jnp.concatenate([jnp.where(blk == 2 * hd + comp, q8, 0.0) for hd in range(N_HEADS)],
                          axis=0).astype(BF16)

    @pl.when(step == 0)
    def _():
        m_ref[...] = jnp.full(m_ref.shape, -jnp.inf, F32)
        l_ref[...] = jnp.zeros(l_ref.shape, F32)
        acc_ref[...] = jnp.zeros(acc_ref.shape, F32)

    m, l, acc = m_ref[...], l_ref[...], acc_ref[...]
    s = jnp.concatenate([_bdot(qbd, kr[...].astype(BF16)) for kr in k_refs], axis=1)
    m_new = jnp.maximum(m, jnp.max(s, axis=-1, keepdims=True))
    alpha = jnp.exp(m - m_new)
    p = jnp.exp(s - m_new)
    l = l * alpha + jnp.sum(p, axis=-1, keepdims=True)
    pv = []
    for hd in range(N_HEADS):
        rows = slice(hd * rows_per_head, (hd + 1) * rows_per_head)
        v_h = jnp.concatenate([vr[pl.ds(hd, PAGE_SIZE, stride=N_HEADS), :].astype(BF16) for vr in v_refs],
                              axis=0)
        pv.append(_bdot(p[rows].astype(BF16), v_h))
    acc = acc * alpha + jnp.concatenate(pv, axis=0)
    m = m_new
    m_ref[...], l_ref[...], acc_ref[...] = m, l, acc

    @pl.when(step == pl.num_programs(1) - 1)
    def _():
        kn = kn_ref[...].astype(BF16).astype(F32)
        vn = vn_ref[...].astype(BF16).astype(F32)
        qf = qbd.astype(F32)
        row_q = _mod_pow2(lax.broadcasted_iota(jnp.int32, (DEC_ROWS, 1), 0), DEC_SEQ)
        s_new = []
        for j in range(DEC_SEQ):
            sj = jnp.sum(qf * kn[j:j + 1, :], axis=-1, keepdims=True)
            s_new.append(jnp.where(row_q >= j, sj, -jnp.inf))
        m2 = m
        for sj in s_new:
            m2 = jnp.maximum(m2, sj)
        alpha = jnp.exp(m - m2)
        l2 = l * alpha
        acc2 = acc * alpha
        for j, sj in enumerate(s_new):
            pj = jnp.exp(sj - m2)
            l2 = l2 + pj
            v_rows = jnp.concatenate(
                [jnp.broadcast_to(vn[j:j + 1, hd * LANES:(hd + 1) * LANES], (rows_per_head, LANES))
                 for hd in range(N_HEADS)], axis=0)
            acc2 = acc2 + pj.astype(BF16).astype(F32) * v_rows
        outn = acc2 / l2
        lam = _lambda(lq1_ref, lk1_ref, lq2_ref, lk2_ref, lambda_init)
        for hd in range(N_HEADS):
            r0 = hd * rows_per_head
            o = outn[r0:r0 + DEC_SEQ] - lam * outn[r0 + DEC_SEQ:r0 + rows_per_head]
            o_ref[:, hd * LANES:(hd + 1) * LANES] = _rms(o) * sg_ref[...] * (1.0 - lambda_init)


def _attn_decode(page_table, q8, k_new, v_new, cache_kt, cache_v, layer, lq1, lk1, lq2, lk2, subln_g, lambda_init):
    n_steps = N_PAGES // PAGES_PER_STEP
    kern = functools.partial(_decode_kernel, lambda_init=lambda_init)
    tok = pl.BlockSpec((None, DEC_SEQ, D_MODEL), lambda b, p, pt: (b, 0, 0))
    vec = pl.BlockSpec((1, HEAD_DIM), lambda b, p, pt: (0, 0))

    def page_spec(i):
        return pl.BlockSpec((None, None, N_HEADS * PAGE_SIZE, LANES),
                            lambda b, p, pt: (pt[b * N_PAGES + p * PAGES_PER_STEP + i], layer, 0, 0))

    grid_spec = pltpu.PrefetchScalarGridSpec(
        num_scalar_prefetch=1,
        grid=(DEC_BATCH, n_steps),
        in_specs=[pl.BlockSpec((None, 2 * DEC_SEQ, D_MODEL), lambda b, p, pt: (b, 0, 0)), tok, tok,
                  vec, vec, vec, vec, pl.BlockSpec((1, 2 * HEAD_DIM), lambda b, p, pt: (0, 0))]
                 + [page_spec(i) for i in range(PAGES_PER_STEP)] * 2,
        out_specs=tok,
        scratch_shapes=[pltpu.VMEM((DEC_ROWS, 1), F32), pltpu.VMEM((DEC_ROWS, 1), F32),
                        pltpu.VMEM((DEC_ROWS, LANES), F32)],
    )
    return pl.pallas_call(
        kern,
        grid_spec=grid_spec,
        out_shape=jax.ShapeDtypeStruct((DEC_BATCH, DEC_SEQ, D_MODEL), F32),
        compiler_params=_params("arbitrary", "arbitrary"),
        name="diff_attn_decode",
    )(page_table.reshape(-1), q8, k_new, v_new, lq1, lk1, lq2, lk2, subln_g,
      *([cache_kt] * PAGES_PER_STEP), *([cache_v] * PAGES_PER_STEP))


def _outproj_kernel(o_ref, x_ref, gt_ref, w_ref, xo_ref):
    xo_ref[...] = x_ref[...] + gt_ref[...] * _bdot(o_ref[...].astype(BF16), w_ref[...])


def _outproj_layer(st, o, x, mods, w_o):
    return pl.pallas_call(
        _outproj_kernel,
        grid=(st.grid,),
        in_specs=[st.rows_spec(), st.rows_spec(), st.mod_spec(2), _full_spec((D_MODEL, D_MODEL))],
        out_specs=st.rows_spec(),
        out_shape=jax.ShapeDtypeStruct((st.rows, D_MODEL), F32),
        compiler_params=_params("arbitrary"),
        name="attn_out_proj",
    )(o, x, mods, w_o)


N_ALL = N_PROMPT + N_SAMPLE
ROW_W = D_MODEL + LANES
PAIRS_PER_GROUP = EXPERTS_PER_GROUP * (EXPERTS_PER_GROUP - 1) // 2
N_BUCKETS = N_EXPERT_GROUPS * PAIRS_PER_GROUP
SORT_TILE = 256
N_SORT_TILES = (N_ALL + N_BUCKETS * (SORT_TILE - 1)) // SORT_TILE
N_SORT_ROWS = N_SORT_TILES * SORT_TILE
DMA_UNROLL = 8


def _route_store(x, route_in_refs, route_ref, gates_ref):
    sh_ref, sc_ref, ng_ref, rwt_ref, rb_ref = route_in_refs
    tm = x.shape[0]
    h = _modulate(x, ng_ref[...], sh_ref[...], sc_ref[...])
    logits = lax.dot_general(rwt_ref[...], h, (((1,), (1,)), ((), ())), preferred_element_type=F32,
                             precision=lax.Precision.HIGHEST)
    scores = jax.nn.sigmoid(logits)
    sel = scores + rb_ref[...]
    eid = lax.broadcasted_iota(jnp.int32, sel.shape, 0)
    grp = _div_pow2(eid, EXPERTS_PER_GROUP)
    neg = -jnp.inf

    def first_max(vals):
        m = jnp.max(vals, axis=0, keepdims=True)
        idx = jnp.min(jnp.where(vals == m, eid, N_EXPERTS), axis=0, keepdims=True)
        return m, idx

    best = None
    for g in range(N_EXPERT_GROUPS):
        vg = jnp.where(grp == g, sel, neg)
        m1, i1 = first_max(vg)
        m2, _ = first_max(jnp.where(eid == i1, neg, vg))
        gs = m1 + m2
        if best is None:
            best, best_score = jnp.zeros_like(i1), gs
        else:
            better = gs > best_score
            best = jnp.where(better, g, best)
            best_score = jnp.where(better, gs, best_score)
    masked = jnp.where(grp == best, sel, neg)
    _, ia = first_max(masked)
    _, ib = first_max(jnp.where(eid == ia, neg, masked))
    ga = jnp.sum(jnp.where(eid == ia, scores, 0.0), axis=0, keepdims=True)
    gb = jnp.sum(jnp.where(eid == ib, scores, 0.0), axis=0, keepdims=True)
    den = ga + gb
    a_first = ia < ib
    lo = jnp.where(a_first, ia, ib).astype(F32)
    hi = jnp.where(a_first, ib, ia).astype(F32)
    g_lo = jnp.where(a_first, ga, gb) / den
    g_hi = jnp.where(a_first, gb, ga) / den
    r8 = lax.broadcasted_iota(jnp.int32, route_ref.shape, 0)
    route_ref[...] = jnp.where(r8 == 0, lo, jnp.where(r8 == 1, hi, 0.0))
    r128 = lax.broadcasted_iota(jnp.int32, (LANES, tm), 0)
    gates_t = jnp.where(r128 == 0, g_lo, jnp.where(r128 == 1, g_hi, 0.0))
    gates_ref[...] = gates_t.T


ROUTE_ROWS = 8


class _Routing:
    def __init__(self, st, mods, ng, router_wt, router_b):
        self.args = [mods, mods, ng, router_wt, router_b]
        self.in_specs = [st.mod_spec(3), st.mod_spec(4), _full_spec((1, D_MODEL)),
                         _full_spec((N_EXPERTS, D_MODEL)), _full_spec((N_EXPERTS, 1))]
        self.out_specs = [pl.BlockSpec((ROUTE_ROWS, st.tile), lambda i: (0, i)), st.rows_spec(LANES)]
        self.out_shape = [jax.ShapeDtypeStruct((ROUTE_ROWS, st.rows), F32),
                          jax.ShapeDtypeStruct((st.rows, LANES), F32)]


N_ROUTE_IN = 5


def _router_kernel(x_ref, *refs):
    _route_store(x_ref[...], refs[:N_ROUTE_IN], *refs[N_ROUTE_IN:])


def _router_layer(st, x, routing):
    return pl.pallas_call(
        _router_kernel,
        grid=(st.grid,),
        in_specs=[st.rows_spec()] + routing.in_specs,
        out_specs=routing.out_specs,
        out_shape=routing.out_shape,
        compiler_params=_params("arbitrary"),
        name="moe_router",
    )(x, *routing.args)


def _moe_plan(route):
    lo = route[0].astype(jnp.int32)
    hi = route[1].astype(jnp.int32)
    i, j = lo % EXPERTS_PER_GROUP, hi % EXPERTS_PER_GROUP
    pair = i * (2 * EXPERTS_PER_GROUP - 1 - i) // 2 + (j - i - 1)
    bucket = (lo // EXPERTS_PER_GROUP) * PAIRS_PER_GROUP + pair
    onehot = (bucket[:, None] == jnp.arange(N_BUCKETS)[None, :]).astype(jnp.int32)
    csum = jnp.cumsum(onehot, axis=0)
    counts = csum[-1]
    rank = jnp.sum(onehot * csum, axis=1) - 1
    tiles = (counts + SORT_TILE - 1) // SORT_TILE
    ends = jnp.cumsum(tiles)
    starts = ends - tiles
    n_used = ends[-1]
    pos = jnp.sum(onehot * starts[None, :], axis=1) * SORT_TILE + rank
    tile_id = jnp.arange(N_SORT_TILES)
    tile_bucket = jnp.sum((jnp.minimum(tile_id, n_used - 1)[:, None] >= ends[None, :]).astype(jnp.int32), axis=1)
    tile_bucket = jnp.minimum(tile_bucket, N_BUCKETS - 1)
    pair_lo = jnp.array([a for a in range(EXPERTS_PER_GROUP) for _ in range(a + 1, EXPERTS_PER_GROUP)], jnp.int32)
    pair_hi = jnp.array([b for a in range(EXPERTS_PER_GROUP) for b in range(a + 1, EXPERTS_PER_GROUP)], jnp.int32)
    base = (tile_bucket // PAIRS_PER_GROUP) * EXPERTS_PER_GROUP
    tile_lo = base + pair_lo[tile_bucket % PAIRS_PER_GROUP]
    tile_hi = base + pair_hi[tile_bucket % PAIRS_PER_GROUP]
    return pos.astype(jnp.int32), tile_lo, tile_hi, n_used.reshape(1).astype(jnp.int32)


def _bucket_kernel(pos_ref, x_ref, sh_ref, sc_ref, ng_ref, gates_ref, dst_in_ref, dst_ref, buf, sem, *, row0):
    del dst_in_ref
    tm = x_ref.shape[0]
    t = pl.program_id(0)
    nt = pl.num_programs(0)
    slot = t % 2

    def wait(s):
        pltpu.make_async_copy(buf.at[s], dst_ref.at[pl.ds(0, tm), :], sem.at[s]).wait()

    @pl.when(t >= 2)
    def _():
        wait(slot)

    buf[slot, :, :D_MODEL] = _modulate(x_ref[...], ng_ref[...], sh_ref[...], sc_ref[...])
    buf[slot, :, D_MODEL:] = gates_ref[...]
    base = row0 + t * tm

    def body(i, carry):
        for u in range(DMA_UNROLL):
            r = i * DMA_UNROLL + u
            pltpu.make_async_copy(buf.at[slot, pl.ds(r, 1), :],
                                  dst_ref.at[pl.ds(pos_ref[base + r], 1), :], sem.at[slot]).start()
        return carry

    lax.fori_loop(0, tm // DMA_UNROLL, body, 0)

    @pl.when(t == nt - 1)
    def _():
        wait(slot)

    @pl.when((t == nt - 1) & (nt > 1))
    def _():
        wait(1 - slot)


def _bucket_rows(st, pos, x, mods, ng, gates, dst, row0):
    kern = functools.partial(_bucket_kernel, row0=row0)
    if st.per_row_mods:
        def mod_spec(k):
            return pl.BlockSpec((st.tile, D_MODEL), lambda i, p: (i, k))
    else:
        tpb = st.tiles_per_batch

        def mod_spec(k):
            return pl.BlockSpec((None, None, 1, D_MODEL), lambda i, p: (i // tpb, k, 0, 0))
    hbm = pl.BlockSpec(memory_space=pl.ANY)
    grid_spec = pltpu.PrefetchScalarGridSpec(
        num_scalar_prefetch=1,
        grid=(st.grid,),
        in_specs=[pl.BlockSpec((st.tile, D_MODEL), lambda i, p: (i, 0)), mod_spec(3), mod_spec(4),
                  pl.BlockSpec((1, D_MODEL), lambda i, p: (0, 0)),
                  pl.BlockSpec((st.tile, LANES), lambda i, p: (i, 0)), hbm],
        out_specs=hbm,
        scratch_shapes=[pltpu.VMEM((2, st.tile, ROW_W), F32), pltpu.SemaphoreType.DMA((2,))],
    )
    return pl.pallas_call(
        kern,
        grid_spec=grid_spec,
        out_shape=jax.ShapeDtypeStruct(dst.shape, dst.dtype),
        input_output_aliases={6: 0},
        compiler_params=_params("arbitrary"),
        name="moe_bucket_rows",
    )(pos, x, mods, mods, ng, gates, dst)


def _moe_kernel(lo_ref, hi_ref, nu_ref, xs_ref, wg_lo, wu_lo, wd_lo, wg_hi, wu_hi, wd_hi, ys_ref):
    del lo_ref, hi_ref
    t = pl.program_id(0)

    @pl.when(t < nu_ref[0])
    def _():
        row = xs_ref[...]
        x = row[:, :D_MODEL].astype(BF16)

        def hidden(wg, wu, gate):
            g = _bdot(x, wg[...])
            u = _bdot(x, wu[...])
            return (jax.nn.silu(g) * u * gate).astype(BF16)

        a_lo = hidden(wg_lo, wu_lo, row[:, D_MODEL:D_MODEL + 1])
        a_hi = hidden(wg_hi, wu_hi, row[:, D_MODEL + 1:D_MODEL + 2])
        a = jnp.concatenate([a_lo, a_hi], axis=1)
        wd = jnp.concatenate([wd_lo[...], wd_hi[...]], axis=0)
        ys_ref[...] = _bdot(a, wd)

    @pl.when(t >= nu_ref[0])
    def _():
        ys_ref[...] = jnp.zeros(ys_ref.shape, ys_ref.dtype)


def _moe_experts(xs, tile_lo, tile_hi, n_used, wg, wu, wd, layer):
    def w_spec(shape, which):
        return pl.BlockSpec((None, None) + shape,
                            lambda t, lo, hi, nu: (layer, (lo, hi)[which][t], 0, 0))

    up, down = (D_MODEL, D_EXPERT), (D_EXPERT, D_MODEL)
    grid_spec = pltpu.PrefetchScalarGridSpec(
        num_scalar_prefetch=3,
        grid=(N_SORT_TILES,),
        in_specs=[pl.BlockSpec((SORT_TILE, ROW_W), lambda t, lo, hi, nu: (t, 0)),
                  w_spec(up, 0), w_spec(up, 0), w_spec(down, 0),
                  w_spec(up, 1), w_spec(up, 1), w_spec(down, 1)],
        out_specs=pl.BlockSpec((SORT_TILE, D_MODEL), lambda t, lo, hi, nu: (t, 0)),
    )
    return pl.pallas_call(
        _moe_kernel,
        grid_spec=grid_spec,
        out_shape=jax.ShapeDtypeStruct((N_SORT_ROWS, D_MODEL), F32),
        compiler_params=_params("arbitrary"),
        name="moe_experts",
    )(tile_lo, tile_hi, n_used, xs, wg, wu, wd, wg, wu, wd)


def _unsort_kernel(pos_ref, ys_ref, x_ref, g2_ref, fg_ref, o_ref, buf, sem, *, row0, final):
    tm = x_ref.shape[0]
    t = pl.program_id(0)

    def fetch(tile, slot):
        base = row0 + tile * tm

        def body(i, carry):
            for u in range(DMA_UNROLL):
                r = i * DMA_UNROLL + u
                pltpu.make_async_copy(ys_ref.at[pl.ds(pos_ref[base + r], 1), :],
                                      buf.at[slot, pl.ds(r, 1), :], sem.at[slot]).start()
            return carry

        lax.fori_loop(0, tm // DMA_UNROLL, body, 0)

    @pl.when(t == 0)
    def _():
        fetch(0, 0)

    @pl.when(t + 1 < pl.num_programs(0))
    def _():
        fetch(t + 1, (t + 1) % 2)

    slot = t % 2
    pltpu.make_async_copy(ys_ref.at[pl.ds(0, tm), :], buf.at[slot], sem.at[slot]).wait()
    y = x_ref[...] + g2_ref[...] * buf[slot]
    o_ref[...] = _rms(y) * fg_ref[...] if final else y


def _unsort_layer(st, pos, ys, x, mods, row0, final_g, final):
    kern = functools.partial(_unsort_kernel, row0=row0, final=final)
    if st.per_row_mods:
        g2_spec = pl.BlockSpec((st.tile, D_MODEL), lambda i, p: (i, 5))
    else:
        tpb = st.tiles_per_batch
        g2_spec = pl.BlockSpec((None, None, 1, D_MODEL), lambda i, p: (i // tpb, 5, 0, 0))
    rows = pl.BlockSpec((st.tile, D_MODEL), lambda i, p: (i, 0))
    grid_spec = pltpu.PrefetchScalarGridSpec(
        num_scalar_prefetch=1,
        grid=(st.grid,),
        in_specs=[pl.BlockSpec(memory_space=pl.ANY), rows, g2_spec,
                  pl.BlockSpec((1, D_MODEL), lambda i, p: (0, 0))],
        out_specs=rows,
        scratch_shapes=[pltpu.VMEM((2, st.tile, D_MODEL), F32), pltpu.SemaphoreType.DMA((2,))],
    )
    return pl.pallas_call(
        kern,
        grid_spec=grid_spec,
        out_shape=jax.ShapeDtypeStruct((st.rows, D_MODEL), F32),
        compiler_params=_params("arbitrary"),
        name="moe_unsort_residual",
    )(pos, ys, x, mods, final_g)


def kernel(x_prompt, x_sample, cache_k, cache_v, state_pool, page_table, c_prompt, c_sample,
           norm1_g, norm2_g, ada_w, ada_b, a_w_in, a_ln_g, a_ln_b, a_w_s, a_b_s, a_w_out,
           b_w_grp, b_scale, c_w_qkv, c_lq1, c_lk1, c_lq2, c_lk2, c_subln_g, c_w_o,
           router_w, router_bias, e_w_gate, e_w_up, e_w_down, final_g):
    sp = _Stream(N_PROMPT, PROMPT_TILE, SEQ // PROMPT_TILE, per_row_mods=False)
    ss = _Stream(N_SAMPLE, N_SAMPLE, 1, per_row_mods=True)
    sr = _Stream(N_PROMPT, ROUTER_TILE, SEQ // ROUTER_TILE, per_row_mods=False)

    xp = x_prompt.reshape(N_PROMPT, D_MODEL)
    xs = x_sample.reshape(N_SAMPLE, D_MODEL)
    mods_all = _ada(jnp.concatenate([c_prompt, c_sample], axis=0), ada_w, ada_b)

    wg_b, wu_b, wd_b = e_w_gate.astype(BF16), e_w_up.astype(BF16), e_w_down.astype(BF16)
    rwt = router_w.T
    rb = router_bias.reshape(N_EXPERTS, 1)
    fg = final_g.reshape(1, D_MODEL)
    bucketed = jnp.zeros((N_SORT_ROWS, ROW_W), F32)
    n_phys = cache_k.shape[0]
    n_c_layers = cache_k.shape[1]
    ckt = jnp.transpose(cache_k, (0, 1, 3, 4, 5, 2)).reshape(n_phys, n_c_layers, D_MODEL, PAGE_SIZE)
    cv = cache_v.reshape(n_phys, n_c_layers, PAGE_SIZE * N_HEADS, 2 * HEAD_DIM)

    av_p, av_s, pr_p, pr_s, kp_l, vp_l, ks_l, vs_l = [], [], [], [], [], [], [], []
    for i in range(DEPTH):
        kind, j = i % N_MIXERS, i // N_MIXERS
        mp = mods_all[i, :BATCH].reshape(BATCH, N_MODS, 1, D_MODEL)
        ms = jnp.repeat(mods_all[i, BATCH:], DEC_SEQ, axis=0)
        ng1 = norm1_g[i].reshape(1, D_MODEL)
        ng2 = norm2_g[i].reshape(1, D_MODEL)
        if kind == 0:
            w_in = a_w_in[j].astype(BF16)
            w_out = a_w_out[j].astype(BF16)
            ln_g, ln_b = a_ln_g[j].reshape(1, D_GATE), a_ln_b[j].reshape(1, D_GATE)
            bias = jnp.repeat(a_b_s[j].T, D_GATE // A_GROUPS, axis=1)
            xp, vrow_p = _gmlp_layer(sp, xp, mp, ng1, w_in, ln_g, ln_b, a_w_s[j], bias, w_out, CHUNK, BATCH, False)
            reps = CHUNK // DEC_SEQ
            wmix_s = jnp.tile(a_w_s[j][:, :DEC_SEQ, :DEC_SEQ], (1, reps, reps))
            bias_s = jnp.tile(bias[:DEC_SEQ], (reps, 1))
            xs, vrow_s = _gmlp_layer(ss, xs, ms, ng1, a_w_in[j], ln_g, ln_b, wmix_s, bias_s, a_w_out[j],
                                     DEC_SEQ, 1, True)
            av_p.append(vrow_p.reshape(BATCH, CHUNK, D_GATE))
            av_s.append(vrow_s.reshape(DEC_BATCH, DEC_SEQ, D_GATE))
        elif kind == 1:
            w_grp = b_w_grp[j].astype(BF16)
            scale = b_scale[j].reshape(1, D_MODEL)
            hb = PROMPT_TILE // POOL_HALO
            halo_spec = pl.BlockSpec((POOL_HALO, D_MODEL), lambda t: (jnp.maximum(t * hb - 1, 0), 0))
            xp, rows_p = _pool_layer(sp, xp, xp, halo_spec, mp, ng1, w_grp, scale, False, 0, POOL_HALO, BATCH)
            pr_p.append(rows_p.reshape(BATCH, POOL_HALO, D_MODEL)[:, 1:])
            pad_t = 2 * DEC_SEQ
            s8 = _Stream(DEC_BATCH * pad_t, pad_t, 1, per_row_mods=True)
            xs8 = jnp.pad(xs.reshape(DEC_BATCH, DEC_SEQ, D_MODEL), ((0, 0), (0, DEC_SEQ), (0, 0)))
            ms8 = jnp.repeat(mods_all[i, BATCH:], pad_t, axis=0)
            prev = jnp.pad(state_pool[:, j], ((0, 0), (1, 0), (0, 0))).reshape(DEC_BATCH * POOL_HALO, D_MODEL)
            prev_spec = pl.BlockSpec((POOL_HALO, D_MODEL), lambda t: (t, 0))
            n_out = POOL_HALO + pad_t
            xs8, rows_s = _pool_layer(s8, xs8.reshape(-1, D_MODEL), prev, prev_spec, ms8, ng1, w_grp, scale,
                                      True, PAST_LEN, n_out, DEC_BATCH)
            xs = xs8.reshape(DEC_BATCH, pad_t, D_MODEL)[:, :DEC_SEQ].reshape(N_SAMPLE, D_MODEL)
            lo = 1 + DEC_SEQ
            pr_s.append(rows_s.reshape(DEC_BATCH, n_out, D_MODEL)[:, lo:lo + POOL_BUF])
        else:
            lambda_init = 0.8 - 0.6 * math.exp(-0.3 * i)
            w_qkv = c_w_qkv[j].astype(BF16)
            w_o = c_w_o[j].astype(BF16)
            lq1, lk1 = c_lq1[j].reshape(1, HEAD_DIM), c_lk1[j].reshape(1, HEAD_DIM)
            lq2, lk2 = c_lq2[j].reshape(1, HEAD_DIM), c_lk2[j].reshape(1, HEAD_DIM)
            sg = c_subln_g[j].reshape(1, 2 * HEAD_DIM)
            pos_p = jnp.arange(SEQ)
            cos_p, sin_p = _rope_angles(pos_p)
            qp, kt, ktb, vp, vpb = _qkv_t_layer(
                sp, xp, mp, ng1, w_qkv[:, :D_MODEL], w_qkv[:, D_MODEL:2 * D_MODEL].T, w_qkv[:, 2 * D_MODEL:],
                _rope_tables(pos_p), (cos_p.T, sin_p.T))
            op = _attn_prompt(qp, ktb, vpb, lq1, lk1, lq2, lk2, sg, lambda_init)
            xp = _outproj_layer(sp, op, xp, mp, w_o)
            tab_s = _rope_tables(jnp.tile(PAST_LEN + jnp.arange(DEC_SEQ), DEC_BATCH))
            qs, ks, vs = _qkv_layer(ss, xs, ms, ng1, w_qkv, tab_s,
                                    pl.BlockSpec((N_SAMPLE, LANES), lambda t: (0, 0)))
            tok = (DEC_BATCH, DEC_SEQ, D_MODEL)
            q8 = jnp.concatenate([qs.reshape(tok)] * 2, axis=1)
            os_ = _attn_decode(page_table, q8, ks.reshape(tok), vs.reshape(tok), ckt, cv, j,
                               lq1, lk1, lq2, lk2, sg, lambda_init)
            xs = _outproj_layer(ss, os_.reshape(N_SAMPLE, D_MODEL), xs, ms, w_o)
            kp = jnp.transpose(kt.reshape(BATCH, N_HEADS, 2, HEAD_DIM, SEQ), (0, 4, 1, 2, 3))
            kp_l.append(kp)
            vp_l.append(vp.reshape(BATCH, SEQ, N_HEADS, 2 * HEAD_DIM))
            ks_l.append(ks.reshape(DEC_BATCH, DEC_SEQ, N_HEADS, 2, HEAD_DIM))
            vs_l.append(vs.reshape(DEC_BATCH, DEC_SEQ, N_HEADS, 2 * HEAD_DIM))

        route_p, gates_p = _router_layer(sr, xp, _Routing(sr, mp, ng2, rwt, rb))
        route_s, gates_s = _router_layer(ss, xs, _Routing(ss, ms, ng2, rwt, rb))
        pos, tile_lo, tile_hi, n_used = _moe_plan(jnp.concatenate([route_p, route_s], axis=1))
        bucketed = _bucket_rows(sr, pos, xp, mp, ng2, gates_p, bucketed, 0)
        bucketed = _bucket_rows(ss, pos, xs, ms, ng2, gates_s, bucketed, N_PROMPT)
        ys = _moe_experts(bucketed, tile_lo, tile_hi, n_used, wg_b, wu_b, wd_b, i)
        last = i == DEPTH - 1
        xp = _unsort_layer(sr, pos, ys, xp, mp, 0, fg, last)
        xs = _unsort_layer(ss, pos, ys, xs, ms, N_PROMPT, fg, last)

    y_prompt = xp.reshape(BATCH, SEQ, D_MODEL)
    y_sample = xs.reshape(DEC_BATCH, DEC_SEQ, D_MODEL)
    return (y_prompt, y_sample,
            jnp.stack(av_p, axis=1), jnp.stack(av_s, axis=1),
            jnp.stack(pr_p, axis=1), jnp.stack(pr_s, axis=1),
            jnp.stack(kp_l, axis=1), jnp.stack(vp_l, axis=1),
            jnp.stack(ks_l, axis=1), jnp.stack(vs_l, axis=1))
```

```python
import functools
import math

import jax
import jax.numpy as jnp
from jax import lax
from jax.experimental import pallas as pl
from jax.experimental.pallas import tpu as pltpu

F32 = jnp.float32
BF16 = jnp.bfloat16

D_MODEL = 1024
BATCH = 8
SEQ = 2048
DEPTH = 4
DEC_BATCH = 32
DEC_SEQ = 4
PAST_LEN = 8192
PAGE_SIZE = 128
N_PAGES = PAST_LEN // PAGE_SIZE
N_MIXERS = 3
CHUNK = 128
D_GATE = D_MODEL
A_GROUPS = 4
POOL_WINDOWS = (2, 4, 8, 16)
B_GROUPS = len(POOL_WINDOWS)
POOL_BUF = max(POOL_WINDOWS) - 1
POOL_HALO = POOL_BUF + 1
N_HEADS = 8
HEAD_DIM = D_MODEL // N_HEADS // 2
ROT_DIM = HEAD_DIM // 4
ROPE_THETA = 500000.0
N_EXPERTS = 16
N_EXPERT_GROUPS = 4
EXPERTS_PER_GROUP = N_EXPERTS // N_EXPERT_GROUPS
D_EXPERT = 512
EPS = 1e-6
N_MODS = 6

N_PROMPT = BATCH * SEQ
N_SAMPLE = DEC_BATCH * DEC_SEQ
LANES = 128
VMEM_LIMIT = 56 * 1024 * 1024

PROMPT_TILE = 512
ROUTER_TILE = 512
ATTN_Q_TILE = 512
PAGES_PER_STEP = 16


def _params(*sem):
    return pltpu.CompilerParams(dimension_semantics=sem, vmem_limit_bytes=VMEM_LIMIT)


class _Stream:
    def __init__(self, rows, tile, tiles_per_batch, per_row_mods):
        self.rows = rows
        self.tile = tile
        self.tiles_per_batch = tiles_per_batch
        self.per_row_mods = per_row_mods
        self.grid = rows // tile

    def rows_spec(self, width=D_MODEL):
        return pl.BlockSpec((self.tile, width), lambda i: (i, 0))

    def mod_spec(self, k):
        if self.per_row_mods:
            return pl.BlockSpec((self.tile, D_MODEL), lambda i: (i, k))
        tpb = self.tiles_per_batch
        return pl.BlockSpec((None, None, 1, D_MODEL), lambda i: (i // tpb, k, 0, 0))


def _full_spec(shape):
    nd = len(shape)
    return pl.BlockSpec(shape, lambda i: (0,) * nd)


def _rms(x):
    return x * lax.rsqrt(jnp.mean(x * x, axis=-1, keepdims=True) + EPS)


def _modulate(x, g, sh, sc):
    return _rms(x) * g * (1.0 + sc) + sh


def _bdot(a, b):
    return jnp.dot(a, b, preferred_element_type=F32)


def _div_pow2(x, n):
    assert n & (n - 1) == 0
    return lax.shift_right_logical(x, n.bit_length() - 1)


def _mod_pow2(x, n):
    assert n & (n - 1) == 0
    return x & (n - 1)


def _ada_kernel(c_ref, w_ref, b_ref, o_ref):
    o_ref[...] = jnp.dot(jax.nn.silu(c_ref[...]), w_ref[...], preferred_element_type=F32,
                         precision=lax.Precision.HIGHEST) + b_ref[...]


def _ada(c_all, ada_w, ada_b):
    nb = c_all.shape[0]
    tn = 1536
    return pl.pallas_call(
        _ada_kernel,
        grid=(DEPTH, N_MODS * D_MODEL // tn),
        in_specs=[pl.BlockSpec((nb, D_MODEL), lambda l, n: (0, 0)),
                  pl.BlockSpec((None, D_MODEL, tn), lambda l, n: (l, 0, n)),
                  pl.BlockSpec((None, 1, tn), lambda l, n: (l, 0, n))],
        out_specs=pl.BlockSpec((None, nb, tn), lambda l, n: (l, 0, n)),
        out_shape=jax.ShapeDtypeStruct((DEPTH, nb, N_MODS * D_MODEL), F32),
        compiler_params=_params("arbitrary", "arbitrary"),
        name="ada_mod",
    )(c_all, ada_w, ada_b.reshape(DEPTH, 1, N_MODS * D_MODEL))


def _mm(a, b, precise):
    if precise:
        return jnp.dot(a.astype(F32), b.astype(F32), preferred_element_type=F32, precision=lax.Precision.HIGHEST)
    return _bdot(a.astype(BF16), b.astype(BF16))


def _gmlp_kernel(x_ref, sh_ref, sc_ref, gt_ref, ng_ref, win_ref, lng_ref, lnb_ref, wmix_ref,
                 bmix_ref, wout_ref, xo_ref, v_ref, mix_ref, *, period, precise):
    tm = x_ref.shape[0]
    x = x_ref[...]
    h = _modulate(x, ng_ref[...], sh_ref[...], sc_ref[...])
    z = _mm(h, win_ref[...], precise)
    z = 0.5 * z * (1.0 + lax.erf(z * (2.0 ** -0.5)))
    u = z[:, :D_GATE]
    v = z[:, D_GATE:]
    mu = jnp.mean(v, axis=-1, keepdims=True)
    vc = v - mu
    var = jnp.mean(vc * vc, axis=-1, keepdims=True)
    v = vc * lax.rsqrt(var + EPS) * lng_ref[...] + lnb_ref[...]
    v_ref[...] = v[tm - CHUNK:, :]
    vb = v if precise else v.astype(BF16)
    t = lax.broadcasted_iota(jnp.int32, (CHUNK, CHUNK), 0)
    s = lax.broadcasted_iota(jnp.int32, (CHUNK, CHUNK), 1)
    keep = (s <= t) & (_div_pow2(t, period) == _div_pow2(s, period))
    cg = D_GATE // A_GROUPS
    for g in range(A_GROUPS):
        wm = jnp.where(keep, wmix_ref[g], 0.0)
        for c in range(tm // CHUNK):
            rows = slice(c * CHUNK, (c + 1) * CHUNK)
            cols = slice(g * cg, (g + 1) * cg)
            mix_ref[rows, cols] = _mm(wm, vb[rows, cols], precise) + bmix_ref[:, cols]
    y = _mm(u * mix_ref[...], wout_ref[...], precise)
    xo_ref[...] = x + gt_ref[...] * y


def _gmlp_layer(st, x, mods, ng, w_in, ln_g, ln_b, wmix, bmix, w_out, period, n_vblocks, precise):
    kern = functools.partial(_gmlp_kernel, period=period, precise=precise)
    vb_per = st.grid // n_vblocks
    return pl.pallas_call(
        kern,
        grid=(st.grid,),
        in_specs=[st.rows_spec(), st.mod_spec(0), st.mod_spec(1), st.mod_spec(2),
                  _full_spec((1, D_MODEL)), _full_spec((D_MODEL, 2 * D_GATE)),
                  _full_spec((1, D_GATE)), _full_spec((1, D_GATE)),
                  _full_spec((A_GROUPS, CHUNK, CHUNK)), _full_spec((CHUNK, D_GATE)),
                  _full_spec((D_GATE, D_MODEL))],
        out_specs=[st.rows_spec(),
                   pl.BlockSpec((CHUNK, D_GATE), lambda i: (i // vb_per, 0))],
        out_shape=[jax.ShapeDtypeStruct((st.rows, D_MODEL), F32),
                   jax.ShapeDtypeStruct((n_vblocks * CHUNK, D_GATE), F32)],
        scratch_shapes=[pltpu.VMEM((st.tile, D_GATE), F32)],
        compiler_params=_params("arbitrary"),
        name="gmlp_mixer",
    )(x, mods, mods, mods, ng, w_in, ln_g, ln_b, wmix, bmix, w_out)


def _pool_kernel(x_ref, halo_ref, sh_ref, sc_ref, gt_ref, ng_ref, wgrp_ref, scale_ref,
                 xo_ref, rows_ref, *, halo_is_h, tiles_per_batch, pos0, n_rows_out):
    tm = x_ref.shape[0]
    x = x_ref[...]
    ng, sh, sc = ng_ref[...], sh_ref[...], sc_ref[...]
    h = _modulate(x, ng, sh, sc)
    if halo_is_h:
        halo = halo_ref[...]
    else:
        sh_h = sh if sh.shape[0] == 1 else sh[:POOL_HALO]
        sc_h = sc if sc.shape[0] == 1 else sc[:POOL_HALO]
        first = (pl.program_id(0) % tiles_per_batch) == 0
        halo = jnp.where(first, 0.0, _modulate(halo_ref[...], ng, sh_h, sc_h))
    ext = jnp.concatenate([halo, h], axis=0)
    rows_ref[...] = ext[POOL_HALO + tm - n_rows_out:, :]
    t0 = (pl.program_id(0) % tiles_per_batch) * tm
    pos = pos0 + t0 + lax.broadcasted_iota(jnp.int32, (tm, 1), 0)
    cg = D_MODEL // B_GROUPS
    ys = []
    for g, w in enumerate(POOL_WINDOWS):
        cols = slice(g * cg, (g + 1) * cg)
        acc = ext[:, cols]
        span = 1
        while span < w:
            acc = acc + pltpu.roll(acc, span, 0)
            span *= 2
        cnt = jnp.minimum(pos + 1, w).astype(F32)
        pooled = acc[POOL_HALO:, :] / cnt - h[:, cols]
        ys.append(_bdot(pooled.astype(BF16), wgrp_ref[g]))
    y = jnp.concatenate(ys, axis=-1) * scale_ref[...]
    xo_ref[...] = x + gt_ref[...] * y


def _pool_layer(st, x, halo, halo_spec, mods, ng, w_grp, scale, halo_is_h, pos0, n_rows_out, n_batches):
    kern = functools.partial(_pool_kernel, halo_is_h=halo_is_h, tiles_per_batch=st.tiles_per_batch,
                             pos0=pos0, n_rows_out=n_rows_out)
    cg = D_MODEL // B_GROUPS
    tpb = st.tiles_per_batch
    return pl.pallas_call(
        kern,
        grid=(st.grid,),
        in_specs=[st.rows_spec(), halo_spec, st.mod_spec(0), st.mod_spec(1), st.mod_spec(2),
                  _full_spec((1, D_MODEL)), _full_spec((B_GROUPS, cg, cg)), _full_spec((1, D_MODEL))],
        out_specs=[st.rows_spec(),
                   pl.BlockSpec((n_rows_out, D_MODEL), lambda i: (i // tpb, 0))],
        out_shape=[jax.ShapeDtypeStruct((st.rows, D_MODEL), F32),
                   jax.ShapeDtypeStruct((n_batches * n_rows_out, D_MODEL), F32)],
        compiler_params=_params("arbitrary"),
        name="pool_mixer",
    )(x, halo, mods, mods, mods, ng, w_grp, scale)


def _rope_angles(pos):
    half = ROT_DIM // 2
    inv = ROPE_THETA ** (-(jnp.arange(half, dtype=F32) * 2.0) / ROT_DIM)
    ang = pos.astype(F32)[:, None] * inv[None, :]
    return jnp.cos(ang), jnp.sin(ang)


def _rope_tables(pos):
    cos, sin = _rope_angles(pos)
    n = pos.shape[0]
    pad = jnp.zeros((n, HEAD_DIM - ROT_DIM), F32)
    c = jnp.concatenate([cos, cos, pad + 1.0], axis=1)
    sa = jnp.concatenate([-sin, jnp.zeros_like(sin), pad], axis=1)
    sb = jnp.concatenate([jnp.zeros_like(sin), sin, pad], axis=1)
    return tuple(jnp.concatenate([t, t], axis=1) for t in (c, sa, sb))


def _rope_rows(x, c, sa, sb):
    half = ROT_DIM // 2
    out = []
    for hd in range(N_HEADS):
        xh = x[:, hd * LANES:(hd + 1) * LANES]
        out.append(xh * c + pltpu.roll(xh, LANES - half, 1) * sa + pltpu.roll(xh, half, 1) * sb)
    return out


def _qkv_kernel(x_ref, sh_ref, sc_ref, ng_ref, w_ref, c_ref, sa_ref, sb_ref, q_ref, k_ref, v_ref):
    h = _modulate(x_ref[...], ng_ref[...], sh_ref[...], sc_ref[...]).astype(BF16)
    qkv = _bdot(h, w_ref[...])
    c, sa, sb = c_ref[...], sa_ref[...], sb_ref[...]
    for which, ref in ((0, q_ref), (1, k_ref)):
        rot = _rope_rows(qkv[:, which * D_MODEL:(which + 1) * D_MODEL], c, sa, sb)
        for hd in range(N_HEADS):
            ref[:, hd * LANES:(hd + 1) * LANES] = rot[hd]
    v_ref[...] = qkv[:, 2 * D_MODEL:]


def _qkv_layer(st, x, mods, ng, w_qkv, tables, table_spec):
    return pl.pallas_call(
        _qkv_kernel,
        grid=(st.grid,),
        in_specs=[st.rows_spec(), st.mod_spec(0), st.mod_spec(1), _full_spec((1, D_MODEL)),
                  _full_spec((D_MODEL, 3 * D_MODEL)), table_spec, table_spec, table_spec],
        out_specs=[st.rows_spec(), st.rows_spec(), st.rows_spec()],
        out_shape=[jax.ShapeDtypeStruct((st.rows, D_MODEL), F32)] * 3,
        compiler_params=_params("arbitrary"),
        name="qkv_rope",
    )(x, mods, mods, ng, w_qkv, *tables)


def _qkv_t_kernel(x_ref, sh_ref, sc_ref, ng_ref, wq_ref, wkt_ref, wv_ref, c_ref, sa_ref, sb_ref,
                  ct_ref, st_ref, q_ref, kt_ref, ktb_ref, v_ref, vb_ref):
    h = _modulate(x_ref[...], ng_ref[...], sh_ref[...], sc_ref[...]).astype(BF16)
    rot = _rope_rows(_bdot(h, wq_ref[...]), c_ref[...], sa_ref[...], sb_ref[...])
    for hd in range(N_HEADS):
        q_ref[:, hd * LANES:(hd + 1) * LANES] = (rot[hd] * (HEAD_DIM ** -0.5)).astype(q_ref.dtype)
    kt = lax.dot_general(wkt_ref[...], h, (((1,), (1,)), ((), ())), preferred_element_type=F32)
    ct, st = ct_ref[...], st_ref[...]
    half = ROT_DIM // 2
    for blk in range(2 * N_HEADS):
        base = blk * HEAD_DIM
        x1 = kt[base:base + half]
        x2 = kt[base + half:base + ROT_DIM]
        full = jnp.concatenate([x1 * ct - x2 * st, x2 * ct + x1 * st, kt[base + ROT_DIM:base + HEAD_DIM]], axis=0)
        kt_ref[base:base + HEAD_DIM, :] = full
        ktb_ref[base:base + HEAD_DIM, :] = full.astype(ktb_ref.dtype)
    v = _bdot(h, wv_ref[...])
    v_ref[...] = v
    vb_ref[...] = v.astype(vb_ref.dtype)


def _qkv_t_layer(st, x, mods, ng, wq, wkt, wv, tables, angles_t):
    tpb = st.tiles_per_batch
    tm = st.tile
    tab = pl.BlockSpec((tm, LANES), lambda t: (t % tpb, 0))
    tab_t = pl.BlockSpec((ROT_DIM // 2, tm), lambda t: (0, t % tpb))
    w = _full_spec((D_MODEL, D_MODEL))
    return pl.pallas_call(
        _qkv_t_kernel,
        grid=(st.grid,),
        in_specs=[st.rows_spec(), st.mod_spec(0), st.mod_spec(1), _full_spec((1, D_MODEL)), w, w, w,
                  tab, tab, tab, tab_t, tab_t],
        out_specs=[st.rows_spec(),
                   pl.BlockSpec((None, D_MODEL, tm), lambda t: (t // tpb, 0, t % tpb)),
                   pl.BlockSpec((None, D_MODEL, tm), lambda t: (t // tpb, 0, t % tpb)),
                   st.rows_spec(), st.rows_spec()],
        out_shape=[jax.ShapeDtypeStruct((st.rows, D_MODEL), BF16),
                   jax.ShapeDtypeStruct((BATCH, D_MODEL, SEQ), F32),
                   jax.ShapeDtypeStruct((BATCH, D_MODEL, SEQ), BF16),
                   jax.ShapeDtypeStruct((st.rows, D_MODEL), F32),
                   jax.ShapeDtypeStruct((st.rows, D_MODEL), BF16)],
        compiler_params=_params("arbitrary"),
        name="qkv_rope_prompt",
    )(x, mods, mods, ng, wq, wkt, wv, *tables, *angles_t)


def _lambda(lq1_ref, lk1_ref, lq2_ref, lk2_ref, lambda_init):
    a = jnp.sum(lq1_ref[...] * lk1_ref[...], axis=-1, keepdims=True)
    b = jnp.sum(lq2_ref[...] * lk2_ref[...], axis=-1, keepdims=True)
    return jnp.exp(a) - jnp.exp(b) + lambda_init


def _attn_kernel(q_ref, kt_ref, v_ref, lq1_ref, lk1_ref, lq2_ref, lk2_ref, sg_ref, o_ref, *, lambda_init):
    tq = q_ref.shape[0]
    qi = pl.program_id(2)
    q = q_ref[...]
    first = lax.broadcasted_iota(jnp.int32, (1, LANES), 1) < HEAD_DIM
    zero = jnp.zeros_like(q)
    qs = (jnp.where(first, q, zero), jnp.where(first, zero, q))
    lam = _lambda(lq1_ref, lk1_ref, lq2_ref, lk2_ref, lambda_init)

    def query_tile(n):
        tv = (n + 1) * tq
        past = n * tq
        visible = (lax.broadcasted_iota(jnp.int32, (tq, tq), 1)
                   <= lax.broadcasted_iota(jnp.int32, (tq, tq), 0))
        es, totals = [], []
        for comp in range(2):
            s = _bdot(qs[comp], kt_ref[:, :tv])
            s_diag = jnp.where(visible, s[:, past:], -jnp.inf)
            m = jnp.max(s_diag, axis=-1, keepdims=True)
            if n:
                m = jnp.maximum(m, jnp.max(s[:, :past], axis=-1, keepdims=True))
            e = jnp.exp(s_diag - m)
            total = jnp.sum(e, axis=-1, keepdims=True)
            if n:
                e_past = jnp.exp(s[:, :past] - m)
                total = total + jnp.sum(e_past, axis=-1, keepdims=True)
                e = jnp.concatenate([e_past, e], axis=1)
            es.append(e)
            totals.append(total)
        a = (es[0] * (1.0 / totals[0]) - es[1] * (lam / totals[1])).astype(BF16)
        o = _bdot(a, v_ref[:tv, :])
        o_ref[...] = (_rms(o) * sg_ref[...] * (1.0 - lambda_init)).astype(o_ref.dtype)

    for n in range(SEQ // tq):
        pl.when(qi == n)(functools.partial(query_tile, n))


def _attn_prompt(q, ktb, vb, lq1, lk1, lq2, lk2, subln_g, lambda_init):
    nq = SEQ // ATTN_Q_TILE
    kern = functools.partial(_attn_kernel, lambda_init=lambda_init)
    vec = pl.BlockSpec((1, HEAD_DIM), lambda b, h, i: (0, 0))
    return pl.pallas_call(
        kern,
        grid=(BATCH, N_HEADS, nq),
        in_specs=[pl.BlockSpec((ATTN_Q_TILE, LANES), lambda b, h, i: (b * nq + i, h)),
                  pl.BlockSpec((None, LANES, SEQ), lambda b, h, i: (b, h, 0)),
                  pl.BlockSpec((SEQ, LANES), lambda b, h, i: (b, h)),
                  vec, vec, vec, vec,
                  pl.BlockSpec((1, 2 * HEAD_DIM), lambda b, h, i: (0, 0))],
        out_specs=pl.BlockSpec((ATTN_Q_TILE, LANES), lambda b, h, i: (b * nq + i, h)),
        out_shape=jax.ShapeDtypeStruct((N_PROMPT, D_MODEL), BF16),
        compiler_params=_params("arbitrary", "arbitrary", "arbitrary"),
        name="diff_attn_prompt",
    )(q, ktb, vb, lq1, lk1, lq2, lk2, subln_g)


DEC_ROWS = N_HEADS * 2 * DEC_SEQ


def _decode_kernel(pt_ref, q_ref, kn_ref, vn_ref, lq1_ref, lk1_ref, lq2_ref, lk2_ref, sg_ref, *rest,
                   lambda_init):
    k_refs = rest[:PAGES_PER_STEP]
    v_refs = rest[PAGES_PER_STEP:2 * PAGES_PER_STEP]
    o_ref, m_ref, l_ref, acc_ref = rest[2 * PAGES_PER_STEP:]
    step = pl.program_id(1)
    rows_per_head = 2 * DEC_SEQ
    lane = lax.broadcasted_iota(jnp.int32, (rows_per_head, D_MODEL), 1)
    comp = _div_pow2(lax.broadcasted_iota(jnp.int32, (rows_per_head, D_MODEL), 0), DEC_SEQ)
    q8 = q_ref[...] * (HEAD_DIM ** -0.5)
    blk = _div_pow2(lane, HEAD_DIM)
    qbd = jnp.concatenate([jnp.where(blk == 2 * hd + comp, q8, 0.0) for hd in range(N_HEADS)],
                          axis=0).astype(BF16)

    @pl.when(step == 0)
    def _():
        m_ref[...] = jnp.full(m_ref.shape, -jnp.inf, F32)
        l_ref[...] = jnp.zeros(l_ref.shape, F32)
        acc_ref[...] = jnp.zeros(acc_ref.shape, F32)

    m, l, acc = m_ref[...], l_ref[...], acc_ref[...]
    s = jnp.concatenate([_bdot(qbd, kr[...].astype(BF16)) for kr in k_refs], axis=1)
    m_new = jnp.maximum(m, jnp.max(s, axis=-1, keepdims=True))
    alpha = jnp.exp(m - m_new)
    p = jnp.exp(s - m_new)
    l = l * alpha + jnp.sum(p, axis=-1, keepdims=True)
    pv = []
    for hd in range(N_HEADS):
        rows = slice(hd * rows_per_head, (hd + 1) * rows_per_head)
        v_h = jnp.concatenate([vr[pl.ds(hd, PAGE_SIZE, stride=N_HEADS), :].astype(BF16) for vr in v_refs],
                              axis=0)
        pv.append(_bdot(p[rows].astype(BF16), v_h))
    acc = acc * alpha + jnp.concatenate(pv, axis=0)
    m = m_new
    m_ref[...], l_ref[...], acc_ref[...] = m, l, acc

    @pl.when(step == pl.num_programs(1) - 1)
    def _():
        kn = kn_ref[...].astype(BF16).astype(F32)
        vn = vn_ref[...].astype(BF16).astype(F32)
        qf = qbd.astype(F32)
        row_q = _mod_pow2(lax.broadcasted_iota(jnp.int32, (DEC_ROWS, 1), 0), DEC_SEQ)
        s_new = []
        for j in range(DEC_SEQ):
            sj = jnp.sum(qf * kn[j:j + 1, :], axis=-1, keepdims=True)
            s_new.append(jnp.where(row_q >= j, sj, -jnp.inf))
        m2 = m
        for sj in s_new:
            m2 = jnp.maximum(m2, sj)
        alpha = jnp.exp(m - m2)
        l2 = l * alpha
        acc2 = acc * alpha
        for j, sj in enumerate(s_new):
            pj = jnp.exp(sj - m2)
            l2 = l2 + pj
            v_rows = jnp.concatenate(
                [jnp.broadcast_to(vn[j:j + 1, hd * LANES:(hd + 1) * LANES], (rows_per_head, LANES))
                 for hd in range(N_HEADS)], axis=0)
            acc2 = acc2 + pj.astype(BF16).astype(F32) * v_rows
        outn = acc2 / l2
        lam = _lambda(lq1_ref, lk1_ref, lq2_ref, lk2_ref, lambda_init)
        for hd in range(N_HEADS):
            r0 = hd * rows_per_head
            o = outn[r0:r0 + DEC_SEQ] - lam * outn[r0 + DEC_SEQ:r0 + rows_per_head]
            o_ref[:, hd * LANES:(hd + 1) * LANES] = _rms(o) * sg_ref[...] * (1.0 - lambda_init)


def _attn_decode(page_table, q8, k_new, v_new, cache_kt, cache_v, layer, lq1, lk1, lq2, lk2, subln_g, lambda_init):
    n_steps = N_PAGES // PAGES_PER_STEP
    kern = functools.partial(_decode_kernel, lambda_init=lambda_init)
    tok = pl.BlockSpec((None, DEC_SEQ, D_MODEL), lambda b, p, pt: (b, 0, 0))
    vec = pl.BlockSpec((1, HEAD_DIM), lambda b, p, pt: (0, 0))

    def page_spec(i):
        return pl.BlockSpec((None, None, N_HEADS * PAGE_SIZE, LANES),
                            lambda b, p, pt: (pt[b * N_PAGES + p * PAGES_PER_STEP + i], layer, 0, 0))

    grid_spec = pltpu.PrefetchScalarGridSpec(
        num_scalar_prefetch=1,
        grid=(DEC_BATCH, n_steps),
        in_specs=[pl.BlockSpec((None, 2 * DEC_SEQ, D_MODEL), lambda b, p, pt: (b, 0, 0)), tok, tok,
                  vec, vec, vec, vec, pl.BlockSpec((1, 2 * HEAD_DIM), lambda b, p, pt: (0, 0))]
                 + [page_spec(i) for i in range(PAGES_PER_STEP)] * 2,
        out_specs=tok,
        scratch_shapes=[pltpu.VMEM((DEC_ROWS, 1), F32), pltpu.VMEM((DEC_ROWS, 1), F32),
                        pltpu.VMEM((DEC_ROWS, LANES), F32)],
    )
    return pl.pallas_call(
        kern,
        grid_spec=grid_spec,
        out_shape=jax.ShapeDtypeStruct((DEC_BATCH, DEC_SEQ, D_MODEL), F32),
        compiler_params=_params("arbitrary", "arbitrary"),
        name="diff_attn_decode",
    )(page_table.reshape(-1), q8, k_new, v_new, lq1, lk1, lq2, lk2, subln_g,
      *([cache_kt] * PAGES_PER_STEP), *([cache_v] * PAGES_PER_STEP))


def _outproj_kernel(o_ref, x_ref, gt_ref, w_ref, xo_ref):
    xo_ref[...] = x_ref[...] + gt_ref[...] * _bdot(o_ref[...].astype(BF16), w_ref[...])


def _outproj_layer(st, o, x, mods, w_o):
    return pl.pallas_call(
        _outproj_kernel,
        grid=(st.grid,),
        in_specs=[st.rows_spec(), st.rows_spec(), st.mod_spec(2), _full_spec((D_MODEL, D_MODEL))],
        out_specs=st.rows_spec(),
        out_shape=jax.ShapeDtypeStruct((st.rows, D_MODEL), F32),
        compiler_params=_params("arbitrary"),
        name="attn_out_proj",
    )(o, x, mods, w_o)


N_ALL = N_PROMPT + N_SAMPLE
ROW_W = D_MODEL + LANES
PAIRS_PER_GROUP = EXPERTS_PER_GROUP * (EXPERTS_PER_GROUP - 1) // 2
N_BUCKETS = N_EXPERT_GROUPS * PAIRS_PER_GROUP
SORT_TILE = 256
N_SORT_TILES = (N_ALL + N_BUCKETS * (SORT_TILE - 1)) // SORT_TILE
N_SORT_ROWS = N_SORT_TILES * SORT_TILE
DMA_UNROLL = 8


def _route_store(x, route_in_refs, route_ref, gates_ref):
    sh_ref, sc_ref, ng_ref, rwt_ref, rb_ref = route_in_refs
    tm = x.shape[0]
    h = _modulate(x, ng_ref[...], sh_ref[...], sc_ref[...])
    logits = lax.dot_general(rwt_ref[...], h, (((1,), (1,)), ((), ())), preferred_element_type=F32,
                             precision=lax.Precision.HIGHEST)
    scores = jax.nn.sigmoid(logits)
    sel = scores + rb_ref[...]
    eid = lax.broadcasted_iota(jnp.int32, sel.shape, 0)
    grp = _div_pow2(eid, EXPERTS_PER_GROUP)
    neg = -jnp.inf

    def first_max(vals):
        m = jnp.max(vals, axis=0, keepdims=True)
        idx = jnp.min(jnp.where(vals == m, eid, N_EXPERTS), axis=0, keepdims=True)
        return m, idx

    best = None
    for g in range(N_EXPERT_GROUPS):
        vg = jnp.where(grp == g, sel, neg)
        m1, i1 = first_max(vg)
        m2, _ = first_max(jnp.where(eid == i1, neg, vg))
        gs = m1 + m2
        if best is None:
            best, best_score = jnp.zeros_like(i1), gs
        else:
            better = gs > best_score
            best = jnp.where(better, g, best)
            best_score = jnp.where(better, gs, best_score)
    masked = jnp.where(grp == best, sel, neg)
    _, ia = first_max(masked)
    _, ib = first_max(jnp.where(eid == ia, neg, masked))
    ga = jnp.sum(jnp.where(eid == ia, scores, 0.0), axis=0, keepdims=True)
    gb = jnp.sum(jnp.where(eid == ib, scores, 0.0), axis=0, keepdims=True)
    den = ga + gb
    a_first = ia < ib
    lo = jnp.where(a_first, ia, ib).astype(F32)
    hi = jnp.where(a_first, ib, ia).astype(F32)
    g_lo = jnp.where(a_first, ga, gb) / den
    g_hi = jnp.where(a_first, gb, ga) / den
    r8 = lax.broadcasted_iota(jnp.int32, route_ref.shape, 0)
    route_ref[...] = jnp.where(r8 == 0, lo, jnp.where(r8 == 1, hi, 0.0))
    r128 = lax.broadcasted_iota(jnp.int32, (LANES, tm), 0)
    gates_t = jnp.where(r128 == 0, g_lo, jnp.where(r128 == 1, g_hi, 0.0))
    gates_ref[...] = gates_t.T


ROUTE_ROWS = 8


class _Routing:
    def __init__(self, st, mods, ng, router_wt, router_b):
        self.args = [mods, mods, ng, router_wt, router_b]
        self.in_specs = [st.mod_spec(3), st.mod_spec(4), _full_spec((1, D_MODEL)),
                         _full_spec((N_EXPERTS, D_MODEL)), _full_spec((N_EXPERTS, 1))]
        self.out_specs = [pl.BlockSpec((ROUTE_ROWS, st.tile), lambda i: (0, i)), st.rows_spec(LANES)]
        self.out_shape = [jax.ShapeDtypeStruct((ROUTE_ROWS, st.rows), F32),
                          jax.ShapeDtypeStruct((st.rows, LANES), F32)]


N_ROUTE_IN = 5


def _router_kernel(x_ref, *refs):
    _route_store(x_ref[...], refs[:N_ROUTE_IN], *refs[N_ROUTE_IN:])


def _router_layer(st, x, routing):
    return pl.pallas_call(
        _router_kernel,
        grid=(st.grid,),
        in_specs=[st.rows_spec()] + routing.in_specs,
        out_specs=routing.out_specs,
        out_shape=routing.out_shape,
        compiler_params=_params("arbitrary"),
        name="moe_router",
    )(x, *routing.args)


def _moe_plan(route):
    lo = route[0].astype(jnp.int32)
    hi = route[1].astype(jnp.int32)
    i, j = lo % EXPERTS_PER_GROUP, hi % EXPERTS_PER_GROUP
    pair = i * (2 * EXPERTS_PER_GROUP - 1 - i) // 2 + (j - i - 1)
    bucket = (lo // EXPERTS_PER_GROUP) * PAIRS_PER_GROUP + pair
    onehot = (bucket[:, None] == jnp.arange(N_BUCKETS)[None, :]).astype(jnp.int32)
    csum = jnp.cumsum(onehot, axis=0)
    counts = csum[-1]
    rank = jnp.sum(onehot * csum, axis=1) - 1
    tiles = (counts + SORT_TILE - 1) // SORT_TILE
    ends = jnp.cumsum(tiles)
    starts = ends - tiles
    n_used = ends[-1]
    pos = jnp.sum(onehot * starts[None, :], axis=1) * SORT_TILE + rank
    tile_id = jnp.arange(N_SORT_TILES)
    tile_bucket = jnp.sum((jnp.minimum(tile_id, n_used - 1)[:, None] >= ends[None, :]).astype(jnp.int32), axis=1)
    tile_bucket = jnp.minimum(tile_bucket, N_BUCKETS - 1)
    pair_lo = jnp.array([a for a in range(EXPERTS_PER_GROUP) for _ in range(a + 1, EXPERTS_PER_GROUP)], jnp.int32)
    pair_hi = jnp.array([b for a in range(EXPERTS_PER_GROUP) for b in range(a + 1, EXPERTS_PER_GROUP)], jnp.int32)
    base = (tile_bucket // PAIRS_PER_GROUP) * EXPERTS_PER_GROUP
    tile_lo = base + pair_lo[tile_bucket % PAIRS_PER_GROUP]
    tile_hi = base + pair_hi[tile_bucket % PAIRS_PER_GROUP]
    return pos.astype(jnp.int32), tile_lo, tile_hi, n_used.reshape(1).astype(jnp.int32)


def _bucket_kernel(pos_ref, x_ref, sh_ref, sc_ref, ng_ref, gates_ref, dst_in_ref, dst_ref, buf, sem, *, row0):
    del dst_in_ref
    tm = x_ref.shape[0]
    t = pl.program_id(0)
    nt = pl.num_programs(0)
    slot = t % 2

    def wait(s):
        pltpu.make_async_copy(buf.at[s], dst_ref.at[pl.ds(0, tm), :], sem.at[s]).wait()

    @pl.when(t >= 2)
    def _():
        wait(slot)

    buf[slot, :, :D_MODEL] = _modulate(x_ref[...], ng_ref[...], sh_ref[...], sc_ref[...])
    buf[slot, :, D_MODEL:] = gates_ref[...]
    base = row0 + t * tm

    def body(i, carry):
        for u in range(DMA_UNROLL):
            r = i * DMA_UNROLL + u
            pltpu.make_async_copy(buf.at[slot, pl.ds(r, 1), :],
                                  dst_ref.at[pl.ds(pos_ref[base + r], 1), :],
                                  sem.at[slot]).start(priority=u % 2)
        return carry

    lax.fori_loop(0, tm // DMA_UNROLL, body, 0)

    @pl.when(t == nt - 1)
    def _():
        wait(slot)

    @pl.when((t == nt - 1) & (nt > 1))
    def _():
        wait(1 - slot)


def _bucket_rows(st, pos, x, mods, ng, gates, dst, row0):
    kern = functools.partial(_bucket_kernel, row0=row0)
    if st.per_row_mods:
        def mod_spec(k):
            return pl.BlockSpec((st.tile, D_MODEL), lambda i, p: (i, k))
    else:
        tpb = st.tiles_per_batch

        def mod_spec(k):
            return pl.BlockSpec((None, None, 1, D_MODEL), lambda i, p: (i // tpb, k, 0, 0))
    hbm = pl.BlockSpec(memory_space=pl.ANY)
    grid_spec = pltpu.PrefetchScalarGridSpec(
        num_scalar_prefetch=1,
        grid=(st.grid,),
        in_specs=[pl.BlockSpec((st.tile, D_MODEL), lambda i, p: (i, 0)), mod_spec(3), mod_spec(4),
                  pl.BlockSpec((1, D_MODEL), lambda i, p: (0, 0)),
                  pl.BlockSpec((st.tile, LANES), lambda i, p: (i, 0)), hbm],
        out_specs=hbm,
        scratch_shapes=[pltpu.VMEM((2, st.tile, ROW_W), F32), pltpu.SemaphoreType.DMA((2,))],
    )
    return pl.pallas_call(
        kern,
        grid_spec=grid_spec,
        out_shape=jax.ShapeDtypeStruct(dst.shape, dst.dtype),
        input_output_aliases={6: 0},
        compiler_params=_params("arbitrary"),
        name="moe_bucket_rows",
    )(pos, x, mods, mods, ng, gates, dst)


def _moe_kernel(lo_ref, hi_ref, nu_ref, xs_ref, wg_lo, wu_lo, wd_lo, wg_hi, wu_hi, wd_hi, ys_ref):
    del lo_ref, hi_ref
    t = pl.program_id(0)

    @pl.when(t < nu_ref[0])
    def _():
        row = xs_ref[...]
        x = row[:, :D_MODEL].astype(BF16)

        def hidden(wg, wu, gate):
            g = _bdot(x, wg[...])
            u = _bdot(x, wu[...])
            return (jax.nn.silu(g) * u * gate).astype(BF16)

        a_lo = hidden(wg_lo, wu_lo, row[:, D_MODEL:D_MODEL + 1])
        a_hi = hidden(wg_hi, wu_hi, row[:, D_MODEL + 1:D_MODEL + 2])
        a = jnp.concatenate([a_lo, a_hi], axis=1)
        wd = jnp.concatenate([wd_lo[...], wd_hi[...]], axis=0)
        ys_ref[...] = _bdot(a, wd)

    @pl.when(t >= nu_ref[0])
    def _():
        ys_ref[...] = jnp.zeros(ys_ref.shape, ys_ref.dtype)


def _moe_experts(xs, tile_lo, tile_hi, n_used, wg, wu, wd, layer):
    def w_spec(shape, which):
        return pl.BlockSpec((None, None) + shape,
                            lambda t, lo, hi, nu: (layer, (lo, hi)[which][t], 0, 0))

    up, down = (D_MODEL, D_EXPERT), (D_EXPERT, D_MODEL)
    grid_spec = pltpu.PrefetchScalarGridSpec(
        num_scalar_prefetch=3,
        grid=(N_SORT_TILES,),
        in_specs=[pl.BlockSpec((SORT_TILE, ROW_W), lambda t, lo, hi, nu: (t, 0)),
                  w_spec(up, 0), w_spec(up, 0), w_spec(down, 0),
                  w_spec(up, 1), w_spec(up, 1), w_spec(down, 1)],
        out_specs=pl.BlockSpec((SORT_TILE, D_MODEL), lambda t, lo, hi, nu: (t, 0)),
    )
    return pl.pallas_call(
        _moe_kernel,
        grid_spec=grid_spec,
        out_shape=jax.ShapeDtypeStruct((N_SORT_ROWS, D_MODEL), F32),
        compiler_params=_params("arbitrary"),
        name="moe_experts",
    )(tile_lo, tile_hi, n_used, xs, wg, wu, wd, wg, wu, wd)


def _unsort_kernel(pos_ref, ys_ref, x_ref, g2_ref, fg_ref, o_ref, buf, sem, *, row0, final):
    tm = x_ref.shape[0]
    t = pl.program_id(0)

    def fetch(tile, slot):
        base = row0 + tile * tm

        def body(i, carry):
            for u in range(DMA_UNROLL):
                r = i * DMA_UNROLL + u
                pltpu.make_async_copy(ys_ref.at[pl.ds(pos_ref[base + r], 1), :],
                                      buf.at[slot, pl.ds(r, 1), :],
                                      sem.at[slot]).start(priority=u % 2)
            return carry

        lax.fori_loop(0, tm // DMA_UNROLL, body, 0)

    @pl.when(t == 0)
    def _():
        fetch(0, 0)

    @pl.when(t + 1 < pl.num_programs(0))
    def _():
        fetch(t + 1, (t + 1) % 2)

    slot = t % 2
    pltpu.make_async_copy(ys_ref.at[pl.ds(0, tm), :], buf.at[slot], sem.at[slot]).wait()
    y = x_ref[...] + g2_ref[...] * buf[slot]
    o_ref[...] = _rms(y) * fg_ref[...] if final else y


def _unsort_layer(st, pos, ys, x, mods, row0, final_g, final):
    kern = functools.partial(_unsort_kernel, row0=row0, final=final)
    if st.per_row_mods:
        g2_spec = pl.BlockSpec((st.tile, D_MODEL), lambda i, p: (i, 5))
    else:
        tpb = st.tiles_per_batch
        g2_spec = pl.BlockSpec((None, None, 1, D_MODEL), lambda i, p: (i // tpb, 5, 0, 0))
    rows = pl.BlockSpec((st.tile, D_MODEL), lambda i, p: (i, 0))
    grid_spec = pltpu.PrefetchScalarGridSpec(
        num_scalar_prefetch=1,
        grid=(st.grid,),
        in_specs=[pl.BlockSpec(memory_space=pl.ANY), rows, g2_spec,
                  pl.BlockSpec((1, D_MODEL), lambda i, p: (0, 0))],
        out_specs=rows,
        scratch_shapes=[pltpu.VMEM((2, st.tile, D_MODEL), F32), pltpu.SemaphoreType.DMA((2,))],
    )
    return pl.pallas_call(
        kern,
        grid_spec=grid_spec,
        out_shape=jax.ShapeDtypeStruct((st.rows, D_MODEL), F32),
        compiler_params=_params("arbitrary"),
        name="moe_unsort_residual",
    )(pos, ys, x, mods, final_g)


def kernel(x_prompt, x_sample, cache_k, cache_v, state_pool, page_table, c_prompt, c_sample,
           norm1_g, norm2_g, ada_w, ada_b, a_w_in, a_ln_g, a_ln_b, a_w_s, a_b_s, a_w_out,
           b_w_grp, b_scale, c_w_qkv, c_lq1, c_lk1, c_lq2, c_lk2, c_subln_g, c_w_o,
           router_w, router_bias, e_w_gate, e_w_up, e_w_down, final_g):
    sp = _Stream(N_PROMPT, PROMPT_TILE, SEQ // PROMPT_TILE, per_row_mods=False)
    ss = _Stream(N_SAMPLE, N_SAMPLE, 1, per_row_mods=True)
    sr = _Stream(N_PROMPT, ROUTER_TILE, SEQ // ROUTER_TILE, per_row_mods=False)

    xp = x_prompt.reshape(N_PROMPT, D_MODEL)
    xs = x_sample.reshape(N_SAMPLE, D_MODEL)
    mods_all = _ada(jnp.concatenate([c_prompt, c_sample], axis=0), ada_w, ada_b)

    wg_b, wu_b, wd_b = e_w_gate.astype(BF16), e_w_up.astype(BF16), e_w_down.astype(BF16)
    rwt = router_w.T
    rb = router_bias.reshape(N_EXPERTS, 1)
    fg = final_g.reshape(1, D_MODEL)
    bucketed = jnp.zeros((N_SORT_ROWS, ROW_W), F32)
    n_phys = cache_k.shape[0]
    n_c_layers = cache_k.shape[1]
    ckt = jnp.transpose(cache_k, (0, 1, 3, 4, 5, 2)).reshape(n_phys, n_c_layers, D_MODEL, PAGE_SIZE)
    cv = cache_v.reshape(n_phys, n_c_layers, PAGE_SIZE * N_HEADS, 2 * HEAD_DIM)

    av_p, av_s, pr_p, pr_s, kp_l, vp_l, ks_l, vs_l = [], [], [], [], [], [], [], []
    for i in range(DEPTH):
        kind, j = i % N_MIXERS, i // N_MIXERS
        mp = mods_all[i, :BATCH].reshape(BATCH, N_MODS, 1, D_MODEL)
        ms = jnp.repeat(mods_all[i, BATCH:], DEC_SEQ, axis=0)
        ng1 = norm1_g[i].reshape(1, D_MODEL)
        ng2 = norm2_g[i].reshape(1, D_MODEL)
        if kind == 0:
            w_in = a_w_in[j].astype(BF16)
            w_out = a_w_out[j].astype(BF16)
            ln_g, ln_b = a_ln_g[j].reshape(1, D_GATE), a_ln_b[j].reshape(1, D_GATE)
            bias = jnp.repeat(a_b_s[j].T, D_GATE // A_GROUPS, axis=1)
            xp, vrow_p = _gmlp_layer(sp, xp, mp, ng1, w_in, ln_g, ln_b, a_w_s[j], bias, w_out, CHUNK, BATCH, False)
            reps = CHUNK // DEC_SEQ
            wmix_s = jnp.tile(a_w_s[j][:, :DEC_SEQ, :DEC_SEQ], (1, reps, reps))
            bias_s = jnp.tile(bias[:DEC_SEQ], (reps, 1))
            xs, vrow_s = _gmlp_layer(ss, xs, ms, ng1, a_w_in[j], ln_g, ln_b, wmix_s, bias_s, a_w_out[j],
                                     DEC_SEQ, 1, True)
            av_p.append(vrow_p.reshape(BATCH, CHUNK, D_GATE))
            av_s.append(vrow_s.reshape(DEC_BATCH, DEC_SEQ, D_GATE))
        elif kind == 1:
            w_grp = b_w_grp[j].astype(BF16)
            scale = b_scale[j].reshape(1, D_MODEL)
            hb = PROMPT_TILE // POOL_HALO
            halo_spec = pl.BlockSpec((POOL_HALO, D_MODEL), lambda t: (jnp.maximum(t * hb - 1, 0), 0))
            xp, rows_p = _pool_layer(sp, xp, xp, halo_spec, mp, ng1, w_grp, scale, False, 0, POOL_HALO, BATCH)
            pr_p.append(rows_p.reshape(BATCH, POOL_HALO, D_MODEL)[:, 1:])
            pad_t = 2 * DEC_SEQ
            s8 = _Stream(DEC_BATCH * pad_t, pad_t, 1, per_row_mods=True)
            xs8 = jnp.pad(xs.reshape(DEC_BATCH, DEC_SEQ, D_MODEL), ((0, 0), (0, DEC_SEQ), (0, 0)))
            ms8 = jnp.repeat(mods_all[i, BATCH:], pad_t, axis=0)
            prev = jnp.pad(state_pool[:, j], ((0, 0), (1, 0), (0, 0))).reshape(DEC_BATCH * POOL_HALO, D_MODEL)
            prev_spec = pl.BlockSpec((POOL_HALO, D_MODEL), lambda t: (t, 0))
            n_out = POOL_HALO + pad_t
            xs8, rows_s = _pool_layer(s8, xs8.reshape(-1, D_MODEL), prev, prev_spec, ms8, ng1, w_grp, scale,
                                      True, PAST_LEN, n_out, DEC_BATCH)
            xs = xs8.reshape(DEC_BATCH, pad_t, D_MODEL)[:, :DEC_SEQ].reshape(N_SAMPLE, D_MODEL)
            lo = 1 + DEC_SEQ
            pr_s.append(rows_s.reshape(DEC_BATCH, n_out, D_MODEL)[:, lo:lo + POOL_BUF])
        else:
            lambda_init = 0.8 - 0.6 * math.exp(-0.3 * i)
            w_qkv = c_w_qkv[j].astype(BF16)
            w_o = c_w_o[j].astype(BF16)
            lq1, lk1 = c_lq1[j].reshape(1, HEAD_DIM), c_lk1[j].reshape(1, HEAD_DIM)
            lq2, lk2 = c_lq2[j].reshape(1, HEAD_DIM), c_lk2[j].reshape(1, HEAD_DIM)
            sg = c_subln_g[j].reshape(1, 2 * HEAD_DIM)
            pos_p = jnp.arange(SEQ)
            cos_p, sin_p = _rope_angles(pos_p)
            qp, kt, ktb, vp, vpb = _qkv_t_layer(
                sp, xp, mp, ng1, w_qkv[:, :D_MODEL], w_qkv[:, D_MODEL:2 * D_MODEL].T, w_qkv[:, 2 * D_MODEL:],
                _rope_tables(pos_p), (cos_p.T, sin_p.T))
            op = _attn_prompt(qp, ktb, vpb, lq1, lk1, lq2, lk2, sg, lambda_init)
            xp = _outproj_layer(sp, op, xp, mp, w_o)
            tab_s = _rope_tables(jnp.tile(PAST_LEN + jnp.arange(DEC_SEQ), DEC_BATCH))
            qs, ks, vs = _qkv_layer(ss, xs, ms, ng1, w_qkv, tab_s,
                                    pl.BlockSpec((N_SAMPLE, LANES), lambda t: (0, 0)))
            tok = (DEC_BATCH, DEC_SEQ, D_MODEL)
            q8 = jnp.concatenate([qs.reshape(tok)] * 2, axis=1)
            os_ = _attn_decode(page_table, q8, ks.reshape(tok), vs.reshape(tok), ckt, cv, j,
                               lq1, lk1, lq2, lk2, sg, lambda_init)
            xs = _outproj_layer(ss, os_.reshape(N_SAMPLE, D_MODEL), xs, ms, w_o)
            kp = jnp.transpose(kt.reshape(BATCH, N_HEADS, 2, HEAD_DIM, SEQ), (0, 4, 1, 2, 3))
            kp_l.append(kp)
            vp_l.append(vp.reshape(BATCH, SEQ, N_HEADS, 2 * HEAD_DIM))
            ks_l.append(ks.reshape(DEC_BATCH, DEC_SEQ, N_HEADS, 2, HEAD_DIM))
            vs_l.append(vs.reshape(DEC_BATCH, DEC_SEQ, N_HEADS, 2 * HEAD_DIM))

        route_p, gates_p = _router_layer(sr, xp, _Routing(sr, mp, ng2, rwt, rb))
        route_s, gates_s = _router_layer(ss, xs, _Routing(ss, ms, ng2, rwt, rb))
        pos, tile_lo, tile_hi, n_used = _moe_plan(jnp.concatenate([route_p, route_s], axis=1))
        bucketed = _bucket_rows(sr, pos, xp, mp, ng2, gates_p, bucketed, 0)
        bucketed = _bucket_rows(ss, pos, xs, ms, ng2, gates_s, bucketed, N_PROMPT)
        ys = _moe_experts(bucketed, tile_lo, tile_hi, n_used, wg_b, wu_b, wd_b, i)
        last = i == DEPTH - 1
        xp = _unsort_layer(sr, pos, ys, xp, mp, 0, fg, last)
        xs = _unsort_layer(ss, pos, ys, xs, ms, N_PROMPT, fg, last)

    y_prompt = xp.reshape(BATCH, SEQ, D_MODEL)
    y_sample = xs.reshape(DEC_BATCH, DEC_SEQ, D_MODEL)
    return (y_prompt, y_sample,
            jnp.stack(av_p, axis=1), jnp.stack(av_s, axis=1),
            jnp.stack(pr_p, axis=1), jnp.stack(pr_s, axis=1),
            jnp.stack(kp_l, axis=1), jnp.stack(vp_l, axis=1),
            jnp.stack(ks_l, axis=1), jnp.stack(vs_l, axis=1))
```
